```python
import math
import jax
import jax.numpy as jnp
from jax import lax
import numpy as np

D_MODEL = 1024
BATCH = 8
SEQ = 4096
DEPTH = 2
DEC_BATCH = 32
DEC_SEQ = 8
PAST_LEN = 16384
PAGE_SIZE = 128

N_EVEN = (DEPTH + 1) // 2
N_ODD = DEPTH // 2
H_A = 4
DK_A = 128
DV_A = 128
W_A = H_A * DV_A
MLSTM_CHUNK = 128
H_B = 8
DH_B = 64
W_B = H_B * DH_B
H_IDX = 8
D_IDX = 64
TOPK_MAX = 256
Q_BLOCK = 128
W_C = 512
N_BLK_C = 8
BLK_C = W_C // N_BLK_C
CONV_W = 4
LRU_C = 8.0
H_D = 4
DK_D = 128
DV_D = 128
W_D = H_D * DV_D
RET_CHUNK = 128
ROPE_BASE = 10000.0
N_EXPERTS = 16
N_GROUPS = 4
EXP_PER_GROUP = N_EXPERTS // N_GROUPS
TOP_K_EXP = 2
D_FF_EXP = 512
ALPHA = (2 * DEPTH) ** 0.25
BETA = (8 * DEPTH) ** -0.25
LN_EPS = 1e-5
HN_EPS = 1e-6
SPLIT_E = (W_A, W_A, W_A, W_A, 2 * H_A, W_B, W_B, W_B, H_IDX * D_IDX, D_IDX, H_IDX)
SPLIT_O = (W_C, W_C, W_D, W_D, W_D, W_D)
IN_E = sum(SPLIT_E)
IN_O = sum(SPLIT_O)
MIX_E = W_A + W_B
MIX_O = W_C + W_D

kernel_name = 'hybrid_mlstm_dsa_rglru_retention_moe_step'

F32 = jnp.float32


def split_cols(a, sizes):
    out, s = [], 0
    for n in sizes:
        out.append(a[..., s:s + n])
        s += n
    return out


def layer_norm(x, g, b):
    xf = x.astype(F32)
    mu = jnp.mean(xf, -1, keepdims=True)
    var = jnp.mean(jnp.square(xf - mu), -1, keepdims=True)
    return ((xf - mu) * lax.rsqrt(var + LN_EPS) * g.astype(F32) + b.astype(F32)).astype(x.dtype)


def head_norm(h, g):
    mu = jnp.mean(h, -1, keepdims=True)
    var = jnp.mean(jnp.square(h - mu), -1, keepdims=True)
    y = (h - mu) * lax.rsqrt(var + HN_EPS)
    return y.reshape(h.shape[:2] + (-1,)) * g.astype(F32)


def to_heads(t, d):
    B, T = t.shape[:2]
    return t.reshape(B, T, -1, d).transpose(0, 2, 1, 3).astype(F32)


def to_chunks(a, L):
    B, H, T = a.shape[:3]
    a = a.reshape((B, H, T // L, L) + a.shape[3:])
    return jnp.moveaxis(a, 2, 0)


def from_chunks(a):
    a = jnp.moveaxis(a, 0, 2)
    return a.reshape(a.shape[:2] + (-1,) + a.shape[4:])


def mlstm(q, k, v, ig, lf, C0, n0, m0):
    T = q.shape[2]
    L = math.gcd(T, MLSTM_CHUNK)
    causal = jnp.tril(jnp.ones((L, L), dtype=bool))

    def step(carry, inp):
        C, n, m = carry
        qc, kc, vc, ic, fc = inp
        F = jnp.cumsum(fc, axis=-1)
        m_t = F + jnp.maximum(m[..., None], lax.cummax(ic - F, axis=2))
        log_d = F[..., :, None] - F[..., None, :] + ic[..., None, :] - m_t[..., :, None]
        dmat = jnp.exp(jnp.where(causal, log_d, -jnp.inf))
        inter = jnp.exp(F + m[..., None] - m_t)
        s = jnp.einsum('bhtd,bhsd->bhts', qc, kc) * dmat
        num = jnp.einsum('bhts,bhsv->bhtv', s, vc) + inter[..., None] * jnp.einsum('bhtd,bhdv->bhtv', qc, C)
        den = jnp.sum(s, -1) + inter * jnp.einsum('bhtd,bhd->bht', qc, n)
        h = num / jnp.maximum(jnp.abs(den), jnp.exp(-m_t))[..., None]
        m_new = m_t[..., -1]
        w_s = jnp.exp(F[..., -1:] - F + ic - m_new[..., None])
        decay = jnp.exp(F[..., -1] + m - m_new)
        C_new = decay[..., None, None] * C + jnp.einsum('bhsd,bhsv->bhdv', kc * w_s[..., None], vc)
        n_new = decay[..., None] * n + jnp.einsum('bhs,bhsd->bhd', w_s, kc)
        return (C_new, n_new, m_new), h

    carry, h = lax.scan(step, (C0, n0, m0), tuple(to_chunks(a, L) for a in (q, k, v, ig, lf)))
    return carry, from_chunks(h)


def retention(q, k, v, S0):
    T = q.shape[2]
    L = math.gcd(T, RET_CHUNK)
    lg = jnp.log1p(-jnp.exp2(-5.0 - jnp.arange(H_D, dtype=F32)))
    j = jnp.arange(L, dtype=F32)
    causal = jnp.tril(jnp.ones((L, L), dtype=bool))
    dmat = jnp.exp(jnp.where(causal, (j[:, None] - j[None, :]) * lg[:, None, None], -jnp.inf))
    qdec = jnp.exp((j + 1.0) * lg[:, None])
    kdec = jnp.exp((L - 1.0 - j) * lg[:, None])
    sdec = jnp.exp(L * lg)

    def step(S, inp):
        qc, kc, vc = inp
        att = jnp.einsum('bhtd,bhsd->bhts', qc, kc) * dmat
        o = jnp.einsum('bhts,bhsv->bhtv', att, vc) + qdec[..., None] * jnp.einsum('bhtd,bhdv->bhtv', qc, S)
        S_new = sdec[:, None, None] * S + jnp.einsum('bhsd,bhsv->bhdv', kc * kdec[..., None], vc)
        return S_new, o

    S1, o = lax.scan(step, S0, (to_chunks(q, L), to_chunks(k, L), to_chunks(v, L)))
    return S1, from_chunks(o)


def rotary(x, pos):
    half = x.shape[-1] // 2
    freq = ROPE_BASE ** (-jnp.arange(half, dtype=F32) / half)
    ang = pos[:, None] * freq[None, :]
    cos, sin = jnp.cos(ang), jnp.sin(ang)
    x1, x2 = x[..., :half], x[..., half:]
    return jnp.concatenate([x1 * cos - x2 * sin, x1 * sin + x2 * cos], -1)


def causal_conv(xpad, w, b):
    y = lax.conv_general_dilated(xpad.astype(F32), w.astype(F32)[:, None, :], (1,), 'VALID',
                                 dimension_numbers=('NWC', 'WIO', 'NWC'), feature_group_count=W_C)
    return y + b.astype(F32)


def lru_combine(left, right):
    a_l, b_l = left
    a_r, b_r = right
    return a_l * a_r, a_r * b_l + b_r


def rglru(xc, h0, wa, ba, wx, bx, lam):
    B, T, W = xc.shape
    xb = xc.reshape(B, T, N_BLK_C, BLK_C)
    r = jax.nn.sigmoid(jnp.einsum('btni,nij->btnj', xb, wa.astype(F32)).reshape(B, T, W) + ba.astype(F32))
    i = jax.nn.sigmoid(jnp.einsum('btni,nij->btnj', xb, wx.astype(F32)).reshape(B, T, W) + bx.astype(F32))
    log_a = -LRU_C * r * jax.nn.softplus(-lam.astype(F32))
    a = jnp.exp(log_a)
    u = jnp.sqrt(-jnp.expm1(2.0 * log_a)) * (i * xc)
    u = u.at[:, 0].add(a[:, 0] * h0)
    _, h = lax.associative_scan(lru_combine, (a, u), axis=1)
    return h, h[:, -1]


def gather_rows(arr, idx):
    return jax.vmap(lambda a, i: a[i])(arr, idx)


def gather_paged(pool, e, new, idx, page_table):
    P = page_table.shape[1] * PAGE_SIZE
    pidx = jnp.minimum(idx, P - 1)
    phys = jax.vmap(lambda pt, ii: pt[ii // PAGE_SIZE])(page_table, pidx)
    past_rows = pool[e, phys, pidx % PAGE_SIZE]
    new_rows = gather_rows(new, jnp.clip(idx - P, 0, new.shape[1] - 1)).astype(past_rows.dtype)
    in_past = (idx < P).reshape(idx.shape + (1,) * (past_rows.ndim - idx.ndim))
    return jnp.where(in_past, past_rows, new_rows)


def dsa(q, k, v, qi, ki, wi, past):
    B, T = q.shape[:2]
    if past is None:
        P = 0
        ki_all = ki
        fetch = lambda sel: (gather_rows(k, sel), gather_rows(v, sel))
    else:
        e, pool_k, pool_v, pool_ki, page_table = past
        P = page_table.shape[1] * PAGE_SIZE
        past_ki = pool_ki[e, page_table].reshape(B, P, D_IDX).astype(ki.dtype)
        ki_all = jnp.concatenate([past_ki, ki], axis=1)
        fetch = lambda sel: (gather_paged(pool_k, e, k, sel, page_table), gather_paged(pool_v, e, v, sel, page_table))
    L = P + T
    topk = min(TOPK_MAX, L // 4)
    qb = math.gcd(T, Q_BLOCK)
    kpos = jnp.arange(L)
    ki_f = ki_all.astype(F32)

    def block(i):
        s0 = i * qb
        qs = lax.dynamic_slice_in_dim(q, s0, qb, 1).astype(F32)
        qis = lax.dynamic_slice_in_dim(qi, s0, qb, 1).astype(F32)
        wis = lax.dynamic_slice_in_dim(wi, s0, qb, 1).astype(F32)
        qpos = P + s0 + jnp.arange(qb)
        dots = jax.nn.relu(jnp.einsum('bthi,bsi->bths', qis, ki_f))
        score = jnp.einsum('bths,bth->bts', dots, wis)
        score = jnp.where(kpos[None, None, :] <= qpos[None, :, None], score, -jnp.inf)
        _, sel = lax.top_k(score, topk)
        valid = sel <= qpos[None, :, None]
        kg, vg = fetch(sel)
        logits = jnp.einsum('bthd,btkhd->bthk', qs, kg.astype(F32)) * DH_B ** -0.5
        logits = jnp.where(valid[:, :, None, :], logits, -jnp.inf)
        p = jax.nn.softmax(logits, axis=-1)
        return jnp.einsum('bthk,btkhd->bthd', p, vg.astype(F32))

    out = lax.map(block, jnp.arange(T // qb))
    return jnp.moveaxis(out, 0, 1).reshape(B, T, W_B)


def moe(x, router_w, router_bias, w_gate, w_up, w_down):
    B, T, D = x.shape
    xf = x.reshape(-1, D)
    s = jax.nn.sigmoid((xf @ router_w).astype(F32))
    sel = s + router_bias.astype(F32)
    grp = sel.reshape(-1, N_GROUPS, EXP_PER_GROUP)
    g_score = jnp.sum(lax.top_k(grp, TOP_K_EXP)[0], -1)
    g = jnp.argmax(g_score, -1)
    in_grp = jnp.take_along_axis(grp, g[:, None, None], axis=1)[:, 0]
    _, loc = lax.top_k(in_grp, TOP_K_EXP)
    eidx = g[:, None] * EXP_PER_GROUP + loc
    s_sel = jnp.take_along_axis(s, eidx, axis=1)
    gates = s_sel / jnp.sum(s_sel, -1, keepdims=True)
    combine = jnp.sum(jax.nn.one_hot(eidx, N_EXPERTS, dtype=F32) * gates[..., None], 1)

    def expert(acc, inp):
        wg, wu, wd, c = inp
        h = jax.nn.silu(xf @ wg) * (xf @ wu)
        return acc + c[:, None].astype(x.dtype) * (h @ wd), None

    y, _ = lax.scan(expert, jnp.zeros_like(xf), (w_gate, w_up, w_down, combine.T))
    return y.reshape(B, T, D)


def even_mixer(x, e, p, st):
    B, T, _ = x.shape
    a_q, a_k, a_v, a_o, a_if, b_q, b_k, b_v, b_qi, b_ki, b_wi = split_cols(x @ p['w_in_e'][e], SPLIT_E)
    q = to_heads(a_q, DK_A)
    k = to_heads(a_k, DK_A) * DK_A ** -0.5
    v = to_heads(a_v, DV_A)
    gates = (a_if + p['b_if_e'][e]).astype(F32)
    ig = jnp.swapaxes(gates[..., :H_A], 1, 2)
    lf = jnp.swapaxes(jax.nn.log_sigmoid(gates[..., H_A:]), 1, 2)
    if st is None:
        C0 = jnp.zeros((B, H_A, DK_A, DV_A), F32)
        n0 = jnp.zeros((B, H_A, DK_A), F32)
        m0 = jnp.zeros((B, H_A), F32)
        past = None
    else:
        C0 = st['mlstm_C'][e].astype(F32)
        n0 = st['mlstm_n'][e].astype(F32)
        m0 = st['mlstm_m'][e].astype(F32)
        past = (e, st['cache_k'], st['cache_v'], st['cache_kidx'], st['page_table'])
    (C1, n1, m1), h = mlstm(q, k, v, ig, lf, C0, n0, m0)
    ya = head_norm(jnp.swapaxes(h, 1, 2), p['a_norm_g'][e]) * jax.nn.sigmoid(a_o.astype(F32))
    kb = b_k.reshape(B, T, H_B, DH_B)
    vb = b_v.reshape(B, T, H_B, DH_B)
    yb = dsa(b_q.reshape(B, T, H_B, DH_B), kb, vb, b_qi.reshape(B, T, H_IDX, D_IDX), b_ki, b_wi, past)
    y = jnp.concatenate([ya, yb], -1).astype(x.dtype) @ p['w_out_e'][e]
    new = {'mlstm_C': C1.astype(x.dtype), 'mlstm_n': n1.astype(x.dtype), 'mlstm_m': m1.astype(x.dtype),
           'k': kb, 'v': vb, 'kidx': b_ki}
    return y, new


def odd_mixer(x, o, p, st, pos0):
    B, T, _ = x.shape
    c_x, c_g, d_q, d_k, d_v, d_g = split_cols(x @ p['w_in_o'][o], SPLIT_O)
    if st is None:
        buf = jnp.zeros((B, CONV_W - 1, W_C), x.dtype)
        h0 = jnp.zeros((B, W_C), F32)
        S0 = jnp.zeros((B, H_D, DK_D, DV_D), F32)
    else:
        buf = st['conv'][o].astype(x.dtype)
        h0 = st['lru_h'][o].astype(F32)
        S0 = st['ret_S'][o].astype(F32)
    xpad = jnp.concatenate([buf, c_x], axis=1)
    xc = causal_conv(xpad, p['c_conv_w'][o], p['c_conv_b'][o])
    hs, hT = rglru(xc, h0, p['c_wa'][o], p['c_ba'][o], p['c_wx'][o], p['c_bx'][o], p['c_lambda'][o])
    yc = hs * jax.nn.gelu(c_g.astype(F32))
    pos = (pos0 + jnp.arange(T)).astype(F32)
    q = rotary(to_heads(d_q, DK_D), pos)
    k = rotary(to_heads(d_k, DK_D), pos) * DK_D ** -0.5
    v = to_heads(d_v, DV_D)
    S1, r = retention(q, k, v, S0)
    yd = head_norm(jnp.swapaxes(r, 1, 2), p['d_norm_g'][o]) * jax.nn.silu(d_g.astype(F32))
    y = jnp.concatenate([yc, yd], -1).astype(x.dtype) @ p['w_out_o'][o]
    new = {'conv': xpad[:, -(CONV_W - 1):], 'lru_h': hT.astype(x.dtype), 'ret_S': S1.astype(x.dtype)}
    return y, new


def forward(x, st, p):
    pos0 = 0 if st is None else st['page_table'].shape[1] * PAGE_SIZE
    new = {}
    for l in range(DEPTH):
        if l % 2 == 0:
            mix, ns = even_mixer(x, l // 2, p, st)
        else:
            mix, ns = odd_mixer(x, l // 2, p, st, pos0)
        for name, val in ns.items():
            new.setdefault(name, []).append(val)
        x = layer_norm(ALPHA * x + mix, p['ln_g'][l, 0], p['ln_b'][l, 0])
        ff = moe(x, p['router_w'], p['router_bias'], p['moe_w_gate'][l], p['moe_w_up'][l], p['moe_w_down'][l])
        x = layer_norm(ALPHA * x + ff, p['ln_g'][l, 1], p['ln_b'][l, 1])
    return x, {name: jnp.stack(vals) for name, vals in new.items()}


def setup_inputs(seed: int = 0) -> dict:
    key = jax.random.key(seed)
    keys = jax.random.split(key, 48)
    cnt = [0]

    def nxt():
        cnt[0] += 1
        return keys[cnt[0] - 1]

    def nrm(shape, scale):
        return jax.random.normal(nxt(), shape, F32) * scale

    n_pages = PAST_LEN // PAGE_SIZE
    n_pool = (DEC_BATCH * n_pages * 5) // 4
    inp = {}
    inp['x_prompt'] = nrm((BATCH, SEQ, D_MODEL), 1.0)
    inp['x_sample'] = nrm((DEC_BATCH, DEC_SEQ, D_MODEL), 1.0)
    inp['state_mlstm_C'] = nrm((N_EVEN, DEC_BATCH, H_A, DK_A, DV_A), 1.0)
    inp['state_mlstm_n'] = nrm((N_EVEN, DEC_BATCH, H_A, DK_A), 1.0)
    inp['state_mlstm_m'] = nrm((N_EVEN, DEC_BATCH, H_A), 1.0)
    inp['cache_k'] = nrm((N_EVEN, n_pool, PAGE_SIZE, H_B, DH_B), 1.0)
    inp['cache_v'] = nrm((N_EVEN, n_pool, PAGE_SIZE, H_B, DH_B), 1.0)
    inp['cache_kidx'] = nrm((N_EVEN, n_pool, PAGE_SIZE, D_IDX), 1.0)
    inp['page_table'] = jax.random.permutation(nxt(), n_pool)[:DEC_BATCH * n_pages].reshape(DEC_BATCH, n_pages).astype(jnp.int32)
    inp['state_conv'] = nrm((N_ODD, DEC_BATCH, CONV_W - 1, W_C), 1.0)
    inp['state_lru_h'] = nrm((N_ODD, DEC_BATCH, W_C), 0.5)
    inp['state_ret_S'] = nrm((N_ODD, DEC_BATCH, H_D, DK_D, DV_D), 1.0)
    inp['w_in_e'] = nrm((N_EVEN, D_MODEL, IN_E), D_MODEL ** -0.5)
    inp['b_if_e'] = jnp.concatenate([nrm((N_EVEN, H_A), 0.1),
                                     jnp.linspace(3.0, 6.0, H_A, dtype=F32)[None, :] + nrm((N_EVEN, H_A), 0.1)], -1)
    inp['a_norm_g'] = 1.0 + nrm((N_EVEN, W_A), 0.02)
    inp['w_out_e'] = nrm((N_EVEN, MIX_E, D_MODEL), BETA * MIX_E ** -0.5)
    inp['w_in_o'] = nrm((N_ODD, D_MODEL, IN_O), D_MODEL ** -0.5)
    inp['c_conv_w'] = nrm((N_ODD, CONV_W, W_C), CONV_W ** -0.5)
    inp['c_conv_b'] = nrm((N_ODD, W_C), 0.02)
    inp['c_wa'] = nrm((N_ODD, N_BLK_C, BLK_C, BLK_C), BLK_C ** -0.5)
    inp['c_ba'] = nrm((N_ODD, W_C), 0.02)
    inp['c_wx'] = nrm((N_ODD, N_BLK_C, BLK_C, BLK_C), BLK_C ** -0.5)
    inp['c_bx'] = nrm((N_ODD, W_C), 0.02)
    a0 = jax.random.uniform(nxt(), (N_ODD, W_C), F32, minval=0.9, maxval=0.999) ** (1.0 / LRU_C)
    inp['c_lambda'] = jnp.log(a0) - jnp.log1p(-a0)
    inp['d_norm_g'] = 1.0 + nrm((N_ODD, W_D), 0.02)
    inp['w_out_o'] = nrm((N_ODD, MIX_O, D_MODEL), BETA * MIX_O ** -0.5)
    inp['router_w'] = nrm((D_MODEL, N_EXPERTS), D_MODEL ** -0.5)
    inp['router_bias'] = nrm((N_EXPERTS,), 0.01)
    inp['moe_w_gate'] = nrm((DEPTH, N_EXPERTS, D_MODEL, D_FF_EXP), D_MODEL ** -0.5)
    inp['moe_w_up'] = nrm((DEPTH, N_EXPERTS, D_MODEL, D_FF_EXP), D_MODEL ** -0.5)
    inp['moe_w_down'] = nrm((DEPTH, N_EXPERTS, D_FF_EXP, D_MODEL), BETA * D_FF_EXP ** -0.5)
    inp['ln_g'] = 1.0 + nrm((DEPTH, 2, D_MODEL), 0.02)
    inp['ln_b'] = nrm((DEPTH, 2, D_MODEL), 0.02)
    return inp


def reference(x_prompt, x_sample, state_mlstm_C, state_mlstm_n, state_mlstm_m, cache_k, cache_v, cache_kidx,
              page_table, state_conv, state_lru_h, state_ret_S, w_in_e, b_if_e, a_norm_g, w_out_e, w_in_o,
              c_conv_w, c_conv_b, c_wa, c_ba, c_wx, c_bx, c_lambda, d_norm_g, w_out_o, router_w, router_bias,
              moe_w_gate, moe_w_up, moe_w_down, ln_g, ln_b):
    p = {'w_in_e': w_in_e, 'b_if_e': b_if_e, 'a_norm_g': a_norm_g, 'w_out_e': w_out_e, 'w_in_o': w_in_o,
         'c_conv_w': c_conv_w, 'c_conv_b': c_conv_b, 'c_wa': c_wa, 'c_ba': c_ba, 'c_wx': c_wx, 'c_bx': c_bx,
         'c_lambda': c_lambda, 'd_norm_g': d_norm_g, 'w_out_o': w_out_o, 'router_w': router_w,
         'router_bias': router_bias, 'moe_w_gate': moe_w_gate, 'moe_w_up': moe_w_up, 'moe_w_down': moe_w_down,
         'ln_g': ln_g, 'ln_b': ln_b}
    st = {'mlstm_C': state_mlstm_C, 'mlstm_n': state_mlstm_n, 'mlstm_m': state_mlstm_m,
          'cache_k': cache_k, 'cache_v': cache_v, 'cache_kidx': cache_kidx, 'page_table': page_table,
          'conv': state_conv, 'lru_h': state_lru_h, 'ret_S': state_ret_S}
    y_prompt, nsp = forward(x_prompt, None, p)
    y_sample, nss = forward(x_sample, st, p)
    return (y_prompt, y_sample,
            nsp['mlstm_C'], nsp['mlstm_n'], nsp['mlstm_m'], nsp['k'], nsp['v'], nsp['kidx'],
            nsp['conv'], nsp['lru_h'], nsp['ret_S'],
            nss['mlstm_C'], nss['mlstm_n'], nss['mlstm_m'], nss['k'], nss['v'], nss['kidx'],
            nss['conv'], nss['lru_h'], nss['ret_S'])
```

```python
import functools
import math

import jax
import jax.numpy as jnp
from jax import lax
from jax.experimental import pallas as pl
from jax.experimental.pallas import tpu as pltpu

F32 = jnp.float32
BF16 = jnp.bfloat16
I32 = jnp.int32

DEPTH = 2
PAGE_SIZE = 128
H_A, DK_A, DV_A = 4, 128, 128
W_A = H_A * DV_A
H_B, DH_B = 8, 64
W_B = H_B * DH_B
H_IDX, D_IDX = 8, 64
TOPK_MAX = 256
W_C, N_BLK_C, CONV_W, LRU_C = 512, 8, 4, 8.0
BLK_C = W_C // N_BLK_C
H_D, DK_D, DV_D = 4, 128, 128
W_D = H_D * DV_D
ROPE_BASE = 10000.0
N_EXPERTS, N_GROUPS, TOP_K_EXP, D_FF_EXP = 16, 4, 2, 512
EXP_PER_GROUP = N_EXPERTS // N_GROUPS
ALPHA = (2 * DEPTH) ** 0.25
LN_EPS = 1e-5
HN_EPS = 1e-6

CHUNK = 128
LANES = 128
SUBLANES = 8
NEG = -1e30
INT_MIN = -2 ** 31
VMEM_LIMIT = 56 * 1024 * 1024

S_KI = 0
S_IG = 64
S_FG = 68
S_WI = 72


def _cparams(*sem):
    return pltpu.CompilerParams(dimension_semantics=sem, vmem_limit_bytes=VMEM_LIMIT)


def _dot(a, b):
    return jnp.dot(a, b, preferred_element_type=F32)


def _dot_nt(a, b):
    return lax.dot_general(a, b, (((1,), (1,)), ((), ())), preferred_element_type=F32)


def _dot_tn(a, b):
    return lax.dot_general(a, b, (((0,), (0,)), ((), ())), preferred_element_type=F32)


def _row_block(n, target):
    t = min(n, target)
    while n % t:
        t //= 2
    return t


def _layer_norm(z, g, b):
    mu = jnp.mean(z, -1, keepdims=True)
    zc = z - mu
    var = jnp.mean(zc * zc, -1, keepdims=True)
    return zc * lax.rsqrt(var + LN_EPS) * g + b


def _head_norm(h):
    mu = jnp.mean(h, -1, keepdims=True)
    hc = h - mu
    var = jnp.mean(hc * hc, -1, keepdims=True)
    return hc * lax.rsqrt(var + HN_EPS)


def _inproj_e_kernel(x_ref, w_ref, bias_ref, a_ref, bq_ref, k_ref, v_ref, s_ref, kb_ref, vb_ref):
    x = x_ref[...].astype(BF16)

    def mm(lo, hi):
        return _dot(x, w_ref[:, lo:hi])

    a_ref[:, 0:W_A] = mm(0, W_A)
    a_ref[:, W_A:2 * W_A] = mm(W_A, 2 * W_A) * DK_A ** -0.5
    a_ref[:, 2 * W_A:4 * W_A] = mm(2 * W_A, 4 * W_A)
    o = 4 * W_A
    bq_ref[...] = mm(o, o + 2 * W_B)
    o += 2 * W_B
    k = mm(o, o + W_B)
    k_ref[...] = k
    kb_ref[...] = k.astype(BF16)
    o += W_B
    v = mm(o, o + W_B)
    v_ref[...] = v
    vb_ref[...] = v.astype(BF16)
    o += W_B
    s_ref[...] = mm(o, o + LANES) + bias_ref[...]


def _inproj_e(x, w, bias):
    n, d = x.shape
    tm = _row_block(n, 256)
    wcols = w.shape[1]
    outs = [(4 * W_A, F32), (2 * W_B, F32), (W_B, F32), (W_B, F32), (LANES, F32), (W_B, BF16), (W_B, BF16)]
    return pl.pallas_call(
        _inproj_e_kernel,
        grid=(n // tm,),
        in_specs=[pl.BlockSpec((tm, d), lambda i: (i, 0)),
                  pl.BlockSpec((d, wcols), lambda i: (0, 0)),
                  pl.BlockSpec((1, LANES), lambda i: (0, 0))],
        out_specs=[pl.BlockSpec((tm, c), lambda i: (i, 0)) for c, _ in outs],
        out_shape=[jax.ShapeDtypeStruct((n, c), dt) for c, dt in outs],
        compiler_params=_cparams("parallel"),
        name="inproj_even",
    )(x, w, bias)


def _inproj_o_kernel(x_ref, w_ref, c_ref, d_ref):
    x = x_ref[...].astype(BF16)
    c_ref[...] = _dot(x, w_ref[:, 0:2 * W_C])
    d_ref[...] = _dot(x, w_ref[:, 2 * W_C:2 * W_C + 4 * W_D])


def _inproj_o(x, w):
    n, d = x.shape
    tm = _row_block(n, 256)
    return pl.pallas_call(
        _inproj_o_kernel,
        grid=(n // tm,),
        in_specs=[pl.BlockSpec((tm, d), lambda i: (i, 0)),
                  pl.BlockSpec(w.shape, lambda i: (0, 0))],
        out_specs=[pl.BlockSpec((tm, 2 * W_C), lambda i: (i, 0)),
                   pl.BlockSpec((tm, 4 * W_D), lambda i: (i, 0))],
        out_shape=[jax.ShapeDtypeStruct((n, 2 * W_C), F32), jax.ShapeDtypeStruct((n, 4 * W_D), F32)],
        compiler_params=_cparams("parallel"),
        name="inproj_odd",
    )(x, w)


def _outproj_ln_kernel(y1_ref, y2_ref, w_ref, x_ref, g_ref, b_ref, o_ref):
    half = y1_ref.shape[1]
    y = _dot(y1_ref[...], w_ref[0:half, :]) + _dot(y2_ref[...], w_ref[half:2 * half, :])
    o_ref[...] = _layer_norm(ALPHA * x_ref[...] + y, g_ref[...], b_ref[...])


def _outproj_ln(y1, y2, w, x, g, b):
    n, d = x.shape
    tm = _row_block(n, 512)
    half = y1.shape[1]
    return pl.pallas_call(
        _outproj_ln_kernel,
        grid=(n // tm,),
        in_specs=[pl.BlockSpec((tm, half), lambda i: (i, 0)),
                  pl.BlockSpec((tm, half), lambda i: (i, 0)),
                  pl.BlockSpec(w.shape, lambda i: (0, 0)),
                  pl.BlockSpec((tm, d), lambda i: (i, 0)),
                  pl.BlockSpec((1, d), lambda i: (0, 0)),
                  pl.BlockSpec((1, d), lambda i: (0, 0))],
        out_specs=pl.BlockSpec((tm, d), lambda i: (i, 0)),
        out_shape=jax.ShapeDtypeStruct((n, d), F32),
        compiler_params=_cparams("parallel"),
        name="outproj_ln",
    )(y1, y2, w, x, g, b)


def _route(logits, bias):
    lane = lax.broadcasted_iota(I32, logits.shape, 1)
    valid = lane < N_EXPERTS
    pos = lane % EXP_PER_GROUP
    grp = (lane // EXP_PER_GROUP).astype(F32)
    s = jax.nn.sigmoid(logits)
    sel = jnp.where(valid, s + bias, NEG)
    rank = jnp.zeros(logits.shape, F32)
    for d in range(1, EXP_PER_GROUP):
        lo = pltpu.roll(sel, d, 1)
        hi = pltpu.roll(sel, LANES - d, 1)
        rank = rank + jnp.where(jnp.logical_and(pos >= d, lo >= sel), 1.0, 0.0)
        rank = rank + jnp.where(jnp.logical_and(pos + d < EXP_PER_GROUP, hi > sel), 1.0, 0.0)
    top2 = jnp.logical_and(rank < TOP_K_EXP, valid)
    contrib = jnp.where(top2, sel, 0.0)
    gs = contrib
    for d in range(1, EXP_PER_GROUP):
        lo = pltpu.roll(contrib, d, 1)
        hi = pltpu.roll(contrib, LANES - d, 1)
        gs = gs + jnp.where(pos >= d, lo, 0.0) + jnp.where(pos + d < EXP_PER_GROUP, hi, 0.0)
    gs = jnp.where(valid, gs, NEG)
    gmax = jnp.max(gs, axis=1, keepdims=True)
    best = jnp.min(jnp.where(gs == gmax, grp, 1e9), axis=1, keepdims=True)
    chosen = jnp.logical_and(top2, grp == best)
    s_sel = jnp.where(chosen, s, 0.0)
    return s_sel / jnp.sum(s_sel, axis=1, keepdims=True)


def _moe_kernel(x_ref, rw_ref, rb_ref, wg_ref, wu_ref, wd_ref, g_ref, b_ref, o_ref,
                xb_scr, comb_scr, acc_scr, *, sub):
    e = pl.program_id(1)
    tm = x_ref.shape[0]

    @pl.when(e == 0)
    def _():
        xb = x_ref[...].astype(BF16)
        xb_scr[...] = xb
        comb_scr[...] = _route(_dot(xb, rw_ref[...]), rb_ref[...])
        acc_scr[...] = jnp.zeros_like(acc_scr)

    def rows(r, carry):
        r0 = pl.multiple_of(r * sub, sub)
        xb = xb_scr[pl.ds(r0, sub), :]
        gate = _dot(xb, wg_ref[...])
        h = gate * jax.nn.sigmoid(gate) * _dot(xb, wu_ref[...])
        y = _dot(h.astype(BF16), wd_ref[...])
        comb = comb_scr[pl.ds(r0, sub), :]
        lane = lax.broadcasted_iota(I32, comb.shape, 1)
        c_e = jnp.sum(jnp.where(lane == e, comb, 0.0), axis=1, keepdims=True)
        acc_scr[pl.ds(r0, sub), :] += c_e * y
        return carry

    lax.fori_loop(0, tm // sub, rows, 0)

    @pl.when(e == N_EXPERTS - 1)
    def _():
        o_ref[...] = _layer_norm(ALPHA * x_ref[...] + acc_scr[...], g_ref[...], b_ref[...])


def _moe_ln(x, rw, rb, wg, wu, wd, g, b):
    n, d = x.shape
    tm = _row_block(n, 1024)
    sub = _row_block(tm, 256)
    f = wg.shape[2]
    return pl.pallas_call(
        functools.partial(_moe_kernel, sub=sub),
        grid=(n // tm, N_EXPERTS),
        in_specs=[pl.BlockSpec((tm, d), lambda i, e: (i, 0)),
                  pl.BlockSpec((d, LANES), lambda i, e: (0, 0)),
                  pl.BlockSpec((1, LANES), lambda i, e: (0, 0)),
                  pl.BlockSpec((None, d, f), lambda i, e: (e, 0, 0)),
                  pl.BlockSpec((None, d, f), lambda i, e: (e, 0, 0)),
                  pl.BlockSpec((None, f, d), lambda i, e: (e, 0, 0)),
                  pl.BlockSpec((1, d), lambda i, e: (0, 0)),
                  pl.BlockSpec((1, d), lambda i, e: (0, 0))],
        out_specs=pl.BlockSpec((tm, d), lambda i, e: (i, 0)),
        out_shape=jax.ShapeDtypeStruct((n, d), F32),
        scratch_shapes=[pltpu.VMEM((tm, d), BF16), pltpu.VMEM((tm, LANES), F32), pltpu.VMEM((tm, d), F32)],
        compiler_params=_cparams("parallel", "arbitrary"),
        name="moe_ln",
    )(x, rw, rb, wg, wu, wd, g, b)


def _mlstm_kernel(a_ref, s_ref, g_ref, c0_ref, n0_ref, m0_ref, y_ref, c1_ref, n1_ref, m1_ref,
                  c_scr, n_scr, m_scr):
    c = pl.program_id(1)
    L = a_ref.shape[0]

    @pl.when(c == 0)
    def _():
        c_scr[...] = c0_ref[...]
        n_scr[...] = n0_ref[...]
        m_scr[...] = m0_ref[...]

    S = s_ref[...]
    lane = lax.broadcasted_iota(I32, S.shape, 1)
    is_f = jnp.logical_and(lane >= S_FG, lane < S_FG + H_A)
    lf = jnp.where(is_f, jax.nn.log_sigmoid(S), 0.0)
    row = lax.broadcasted_iota(I32, (L, L), 0)
    col = lax.broadcasted_iota(I32, (L, L), 1)
    causal = row >= col
    Fs = jnp.dot(causal.astype(F32), lf, precision=lax.Precision.HIGHEST, preferred_element_type=F32)
    Fa = pltpu.roll(Fs, LANES - (S_FG - S_IG), 1)
    AT = jnp.transpose(S - Fa)

    for h in range(H_A):
        ig = S[:, S_IG + h:S_IG + h + 1]
        F = Fa[:, S_IG + h:S_IG + h + 1]
        a_row = AT[S_IG + h:S_IG + h + 1, :]
        m_prev = m_scr[h:h + 1, 0:1]
        cm = jnp.max(jnp.where(causal, a_row, NEG), axis=1, keepdims=True)
        m_t = F + jnp.maximum(m_prev, cm)
        dmat = jnp.exp(jnp.where(causal, (F - m_t) + a_row, NEG))
        inter = jnp.exp(F + m_prev - m_t)
        q = a_ref[:, h * DK_A:(h + 1) * DK_A].astype(BF16)
        kf = a_ref[:, W_A + h * DK_A:W_A + (h + 1) * DK_A]
        k = kf.astype(BF16)
        v = a_ref[:, 2 * W_A + h * DV_A:2 * W_A + (h + 1) * DV_A].astype(BF16)
        C = c_scr[h]
        n = n_scr[h:h + 1, :]
        s = _dot_nt(q, k) * dmat
        num = _dot(s.astype(BF16), v) + inter * _dot(q, C.astype(BF16))
        qn = jnp.sum(q.astype(F32) * n.astype(BF16).astype(F32), axis=1, keepdims=True)
        den = jnp.sum(s, axis=1, keepdims=True) + inter * qn
        hout = num / jnp.maximum(jnp.abs(den), jnp.exp(-m_t))
        o = a_ref[:, 3 * W_A + h * DV_A:3 * W_A + (h + 1) * DV_A]
        y = _head_norm(hout) * g_ref[:, h * DV_A:(h + 1) * DV_A] * jax.nn.sigmoid(o)
        y_ref[:, h * DV_A:(h + 1) * DV_A] = y.astype(y_ref.dtype)
        m_new = m_t[L - 1:L, :]
        F_last = F[L - 1:L, :]
        w_s = jnp.exp(F_last - F + ig - m_new)
        decay = jnp.exp(F_last + m_prev - m_new)
        kw = kf * w_s
        c_scr[h] = decay * C + _dot_tn(kw.astype(BF16), v)
        n_scr[h:h + 1, :] = decay * n + jnp.sum(kw, axis=0, keepdims=True)
        m_scr[h:h + 1, :] = jnp.broadcast_to(m_new, (1, LANES))

    @pl.when(c == pl.num_programs(1) - 1)
    def _():
        c1_ref[...] = c_scr[...]
        n1_ref[...] = n_scr[...]
        m1_ref[...] = m_scr[...]


def _mlstm(a, s, gnorm, c0, n0, m0b, nb, nc):
    n = a.shape[0]
    L = CHUNK
    return pl.pallas_call(
        _mlstm_kernel,
        grid=(nb, nc),
        in_specs=[pl.BlockSpec((L, 4 * W_A), lambda b, c: (b * nc + c, 0)),
                  pl.BlockSpec((L, LANES), lambda b, c: (b * nc + c, 0)),
                  pl.BlockSpec((1, W_A), lambda b, c: (0, 0)),
                  pl.BlockSpec((None, H_A, DK_A, DV_A), lambda b, c: (b, 0, 0, 0)),
                  pl.BlockSpec((None, H_A, DK_A), lambda b, c: (b, 0, 0)),
                  pl.BlockSpec((None, H_A, LANES), lambda b, c: (b, 0, 0))],
        out_specs=[pl.BlockSpec((L, W_A), lambda b, c: (b * nc + c, 0)),
                   pl.BlockSpec((None, H_A, DK_A, DV_A), lambda b, c: (b, 0, 0, 0)),
                   pl.BlockSpec((None, H_A, DK_A), lambda b, c: (b, 0, 0)),
                   pl.BlockSpec((None, H_A, LANES), lambda b, c: (b, 0, 0))],
        out_shape=[jax.ShapeDtypeStruct((n, W_A), BF16),
                   jax.ShapeDtypeStruct((nb, H_A, DK_A, DV_A), F32),
                   jax.ShapeDtypeStruct((nb, H_A, DK_A), F32),
                   jax.ShapeDtypeStruct((nb, H_A, LANES), F32)],
        scratch_shapes=[pltpu.VMEM((H_A, DK_A, DV_A), F32), pltpu.VMEM((H_A, DK_A), F32),
                        pltpu.VMEM((H_A, LANES), F32)],
        compiler_params=_cparams("parallel", "arbitrary"),
        name="mlstm",
    )(a, s, gnorm, c0, n0, m0b)


def _retention_kernel(d_ref, cos_ref, sin_ref, dm_ref, dec_ref, g_ref, s0_ref, y_ref, s1_ref, s_scr):
    c = pl.program_id(1)

    @pl.when(c == 0)
    def _():
        s_scr[...] = s0_ref[...]

    cos2 = cos_ref[...]
    sin2 = sin_ref[...]
    for h in range(H_D):
        qf = d_ref[:, h * DK_D:(h + 1) * DK_D]
        kf = d_ref[:, W_D + h * DK_D:W_D + (h + 1) * DK_D]
        q = qf * cos2 + pltpu.roll(qf, DK_D // 2, 1) * sin2
        k = (kf * cos2 + pltpu.roll(kf, DK_D // 2, 1) * sin2) * DK_D ** -0.5
        v = d_ref[:, 2 * W_D + h * DV_D:2 * W_D + (h + 1) * DV_D].astype(BF16)
        qdec = dec_ref[h, :, 0:1]
        kdec = dec_ref[h, :, 1:2]
        sdec = dec_ref[h, 0:1, 2:3]
        S = s_scr[h]
        qb = q.astype(BF16)
        att = _dot_nt(qb, k.astype(BF16)) * dm_ref[h]
        o = _dot(att.astype(BF16), v) + qdec * _dot(qb, S.astype(BF16))
        s_scr[h] = sdec * S + _dot_tn((k * kdec).astype(BF16), v)
        gt = d_ref[:, 3 * W_D + h * DV_D:3 * W_D + (h + 1) * DV_D]
        y = _head_norm(o) * g_ref[:, h * DV_D:(h + 1) * DV_D] * (gt * jax.nn.sigmoid(gt))
        y_ref[:, h * DV_D:(h + 1) * DV_D] = y.astype(y_ref.dtype)

    @pl.when(c == pl.num_programs(1) - 1)
    def _():
        s1_ref[...] = s_scr[...]


def _retention(d, cos2, sin2, dmat, dec, gnorm, s0, nb, nc):
    n = d.shape[0]
    L = CHUNK
    return pl.pallas_call(
        _retention_kernel,
        grid=(nb, nc),
        in_specs=[pl.BlockSpec((L, 4 * W_D), lambda b, c: (b * nc + c, 0)),
                  pl.BlockSpec((L, DK_D), lambda b, c: (c, 0)),
                  pl.BlockSpec((L, DK_D), lambda b, c: (c, 0)),
                  pl.BlockSpec((H_D, L, L), lambda b, c: (0, 0, 0)),
                  pl.BlockSpec((H_D, L, LANES), lambda b, c: (0, 0, 0)),
                  pl.BlockSpec((1, W_D), lambda b, c: (0, 0)),
                  pl.BlockSpec((None, H_D, DK_D, DV_D), lambda b, c: (b, 0, 0, 0))],
        out_specs=[pl.BlockSpec((L, W_D), lambda b, c: (b * nc + c, 0)),
                   pl.BlockSpec((None, H_D, DK_D, DV_D), lambda b, c: (b, 0, 0, 0))],
        out_shape=[jax.ShapeDtypeStruct((n, W_D), BF16),
                   jax.ShapeDtypeStruct((nb, H_D, DK_D, DV_D), F32)],
        scratch_shapes=[pltpu.VMEM((H_D, DK_D, DV_D), F32)],
        compiler_params=_cparams("parallel", "arbitrary"),
        name="retention",
    )(d, cos2, sin2, dmat, dec, gnorm, s0)


def _shift_rows(x, prev, j):
    tb = x.shape[0]
    xs = pltpu.roll(x, j, 0)
    pr = pltpu.roll(prev, j, 0)
    row = lax.broadcasted_iota(I32, pr.shape, 0)
    first = jnp.where(row < j, pr, xs[0:SUBLANES])
    if tb == SUBLANES:
        return first
    return jnp.concatenate([first, xs[SUBLANES:]], axis=0)


def _rglru_kernel(c_ref, cw_ref, cb_ref, wa_ref, ba_ref, wx_ref, bx_ref, lam_ref, conv0_ref, h0_ref,
                  y_ref, hl_ref, prev_scr, h_scr):
    t = pl.program_id(1)
    tb = c_ref.shape[0]

    @pl.when(t == 0)
    def _():
        prev_scr[...] = conv0_ref[...]
        h_scr[...] = h0_ref[...]

    x = c_ref[:, 0:W_C]
    gate = c_ref[:, W_C:2 * W_C]
    prev = prev_scr[...]
    xc = x * cw_ref[CONV_W - 1:CONV_W, :] + cb_ref[...]
    for j in range(1, CONV_W):
        xc = xc + _shift_rows(x, prev, j) * cw_ref[CONV_W - 1 - j:CONV_W - j, :]
    prev_scr[...] = x[tb - SUBLANES:tb]

    xb = xc.astype(BF16)
    r = jax.nn.sigmoid(_dot(xb, wa_ref[...]) + ba_ref[...])
    i = jax.nn.sigmoid(_dot(xb, wx_ref[...]) + bx_ref[...])
    log_a = -LRU_C * r * jax.nn.softplus(-lam_ref[...])
    A = jnp.exp(log_a)
    th = jnp.tanh(log_a)
    U = jnp.sqrt(-2.0 * th / (1.0 - th)) * (i * xc)
    row = lax.broadcasted_iota(I32, (tb, W_C), 0)
    d = 1
    while d < tb:
        keep = row >= d
        U = jnp.where(keep, U + A * pltpu.roll(U, d, 0), U)
        A = jnp.where(keep, A * pltpu.roll(A, d, 0), A)
        d *= 2
    h = U + A * h_scr[0:1, :]
    h_scr[...] = jnp.broadcast_to(h[tb - 1:tb, :], (SUBLANES, W_C))
    y_ref[...] = (h * jax.nn.gelu(gate)).astype(y_ref.dtype)

    @pl.when(t == pl.num_programs(1) - 1)
    def _():
        hl_ref[...] = h_scr[...]


def _rglru(c, cw, cb, wa, ba, wx, bx, lam, conv0, h0, nb, nt, tb):
    n = c.shape[0]
    vec = pl.BlockSpec((1, W_C), lambda b, t: (0, 0))
    mat = pl.BlockSpec((W_C, W_C), lambda b, t: (0, 0))
    st = pl.BlockSpec((None, SUBLANES, W_C), lambda b, t: (b, 0, 0))
    return pl.pallas_call(
        _rglru_kernel,
        grid=(nb, nt),
        in_specs=[pl.BlockSpec((tb, 2 * W_C), lambda b, t: (b * nt + t, 0)),
                  pl.BlockSpec((CONV_W, W_C), lambda b, t: (0, 0)), vec, mat, vec, mat, vec, vec, st, st],
        out_specs=[pl.BlockSpec((tb, W_C), lambda b, t: (b * nt + t, 0)), st],
        out_shape=[jax.ShapeDtypeStruct((n, W_C), BF16), jax.ShapeDtypeStruct((nb, SUBLANES, W_C), F32)],
        scratch_shapes=[pltpu.VMEM((SUBLANES, W_C), F32), pltpu.VMEM((SUBLANES, W_C), F32)],
        compiler_params=_cparams("parallel", "arbitrary"),
        name="rglru",
    )(c, cw, cb, wa, ba, wx, bx, lam, conv0, h0)


def _sort_key(score):
    score = jnp.where(score == 0.0, 0.0, score)
    bits = lax.bitcast_convert_type(score, I32)
    return bits ^ ((bits >> 31) & 0x7FFFFFFF)


def _kth_largest_key(count_ge, k, rows):
    kf = float(k)
    zero = jnp.zeros((rows, 1), I32)
    t0 = jnp.where(count_ge(zero) >= kf, zero, jnp.full((rows, 1), INT_MIN, I32))

    def body(bi, t):
        cand = t + jnp.left_shift(jnp.int32(1), 30 - bi)
        return jnp.where(count_ge(cand) >= kf, cand, t)

    return lax.fori_loop(0, 31, body, t0)


def _first_positions(count_eq_before, need, nbits, rows):
    def body(bi, x):
        cand = x + jnp.left_shift(jnp.int32(1), nbits - 1 - bi)
        return jnp.where(count_eq_before(cand) < need, cand, x)

    return lax.fori_loop(0, nbits, body, jnp.zeros((rows, 1), I32))


def _qi_heads(qi):
    lane = lax.broadcasted_iota(I32, (qi.shape[0], LANES), 1)
    low = lane < D_IDX
    out = []
    for h in range(H_IDX):
        pair = qi[:, (h // 2) * LANES:(h // 2 + 1) * LANES]
        if h % 2:
            pair = pltpu.roll(pair, D_IDX, 1)
        out.append(jnp.where(low, pair, 0.0).astype(BF16))
    return out


def _dsa_prompt_kernel(q_ref, s_ref, ki_ref, kb_ref, vb_ref, o_ref, key_scr, bias_scr, x_scr,
                       *, q_off, kt_size, topk):
    i = pl.program_id(1)
    QB = q_ref.shape[0]
    KT = kt_size
    nkt = key_scr.shape[0]
    lk = nkt * KT
    q0 = q_off + i * QB
    qih = _qi_heads(q_ref[:, W_B:2 * W_B])
    wi = s_ref[:, S_WI:S_WI + H_IDX]
    wcols = [wi[:, h:h + 1] for h in range(H_IDX)]
    rowpos = q0 + lax.broadcasted_iota(I32, (QB, KT), 0)
    colpos = lax.broadcasted_iota(I32, (QB, KT), 1)

    def score_tile(kt, carry):
        k0 = pl.multiple_of(kt * KT, KT)
        ki = ki_ref[pl.ds(k0, KT), :].astype(BF16)
        sc = jnp.zeros((QB, KT), F32)
        for h in range(H_IDX):
            sc = sc + wcols[h] * jnp.maximum(_dot_nt(qih[h], ki), 0.0)
        sc = jnp.where(colpos + k0 <= rowpos, sc, -jnp.inf)
        key_scr[kt] = _sort_key(sc)
        return carry

    lax.fori_loop(0, nkt, score_tile, 0)

    lanepos = lax.broadcasted_iota(I32, (QB, LANES), 1)

    def count(pred):
        def body(kt, acc):
            tile = key_scr[kt]
            k0 = kt * KT
            for j in range(KT // LANES):
                sl = slice(j * LANES, (j + 1) * LANES)
                acc = acc + jnp.where(pred(tile[:, sl], lanepos + (k0 + j * LANES)), 1.0, 0.0)
            return acc
        acc = lax.fori_loop(0, nkt, body, jnp.zeros((QB, LANES), F32))
        return jnp.sum(acc, axis=1, keepdims=True)

    thr = _kth_largest_key(lambda t: count(lambda key, pos: key >= t), topk, QB)
    cnt_ge = count(lambda key, pos: key >= thr)
    cnt_gt = count(lambda key, pos: key > thr)
    x_scr[...] = jnp.full(x_scr.shape, lk, I32)

    @pl.when(jnp.max(cnt_ge) > float(topk))
    def _():
        need = float(topk) - cnt_gt
        nbits = max(1, (lk - 1).bit_length())
        x = _first_positions(
            lambda c: count(lambda key, pos: jnp.logical_and(key == thr, pos < c)), need, nbits, QB)
        x_scr[...] = jnp.broadcast_to(x, x_scr.shape)

    xlim = x_scr[:, 0:1]

    def bias_tile(kt, carry):
        key = key_scr[kt]
        pos = colpos + kt * KT
        sel = jnp.logical_or(key > thr, jnp.logical_and(key == thr, pos <= xlim))
        bias_scr[kt] = jnp.where(jnp.logical_and(sel, pos <= rowpos), 0.0, NEG)
        return carry

    lax.fori_loop(0, nkt, bias_tile, 0)

    lane = lax.broadcasted_iota(I32, (QB, LANES), 1)
    low = lane < DH_B
    for j in range(H_B // 2):
        qpair = q_ref[:, j * LANES:(j + 1) * LANES]
        halves = []
        for half in range(2):
            qm = jnp.where(low if half == 0 else jnp.logical_not(low), qpair, 0.0).astype(BF16)

            def tile(kt, carry, qm=qm, j=j):
                m, l, acc = carry
                k0 = pl.multiple_of(kt * KT, KT)
                kk = kb_ref[pl.ds(k0, KT), j * LANES:(j + 1) * LANES]
                vv = vb_ref[pl.ds(k0, KT), j * LANES:(j + 1) * LANES]
                lg = _dot_nt(qm, kk) * DH_B ** -0.5 + bias_scr[kt]
                m_new = jnp.maximum(m, jnp.max(lg, axis=1, keepdims=True))
                p = jnp.exp(lg - m_new)
                alpha = jnp.exp(m - m_new)
                l = alpha * l + jnp.sum(p, axis=1, keepdims=True)
                acc = alpha * acc + _dot(p.astype(BF16), vv)
                return m_new, l, acc

            m, l, acc = lax.fori_loop(
                0, nkt, tile,
                (jnp.full((QB, 1), NEG, F32), jnp.zeros((QB, 1), F32), jnp.zeros((QB, LANES), F32)))
            halves.append(acc / l)
        o_ref[:, j * LANES:(j + 1) * LANES] = jnp.where(low, halves[0], halves[1]).astype(o_ref.dtype)


def _dsa_prompt(bq, s, kb, vb, nb, t):
    n = bq.shape[0]
    QB = math.gcd(t, CHUNK)
    topk = min(TOPK_MAX, t // 4)
    nseg = 1
    for cand in (8, 4, 2):
        if t % cand == 0 and (t // cand) % 512 == 0:
            nseg = cand
            break
    seg = t // nseg
    KT = 512 if seg % 512 == 0 else seg
    s3 = s.reshape(nb, t, LANES)
    kb3 = kb.reshape(nb, t, W_B)
    vb3 = vb.reshape(nb, t, W_B)
    nqb = seg // QB
    outs = []
    for g in range(nseg):
        lk = (g + 1) * seg
        nkt = lk // KT
        row_blk = functools.partial(lambda b, i, g: (b * (t // QB) + g * nqb + i, 0), g=g)
        outs.append(pl.pallas_call(
            functools.partial(_dsa_prompt_kernel, q_off=g * seg, kt_size=KT, topk=topk),
            grid=(nb, nqb),
            in_specs=[pl.BlockSpec((QB, 2 * W_B), row_blk),
                      pl.BlockSpec((QB, LANES), row_blk),
                      pl.BlockSpec((None, lk, LANES), lambda b, i: (b, 0, 0)),
                      pl.BlockSpec((None, lk, W_B), lambda b, i: (b, 0, 0)),
                      pl.BlockSpec((None, lk, W_B), lambda b, i: (b, 0, 0))],
            out_specs=pl.BlockSpec((None, QB, W_B), lambda b, i: (b, i, 0)),
            out_shape=jax.ShapeDtypeStruct((nb, seg, W_B), BF16),
            scratch_shapes=[pltpu.VMEM((nkt, QB, KT), I32), pltpu.VMEM((nkt, QB, KT), F32),
                            pltpu.VMEM((QB, LANES), I32)],
            compiler_params=_cparams("parallel", "arbitrary"),
            name=f"dsa_prompt_{g}",
        )(bq, s, s3, kb3, vb3))
    return jnp.concatenate(outs, axis=1).reshape(n, W_B)


def _dsa_sample_score_kernel(pt_ref, q_ref, s_ref, snew_ref, page_ref, o_ref):
    p = pl.program_id(1)
    npg = pl.num_programs(1) - 1
    T = q_ref.shape[0]
    qih = _qi_heads(q_ref[:, W_B:2 * W_B])
    wi = s_ref[:, S_WI:S_WI + H_IDX]

    def scores(ki):
        sc = jnp.zeros((T, ki.shape[0]), F32)
        for h in range(H_IDX):
            sc = sc + wi[:, h:h + 1] * jnp.maximum(_dot_nt(qih[h], ki), 0.0)
        return sc

    @pl.when(p < npg)
    def _():
        pg = page_ref[...]
        ki = jnp.concatenate([pg, jnp.zeros_like(pg)], axis=1).astype(BF16)
        o_ref[...] = scores(ki)

    @pl.when(p == npg)
    def _():
        sc = scores(snew_ref[...].astype(BF16))
        r = lax.broadcasted_iota(I32, sc.shape, 0)
        c = lax.broadcasted_iota(I32, sc.shape, 1)
        o_ref[...] = jnp.where(c <= r, sc, -jnp.inf)


def _dsa_sample_attn_kernel(pt_ref, q_ref, sc_ref, knew_ref, vnew_ref, kpage_ref, vpage_ref, o_ref,
                            thr_scr, x_scr, m_scr, l_scr, acc_scr, *, topk):
    p = pl.program_id(1)
    npg = pl.num_programs(1) - 1
    T = q_ref.shape[0]
    R = H_B * T
    ltot = sc_ref.shape[0] * PAGE_SIZE

    @pl.when(p == 0)
    def _():
        key = _sort_key(sc_ref[...])
        pos = (lax.broadcasted_iota(I32, key.shape, 0) * PAGE_SIZE
               + lax.broadcasted_iota(I32, key.shape, 2))

        def count(pred):
            per_lane = jnp.sum(jnp.where(pred(key, pos), 1.0, 0.0), axis=0)
            return jnp.sum(per_lane, axis=1, keepdims=True)

        thr = _kth_largest_key(lambda t: count(lambda k_, p_: k_ >= t), topk, T)
        cnt_gt = count(lambda k_, p_: k_ > thr)
        nbits = max(1, (ltot - 1).bit_length())
        x = _first_positions(lambda c: count(lambda k_, p_: jnp.logical_and(k_ == thr, p_ < c)),
                             float(topk) - cnt_gt, nbits, T)
        thr_scr[...] = jnp.broadcast_to(thr, thr_scr.shape)
        x_scr[...] = jnp.broadcast_to(x, x_scr.shape)
        m_scr[...] = jnp.full(m_scr.shape, NEG, F32)
        l_scr[...] = jnp.zeros_like(l_scr)
        acc_scr[...] = jnp.zeros_like(acc_scr)

    q = q_ref[:, 0:W_B]
    qrep = jnp.concatenate([q] * H_B, axis=0)
    rr = lax.broadcasted_iota(I32, (R, W_B), 0) // T
    cc = lax.broadcasted_iota(I32, (R, W_B), 1) // DH_B
    diag = rr == cc
    qbd = jnp.where(diag, qrep, 0.0).astype(BF16)

    def step(kk, vv, valid):
        lg = _dot_nt(qbd, kk) * DH_B ** -0.5
        bias = jnp.where(valid, 0.0, NEG)
        lg = lg + jnp.concatenate([bias] * H_B, axis=0)
        m = m_scr[:, 0:1]
        m_new = jnp.maximum(m, jnp.max(lg, axis=1, keepdims=True))
        pr = jnp.exp(lg - m_new)
        alpha = jnp.exp(m - m_new)
        l_scr[...] = jnp.broadcast_to(alpha * l_scr[:, 0:1] + jnp.sum(pr, axis=1, keepdims=True), l_scr.shape)
        acc_scr[...] = alpha * acc_scr[...] + _dot(pr.astype(BF16), vv)
        m_scr[...] = jnp.broadcast_to(m_new, m_scr.shape)

    key = _sort_key(sc_ref[p])
    pos = p * PAGE_SIZE + lax.broadcasted_iota(I32, key.shape, 1)
    thr = thr_scr[:, 0:1]
    sel = jnp.logical_or(key > thr, jnp.logical_and(key == thr, pos <= x_scr[:, 0:1]))

    @pl.when(p < npg)
    def _():
        step(kpage_ref[...].astype(BF16), vpage_ref[...].astype(BF16), sel)

    @pl.when(p == npg)
    def _():
        r = lax.broadcasted_iota(I32, sel.shape, 0)
        c = lax.broadcasted_iota(I32, sel.shape, 1)
        step(knew_ref[...], vnew_ref[...], jnp.logical_and(sel, c <= r))
        out = jnp.where(diag, acc_scr[...] / l_scr[:, 0:1], 0.0)
        res = out[0:T]
        for h in range(1, H_B):
            res = res + out[h * T:(h + 1) * T]
        o_ref[...] = res.astype(o_ref.dtype)


def _dsa_sample(bq, s, kb, vb, pool_k, pool_v, pool_ki, page_table, nb, t):
    npg = page_table.shape[1]
    past = npg * PAGE_SIZE
    topk = min(TOPK_MAX, (past + t) // 4)
    pad = PAGE_SIZE - t
    s_new = jnp.pad(s.reshape(nb, t, LANES), ((0, 0), (0, pad), (0, 0)))
    k_new = jnp.pad(kb.reshape(nb, t, W_B), ((0, 0), (0, pad), (0, 0)))
    v_new = jnp.pad(vb.reshape(nb, t, W_B), ((0, 0), (0, pad), (0, 0)))
    lastp = npg - 1

    scores = pl.pallas_call(
        _dsa_sample_score_kernel,
        grid_spec=pltpu.PrefetchScalarGridSpec(
            num_scalar_prefetch=1,
            grid=(nb, npg + 1),
            in_specs=[pl.BlockSpec((t, 2 * W_B), lambda b, p, pt: (b, 0)),
                      pl.BlockSpec((t, LANES), lambda b, p, pt: (b, 0)),
                      pl.BlockSpec((None, PAGE_SIZE, LANES), lambda b, p, pt: (b, 0, 0)),
                      pl.BlockSpec((None, PAGE_SIZE, D_IDX),
                                   lambda b, p, pt: (pt[b, jnp.minimum(p, lastp)], 0, 0))],
            out_specs=pl.BlockSpec((None, None, t, PAGE_SIZE), lambda b, p, pt: (b, p, 0, 0))),
        out_shape=jax.ShapeDtypeStruct((nb, npg + 1, t, PAGE_SIZE), F32),
        compiler_params=_cparams("parallel", "arbitrary"),
        name="dsa_sample_scores",
    )(page_table, bq, s, s_new, pool_ki)

    return pl.pallas_call(
        functools.partial(_dsa_sample_attn_kernel, topk=topk),
        grid_spec=pltpu.PrefetchScalarGridSpec(
            num_scalar_prefetch=1,
            grid=(nb, npg + 1),
            in_specs=[pl.BlockSpec((t, 2 * W_B), lambda b, p, pt: (b, 0)),
                      pl.BlockSpec((None, npg + 1, t, PAGE_SIZE), lambda b, p, pt: (b, 0, 0, 0)),
                      pl.BlockSpec((None, PAGE_SIZE, W_B), lambda b, p, pt: (b, 0, 0)),
                      pl.BlockSpec((None, PAGE_SIZE, W_B), lambda b, p, pt: (b, 0, 0)),
                      pl.BlockSpec((None, PAGE_SIZE, W_B),
                                   lambda b, p, pt: (pt[b, jnp.minimum(p, lastp)], 0, 0)),
                      pl.BlockSpec((None, PAGE_SIZE, W_B),
                                   lambda b, p, pt: (pt[b, jnp.minimum(p, lastp)], 0, 0))],
            out_specs=pl.BlockSpec((t, W_B), lambda b, p, pt: (b, 0)),
            scratch_shapes=[pltpu.VMEM((t, LANES), I32), pltpu.VMEM((t, LANES), I32),
                            pltpu.VMEM((H_B * t, LANES), F32), pltpu.VMEM((H_B * t, LANES), F32),
                            pltpu.VMEM((H_B * t, W_B), F32)]),
        out_shape=jax.ShapeDtypeStruct((nb * t, W_B), BF16),
        compiler_params=_cparams("parallel", "arbitrary"),
        name="dsa_sample_attn",
    )(page_table, bq, scores, k_new, v_new, pool_k, pool_v)


def _pad_chunks(a, nb, t, fill=None):
    c = a.shape[1]
    a3 = a.reshape(nb, t, c)
    if fill is None:
        a3 = jnp.pad(a3, ((0, 0), (0, CHUNK - t), (0, 0)))
    else:
        a3 = jnp.concatenate([a3, jnp.broadcast_to(fill, (nb, CHUNK - t, c))], axis=1)
    return a3.reshape(nb * CHUNK, c)


def _retention_tables(t_true, pos0, t_pad):
    L = CHUNK
    lt = min(L, t_true)
    lg = jnp.log1p(-jnp.exp2(-5.0 - jnp.arange(H_D, dtype=F32)))
    j = jnp.arange(L, dtype=F32)
    causal = jnp.tril(jnp.ones((L, L), dtype=bool))
    dmat = jnp.exp(jnp.where(causal, (j[:, None] - j[None, :]) * lg[:, None, None], -jnp.inf))
    qdec = jnp.exp((j + 1.0) * lg[:, None])
    kdec = jnp.where(j < lt, jnp.exp((lt - 1.0 - j) * lg[:, None]), 0.0)
    sdec = jnp.broadcast_to(jnp.exp(lt * lg)[:, None], (H_D, L))
    dec = jnp.zeros((H_D, L, LANES), F32)
    dec = dec.at[:, :, 0].set(qdec).at[:, :, 1].set(kdec).at[:, :, 2].set(sdec)
    half = DK_D // 2
    freq = ROPE_BASE ** (-jnp.arange(half, dtype=F32) / half)
    pos = (pos0 + jnp.arange(t_pad)).astype(F32)
    ang = pos[:, None] * freq[None, :]
    cos, sin = jnp.cos(ang), jnp.sin(ang)
    return dmat, dec, jnp.concatenate([cos, cos], -1), jnp.concatenate([-sin, sin], -1)


def _forward(x3, st, prm):
    nb, t, d = x3.shape
    n = nb * t
    x = x3.reshape(n, d)
    short = t < CHUNK
    t_pad = CHUNK if short else t
    nc = t_pad // CHUNK
    new = {}

    a, bq, k, v, s, kb, vb = _inproj_e(x, prm["w_in_e"], prm["bias_e"])
    if st is None:
        c0 = jnp.zeros((nb, H_A, DK_A, DV_A), F32)
        n0 = jnp.zeros((nb, H_A, DK_A), F32)
        m0 = jnp.zeros((nb, H_A), F32)
    else:
        c0, n0, m0 = st["mlstm_C"][0], st["mlstm_n"][0], st["mlstm_m"][0]
    m0b = jnp.broadcast_to(m0[:, :, None], (nb, H_A, LANES))
    if short:
        lane = jnp.arange(LANES)
        fill = jnp.where((lane >= S_IG) & (lane < S_IG + H_A), NEG,
                         jnp.where((lane >= S_FG) & (lane < S_FG + H_A), 1e4, 0.0)).astype(F32)
        a_m, s_m = _pad_chunks(a, nb, t), _pad_chunks(s, nb, t, fill)
    else:
        a_m, s_m = a, s
    ya, c1, n1, m1 = _mlstm(a_m, s_m, prm["a_norm_g"], c0, n0, m0b, nb, nc)
    if short:
        ya = ya.reshape(nb, CHUNK, W_A)[:, :t].reshape(n, W_A)
    if st is None:
        yb = _dsa_prompt(bq, s, kb, vb, nb, t)
    else:
        yb = _dsa_sample(bq, s, kb, vb, st["pool_k"], st["pool_v"], st["pool_ki"], st["page_table"], nb, t)
    new["mlstm_C"], new["mlstm_n"], new["mlstm_m"] = c1[None], n1[None], m1[None, :, :, 0]
    new["k"] = k.reshape(1, nb, t, H_B, DH_B)
    new["v"] = v.reshape(1, nb, t, H_B, DH_B)
    new["kidx"] = s[:, S_KI:S_KI + D_IDX].reshape(1, nb, t, D_IDX)
    x = _outproj_ln(ya, yb, prm["w_out_e"], x, prm["ln_g"][0, 0], prm["ln_b"][0, 0])
    x = _moe_ln(x, prm["router_w"], prm["router_b"], prm["moe_wg"][0], prm["moe_wu"][0], prm["moe_wd"][0],
                prm["ln_g"][0, 1], prm["ln_b"][0, 1])

    c, dd = _inproj_o(x, prm["w_in_o"])
    if st is None:
        conv0 = jnp.zeros((nb, SUBLANES, W_C), F32)
        h0 = jnp.zeros((nb, SUBLANES, W_C), F32)
        s0 = jnp.zeros((nb, H_D, DK_D, DV_D), F32)
        pos0 = 0
    else:
        conv0 = jnp.pad(st["conv"][0], ((0, 0), (SUBLANES - (CONV_W - 1), 0), (0, 0)))
        h0 = jnp.broadcast_to(st["lru_h"][0][:, None, :], (nb, SUBLANES, W_C))
        s0 = st["ret_S"][0]
        pos0 = st["page_table"].shape[1] * PAGE_SIZE
    tb = _row_block(t, 256)
    yc, hl = _rglru(c, prm["c_conv_w"], prm["c_conv_b"], prm["c_wa"], prm["c_ba"], prm["c_wx"], prm["c_bx"],
                    prm["c_lambda"], conv0, h0, nb, t // tb, tb)
    dmat, dec, cos2, sin2 = _retention_tables(t, pos0, t_pad)
    d_m = _pad_chunks(dd, nb, t) if short else dd
    yd, s1 = _retention(d_m, cos2, sin2, dmat, dec, prm["d_norm_g"], s0, nb, nc)
    if short:
        yd = yd.reshape(nb, CHUNK, W_D)[:, :t].reshape(n, W_D)
    new["conv"] = c[:, 0:W_C].reshape(nb, t, W_C)[None, :, t - (CONV_W - 1):]
    new["lru_h"] = hl[None, :, 0]
    new["ret_S"] = s1[None]
    x = _outproj_ln(yc, yd, prm["w_out_o"], x, prm["ln_g"][1, 0], prm["ln_b"][1, 0])
    x = _moe_ln(x, prm["router_w"], prm["router_b"], prm["moe_wg"][1], prm["moe_wu"][1], prm["moe_wd"][1],
                prm["ln_g"][1, 1], prm["ln_b"][1, 1])
    return x.reshape(nb, t, d), new


def _block_diag(w):
    nblk, blk, _ = w.shape
    eye = jnp.eye(nblk, dtype=w.dtype)
    return (eye[:, None, :, None] * w[:, :, None, :]).reshape(nblk * blk, nblk * blk)


def _prepare_params(w_in_e, b_if_e, a_norm_g, w_out_e, w_in_o, c_conv_w, c_conv_b, c_wa, c_ba, c_wx, c_bx,
                    c_lambda, d_norm_g, w_out_o, router_w, router_bias, moe_w_gate, moe_w_up, moe_w_down,
                    ln_g, ln_b):
    d = w_in_e.shape[1]
    o = [0]
    for width in (W_A, W_A, W_A, W_A, 2 * H_A, W_B, W_B, W_B, H_IDX * D_IDX, D_IDX, H_IDX):
        o.append(o[-1] + width)
    we = w_in_e[0]
    col = lambda i: we[:, o[i]:o[i + 1]]
    slab = jnp.zeros((d, LANES), F32)
    slab = slab.at[:, S_KI:S_KI + D_IDX].set(col(9)).at[:, S_IG:S_IG + 2 * H_A].set(col(4))
    slab = slab.at[:, S_WI:S_WI + H_IDX].set(col(10))
    w_e = jnp.concatenate([col(0), col(1), col(2), col(3), col(5), col(8), col(6), col(7), slab], axis=1)
    bias_e = jnp.zeros((1, LANES), F32).at[0, S_IG:S_IG + 2 * H_A].set(b_if_e[0])
    rw = jnp.zeros((d, LANES), F32).at[:, :N_EXPERTS].set(router_w)
    rb = jnp.zeros((1, LANES), F32).at[0, :N_EXPERTS].set(router_bias)
    row = lambda a: a.reshape(1, -1).astype(F32)
    return {
        "w_in_e": w_e.astype(BF16), "bias_e": bias_e, "a_norm_g": row(a_norm_g[0]),
        "w_out_e": w_out_e[0].astype(BF16), "w_in_o": w_in_o[0].astype(BF16),
        "c_conv_w": c_conv_w[0].astype(F32), "c_conv_b": row(c_conv_b[0]),
        "c_wa": _block_diag(c_wa[0]).astype(BF16), "c_ba": row(c_ba[0]),
        "c_wx": _block_diag(c_wx[0]).astype(BF16), "c_bx": row(c_bx[0]),
        "c_lambda": row(c_lambda[0]), "d_norm_g": row(d_norm_g[0]), "w_out_o": w_out_o[0].astype(BF16),
        "router_w": rw.astype(BF16), "router_b": rb,
        "moe_wg": moe_w_gate.astype(BF16), "moe_wu": moe_w_up.astype(BF16), "moe_wd": moe_w_down.astype(BF16),
        "ln_g": ln_g.reshape(DEPTH, 2, 1, -1).astype(F32), "ln_b": ln_b.reshape(DEPTH, 2, 1, -1).astype(F32),
    }


def kernel(x_prompt, x_sample, state_mlstm_C, state_mlstm_n, state_mlstm_m, cache_k, cache_v, cache_kidx,
           page_table, state_conv, state_lru_h, state_ret_S, w_in_e, b_if_e, a_norm_g, w_out_e, w_in_o,
           c_conv_w, c_conv_b, c_wa, c_ba, c_wx, c_bx, c_lambda, d_norm_g, w_out_o, router_w, router_bias,
           moe_w_gate, moe_w_up, moe_w_down, ln_g, ln_b):
    prm = _prepare_params(w_in_e, b_if_e, a_norm_g, w_out_e, w_in_o, c_conv_w, c_conv_b, c_wa, c_ba, c_wx,
                          c_bx, c_lambda, d_norm_g, w_out_o, router_w, router_bias, moe_w_gate, moe_w_up,
                          moe_w_down, ln_g, ln_b)
    n_pool = cache_k.shape[1]
    st = {"mlstm_C": state_mlstm_C, "mlstm_n": state_mlstm_n, "mlstm_m": state_mlstm_m,
          "pool_k": cache_k[0].reshape(n_pool, PAGE_SIZE, W_B), "pool_v": cache_v[0].reshape(n_pool, PAGE_SIZE, W_B),
          "pool_ki": cache_kidx[0], "page_table": page_table,
          "conv": state_conv, "lru_h": state_lru_h, "ret_S": state_ret_S}
    y_p, nsp = _forward(x_prompt, None, prm)
    y_s, nss = _forward(x_sample, st, prm)
    names = ("mlstm_C", "mlstm_n", "mlstm_m", "k", "v", "kidx", "conv", "lru_h", "ret_S")
    return (y_p, y_s) + tuple(nsp[k] for k in names) + tuple(nss[k] for k in names)
```

```python
import functools
import math

import jax
import jax.numpy as jnp
from jax import lax
from jax.experimental import pallas as pl
from jax.experimental.pallas import tpu as pltpu

F32 = jnp.float32
BF16 = jnp.bfloat16
I32 = jnp.int32

DEPTH = 2
PAGE_SIZE = 128
H_A, DK_A, DV_A = 4, 128, 128
W_A = H_A * DV_A
H_B, DH_B = 8, 64
W_B = H_B * DH_B
H_IDX, D_IDX = 8, 64
TOPK_MAX = 256
W_C, N_BLK_C, CONV_W, LRU_C = 512, 8, 4, 8.0
BLK_C = W_C // N_BLK_C
H_D, DK_D, DV_D = 4, 128, 128
W_D = H_D * DV_D
ROPE_BASE = 10000.0
N_EXPERTS, N_GROUPS, TOP_K_EXP, D_FF_EXP = 16, 4, 2, 512
EXP_PER_GROUP = N_EXPERTS // N_GROUPS
ALPHA = (2 * DEPTH) ** 0.25
LN_EPS = 1e-5
HN_EPS = 1e-6

CHUNK = 128
LANES = 128
SUBLANES = 8
NEG = -1e30
INT_MIN = -2 ** 31
VMEM_LIMIT = 56 * 1024 * 1024

S_KI = 0
S_IG = 64
S_FG = 68
S_WI = 72


def _cparams(*sem):
    return pltpu.CompilerParams(dimension_semantics=sem, vmem_limit_bytes=VMEM_LIMIT)


def _dot(a, b):
    return jnp.dot(a, b, preferred_element_type=F32)


def _dot_nt(a, b):
    return lax.dot_general(a, b, (((1,), (1,)), ((), ())), preferred_element_type=F32)


def _dot_tn(a, b):
    return lax.dot_general(a, b, (((0,), (0,)), ((), ())), preferred_element_type=F32)


def _row_block(n, target):
    t = min(n, target)
    while n % t:
        t //= 2
    return t


def _layer_norm(z, g, b):
    mu = jnp.mean(z, -1, keepdims=True)
    zc = z - mu
    var = jnp.mean(zc * zc, -1, keepdims=True)
    return zc * lax.rsqrt(var + LN_EPS) * g + b


def _head_norm(h):
    mu = jnp.mean(h, -1, keepdims=True)
    hc = h - mu
    var = jnp.mean(hc * hc, -1, keepdims=True)
    return hc * lax.rsqrt(var + HN_EPS)


def _inproj_e_kernel(x_ref, w_ref, bias_ref, a_ref, bq_ref, k_ref, v_ref, s_ref, kb_ref, vb_ref):
    x = x_ref[...].astype(BF16)

    def mm(lo, hi):
        return _dot(x, w_ref[:, lo:hi])

    a_ref[:, 0:W_A] = mm(0, W_A)
    a_ref[:, W_A:2 * W_A] = mm(W_A, 2 * W_A) * DK_A ** -0.5
    a_ref[:, 2 * W_A:4 * W_A] = mm(2 * W_A, 4 * W_A)
    o = 4 * W_A
    bq_ref[...] = mm(o, o + 2 * W_B)
    o += 2 * W_B
    k = mm(o, o + W_B)
    k_ref[...] = k
    kb_ref[...] = k.astype(BF16)
    o += W_B
    v = mm(o, o + W_B)
    v_ref[...] = v
    vb_ref[...] = v.astype(BF16)
    o += W_B
    s_ref[...] = mm(o, o + LANES) + bias_ref[...]


def _inproj_e(x, w, bias):
    n, d = x.shape
    tm = _row_block(n, 256)
    wcols = w.shape[1]
    outs = [(4 * W_A, F32), (2 * W_B, F32), (W_B, F32), (W_B, F32), (LANES, F32), (W_B, BF16), (W_B, BF16)]
    return pl.pallas_call(
        _inproj_e_kernel,
        grid=(n // tm,),
        in_specs=[pl.BlockSpec((tm, d), lambda i: (i, 0)),
                  pl.BlockSpec((d, wcols), lambda i: (0, 0)),
                  pl.BlockSpec((1, LANES), lambda i: (0, 0))],
        out_specs=[pl.BlockSpec((tm, c), lambda i: (i, 0)) for c, _ in outs],
        out_shape=[jax.ShapeDtypeStruct((n, c), dt) for c, dt in outs],
        compiler_params=_cparams("parallel"),
        name="inproj_even",
    )(x, w, bias)


def _inproj_o_kernel(x_ref, w_ref, c_ref, d_ref):
    x = x_ref[...].astype(BF16)
    c_ref[...] = _dot(x, w_ref[:, 0:2 * W_C])
    d_ref[...] = _dot(x, w_ref[:, 2 * W_C:2 * W_C + 4 * W_D])


def _inproj_o(x, w):
    n, d = x.shape
    tm = _row_block(n, 256)
    return pl.pallas_call(
        _inproj_o_kernel,
        grid=(n // tm,),
        in_specs=[pl.BlockSpec((tm, d), lambda i: (i, 0)),
                  pl.BlockSpec(w.shape, lambda i: (0, 0))],
        out_specs=[pl.BlockSpec((tm, 2 * W_C), lambda i: (i, 0)),
                   pl.BlockSpec((tm, 4 * W_D), lambda i: (i, 0))],
        out_shape=[jax.ShapeDtypeStruct((n, 2 * W_C), F32), jax.ShapeDtypeStruct((n, 4 * W_D), F32)],
        compiler_params=_cparams("parallel"),
        name="inproj_odd",
    )(x, w)


def _outproj_ln_kernel(y1_ref, y2_ref, w_ref, x_ref, g_ref, b_ref, o_ref):
    half = y1_ref.shape[1]
    y = _dot(y1_ref[...], w_ref[0:half, :]) + _dot(y2_ref[...], w_ref[half:2 * half, :])
    o_ref[...] = _layer_norm(ALPHA * x_ref[...] + y, g_ref[...], b_ref[...])


def _outproj_ln(y1, y2, w, x, g, b):
    n, d = x.shape
    tm = _row_block(n, 512)
    half = y1.shape[1]
    return pl.pallas_call(
        _outproj_ln_kernel,
        grid=(n // tm,),
        in_specs=[pl.BlockSpec((tm, half), lambda i: (i, 0)),
                  pl.BlockSpec((tm, half), lambda i: (i, 0)),
                  pl.BlockSpec(w.shape, lambda i: (0, 0)),
                  pl.BlockSpec((tm, d), lambda i: (i, 0)),
                  pl.BlockSpec((1, d), lambda i: (0, 0)),
                  pl.BlockSpec((1, d), lambda i: (0, 0))],
        out_specs=pl.BlockSpec((tm, d), lambda i: (i, 0)),
        out_shape=jax.ShapeDtypeStruct((n, d), F32),
        compiler_params=_cparams("parallel"),
        name="outproj_ln",
    )(y1, y2, w, x, g, b)


def _route(logits, bias):
    lane = lax.broadcasted_iota(I32, logits.shape, 1)
    valid = lane < N_EXPERTS
    pos = lane % EXP_PER_GROUP
    grp = (lane // EXP_PER_GROUP).astype(F32)
    s = jax.nn.sigmoid(logits)
    sel = jnp.where(valid, s + bias, NEG)
    rank = jnp.zeros(logits.shape, F32)
    for d in range(1, EXP_PER_GROUP):
        lo = pltpu.roll(sel, d, 1)
        hi = pltpu.roll(sel, LANES - d, 1)
        rank = rank + jnp.where(jnp.logical_and(pos >= d, lo >= sel), 1.0, 0.0)
        rank = rank + jnp.where(jnp.logical_and(pos + d < EXP_PER_GROUP, hi > sel), 1.0, 0.0)
    top2 = jnp.logical_and(rank < TOP_K_EXP, valid)
    contrib = jnp.where(top2, sel, 0.0)
    gs = contrib
    for d in range(1, EXP_PER_GROUP):
        lo = pltpu.roll(contrib, d, 1)
        hi = pltpu.roll(contrib, LANES - d, 1)
        gs = gs + jnp.where(pos >= d, lo, 0.0) + jnp.where(pos + d < EXP_PER_GROUP, hi, 0.0)
    gs = jnp.where(valid, gs, NEG)
    gmax = jnp.max(gs, axis=1, keepdims=True)
    best = jnp.min(jnp.where(gs == gmax, grp, 1e9), axis=1, keepdims=True)
    chosen = jnp.logical_and(top2, grp == best)
    s_sel = jnp.where(chosen, s, 0.0)
    return s_sel / jnp.sum(s_sel, axis=1, keepdims=True)


def _moe_kernel(x_ref, rw_ref, rb_ref, wg_ref, wu_ref, wd_ref, g_ref, b_ref, o_ref,
                xb_scr, comb_scr, acc_scr, *, sub):
    e = pl.program_id(1)
    tm = x_ref.shape[0]

    @pl.when(e == 0)
    def _():
        xb = x_ref[...].astype(BF16)
        xb_scr[...] = xb
        logits = jnp.dot(x_ref[...], rw_ref[...], precision=lax.Precision.HIGHEST, preferred_element_type=F32)
        comb_scr[...] = _route(logits, rb_ref[...])
        acc_scr[...] = jnp.zeros_like(acc_scr)

    def rows(r, carry):
        r0 = pl.multiple_of(r * sub, sub)
        xb = xb_scr[pl.ds(r0, sub), :]
        gate = _dot(xb, wg_ref[...])
        h = gate * jax.nn.sigmoid(gate) * _dot(xb, wu_ref[...])
        y = _dot(h.astype(BF16), wd_ref[...])
        comb = comb_scr[pl.ds(r0, sub), :]
        lane = lax.broadcasted_iota(I32, comb.shape, 1)
        c_e = jnp.sum(jnp.where(lane == e, comb, 0.0), axis=1, keepdims=True)
        acc_scr[pl.ds(r0, sub), :] += c_e * y
        return carry

    lax.fori_loop(0, tm // sub, rows, 0)

    @pl.when(e == N_EXPERTS - 1)
    def _():
        o_ref[...] = _layer_norm(ALPHA * x_ref[...] + acc_scr[...], g_ref[...], b_ref[...])


def _moe_ln(x, rw, rb, wg, wu, wd, g, b):
    n, d = x.shape
    tm = _row_block(n, 1024)
    sub = _row_block(tm, 256)
    f = wg.shape[2]
    return pl.pallas_call(
        functools.partial(_moe_kernel, sub=sub),
        grid=(n // tm, N_EXPERTS),
        in_specs=[pl.BlockSpec((tm, d), lambda i, e: (i, 0)),
                  pl.BlockSpec((d, LANES), lambda i, e: (0, 0)),
                  pl.BlockSpec((1, LANES), lambda i, e: (0, 0)),
                  pl.BlockSpec((None, d, f), lambda i, e: (e, 0, 0)),
                  pl.BlockSpec((None, d, f), lambda i, e: (e, 0, 0)),
                  pl.BlockSpec((None, f, d), lambda i, e: (e, 0, 0)),
                  pl.BlockSpec((1, d), lambda i, e: (0, 0)),
                  pl.BlockSpec((1, d), lambda i, e: (0, 0))],
        out_specs=pl.BlockSpec((tm, d), lambda i, e: (i, 0)),
        out_shape=jax.ShapeDtypeStruct((n, d), F32),
        scratch_shapes=[pltpu.VMEM((tm, d), BF16), pltpu.VMEM((tm, LANES), F32), pltpu.VMEM((tm, d), F32)],
        compiler_params=_cparams("parallel", "arbitrary"),
        name="moe_ln",
    )(x, rw, rb, wg, wu, wd, g, b)


def _mlstm_kernel(a_ref, s_ref, g_ref, c0_ref, n0_ref, m0_ref, y_ref, c1_ref, n1_ref, m1_ref,
                  c_scr, n_scr, m_scr):
    c = pl.program_id(1)
    L = a_ref.shape[0]

    @pl.when(c == 0)
    def _():
        c_scr[...] = c0_ref[...]
        n_scr[...] = n0_ref[...]
        m_scr[...] = m0_ref[...]

    S = s_ref[...]
    lane = lax.broadcasted_iota(I32, S.shape, 1)
    is_f = jnp.logical_and(lane >= S_FG, lane < S_FG + H_A)
    lf = jnp.where(is_f, jax.nn.log_sigmoid(S), 0.0)
    row = lax.broadcasted_iota(I32, (L, L), 0)
    col = lax.broadcasted_iota(I32, (L, L), 1)
    causal = row >= col
    Fs = jnp.dot(causal.astype(F32), lf, precision=lax.Precision.HIGHEST, preferred_element_type=F32)
    Fa = pltpu.roll(Fs, LANES - (S_FG - S_IG), 1)
    AT = jnp.transpose(S - Fa)

    for h in range(H_A):
        ig = S[:, S_IG + h:S_IG + h + 1]
        F = Fa[:, S_IG + h:S_IG + h + 1]
        a_row = AT[S_IG + h:S_IG + h + 1, :]
        m_prev = m_scr[h:h + 1, 0:1]
        cm = jnp.max(jnp.where(causal, a_row, NEG), axis=1, keepdims=True)
        m_t = F + jnp.maximum(m_prev, cm)
        dmat = jnp.exp(jnp.where(causal, (F - m_t) + a_row, NEG))
        inter = jnp.exp(F + m_prev - m_t)
        q = a_ref[:, h * DK_A:(h + 1) * DK_A].astype(BF16)
        kf = a_ref[:, W_A + h * DK_A:W_A + (h + 1) * DK_A]
        k = kf.astype(BF16)
        v = a_ref[:, 2 * W_A + h * DV_A:2 * W_A + (h + 1) * DV_A].astype(BF16)
        C = c_scr[h]
        n = n_scr[h:h + 1, :]
        s = _dot_nt(q, k) * dmat
        num = _dot(s.astype(BF16), v) + inter * _dot(q, C.astype(BF16))
        qn = jnp.sum(q.astype(F32) * n.astype(BF16).astype(F32), axis=1, keepdims=True)
        den = jnp.sum(s, axis=1, keepdims=True) + inter * qn
        hout = num / jnp.maximum(jnp.abs(den), jnp.exp(-m_t))
        o = a_ref[:, 3 * W_A + h * DV_A:3 * W_A + (h + 1) * DV_A]
        y = _head_norm(hout) * g_ref[:, h * DV_A:(h + 1) * DV_A] * jax.nn.sigmoid(o)
        y_ref[:, h * DV_A:(h + 1) * DV_A] = y.astype(y_ref.dtype)
        m_new = m_t[L - 1:L, :]
        F_last = F[L - 1:L, :]
        w_s = jnp.exp(F_last - F + ig - m_new)
        decay = jnp.exp(F_last + m_prev - m_new)
        kw = kf * w_s
        c_scr[h] = decay * C + _dot_tn(kw.astype(BF16), v)
        n_scr[h:h + 1, :] = decay * n + jnp.sum(kw, axis=0, keepdims=True)
        m_scr[h:h + 1, :] = jnp.broadcast_to(m_new, (1, LANES))

    @pl.when(c == pl.num_programs(1) - 1)
    def _():
        c1_ref[...] = c_scr[...]
        n1_ref[...] = n_scr[...]
        m1_ref[...] = m_scr[...]


def _mlstm(a, s, gnorm, c0, n0, m0b, nb, nc):
    n = a.shape[0]
    L = CHUNK
    return pl.pallas_call(
        _mlstm_kernel,
        grid=(nb, nc),
        in_specs=[pl.BlockSpec((L, 4 * W_A), lambda b, c: (b * nc + c, 0)),
                  pl.BlockSpec((L, LANES), lambda b, c: (b * nc + c, 0)),
                  pl.BlockSpec((1, W_A), lambda b, c: (0, 0)),
                  pl.BlockSpec((None, H_A, DK_A, DV_A), lambda b, c: (b, 0, 0, 0)),
                  pl.BlockSpec((None, H_A, DK_A), lambda b, c: (b, 0, 0)),
                  pl.BlockSpec((None, H_A, LANES), lambda b, c: (b, 0, 0))],
        out_specs=[pl.BlockSpec((L, W_A), lambda b, c: (b * nc + c, 0)),
                   pl.BlockSpec((None, H_A, DK_A, DV_A), lambda b, c: (b, 0, 0, 0)),
                   pl.BlockSpec((None, H_A, DK_A), lambda b, c: (b, 0, 0)),
                   pl.BlockSpec((None, H_A, LANES), lambda b, c: (b, 0, 0))],
        out_shape=[jax.ShapeDtypeStruct((n, W_A), BF16),
                   jax.ShapeDtypeStruct((nb, H_A, DK_A, DV_A), F32),
                   jax.ShapeDtypeStruct((nb, H_A, DK_A), F32),
                   jax.ShapeDtypeStruct((nb, H_A, LANES), F32)],
        scratch_shapes=[pltpu.VMEM((H_A, DK_A, DV_A), F32), pltpu.VMEM((H_A, DK_A), F32),
                        pltpu.VMEM((H_A, LANES), F32)],
        compiler_params=_cparams("parallel", "arbitrary"),
        name="mlstm",
    )(a, s, gnorm, c0, n0, m0b)


def _retention_kernel(d_ref, cos_ref, sin_ref, dm_ref, dec_ref, g_ref, s0_ref, y_ref, s1_ref, s_scr):
    c = pl.program_id(1)

    @pl.when(c == 0)
    def _():
        s_scr[...] = s0_ref[...]

    cos2 = cos_ref[...]
    sin2 = sin_ref[...]
    for h in range(H_D):
        qf = d_ref[:, h * DK_D:(h + 1) * DK_D]
        kf = d_ref[:, W_D + h * DK_D:W_D + (h + 1) * DK_D]
        q = qf * cos2 + pltpu.roll(qf, DK_D // 2, 1) * sin2
        k = (kf * cos2 + pltpu.roll(kf, DK_D // 2, 1) * sin2) * DK_D ** -0.5
        v = d_ref[:, 2 * W_D + h * DV_D:2 * W_D + (h + 1) * DV_D].astype(BF16)
        qdec = dec_ref[h, :, 0:1]
        kdec = dec_ref[h, :, 1:2]
        sdec = dec_ref[h, 0:1, 2:3]
        S = s_scr[h]
        qb = q.astype(BF16)
        att = _dot_nt(qb, k.astype(BF16)) * dm_ref[h]
        o = _dot(att.astype(BF16), v) + qdec * _dot(qb, S.astype(BF16))
        s_scr[h] = sdec * S + _dot_tn((k * kdec).astype(BF16), v)
        gt = d_ref[:, 3 * W_D + h * DV_D:3 * W_D + (h + 1) * DV_D]
        y = _head_norm(o) * g_ref[:, h * DV_D:(h + 1) * DV_D] * (gt * jax.nn.sigmoid(gt))
        y_ref[:, h * DV_D:(h + 1) * DV_D] = y.astype(y_ref.dtype)

    @pl.when(c == pl.num_programs(1) - 1)
    def _():
        s1_ref[...] = s_scr[...]


def _retention(d, cos2, sin2, dmat, dec, gnorm, s0, nb, nc):
    n = d.shape[0]
    L = CHUNK
    return pl.pallas_call(
        _retention_kernel,
        grid=(nb, nc),
        in_specs=[pl.BlockSpec((L, 4 * W_D), lambda b, c: (b * nc + c, 0)),
                  pl.BlockSpec((L, DK_D), lambda b, c: (c, 0)),
                  pl.BlockSpec((L, DK_D), lambda b, c: (c, 0)),
                  pl.BlockSpec((H_D, L, L), lambda b, c: (0, 0, 0)),
                  pl.BlockSpec((H_D, L, LANES), lambda b, c: (0, 0, 0)),
                  pl.BlockSpec((1, W_D), lambda b, c: (0, 0)),
                  pl.BlockSpec((None, H_D, DK_D, DV_D), lambda b, c: (b, 0, 0, 0))],
        out_specs=[pl.BlockSpec((L, W_D), lambda b, c: (b * nc + c, 0)),
                   pl.BlockSpec((None, H_D, DK_D, DV_D), lambda b, c: (b, 0, 0, 0))],
        out_shape=[jax.ShapeDtypeStruct((n, W_D), BF16),
                   jax.ShapeDtypeStruct((nb, H_D, DK_D, DV_D), F32)],
        scratch_shapes=[pltpu.VMEM((H_D, DK_D, DV_D), F32)],
        compiler_params=_cparams("parallel", "arbitrary"),
        name="retention",
    )(d, cos2, sin2, dmat, dec, gnorm, s0)


def _shift_rows(x, prev, j):
    tb = x.shape[0]
    xs = pltpu.roll(x, j, 0)
    pr = pltpu.roll(prev, j, 0)
    row = lax.broadcasted_iota(I32, pr.shape, 0)
    first = jnp.where(row < j, pr, xs[0:SUBLANES])
    if tb == SUBLANES:
        return first
    return jnp.concatenate([first, xs[SUBLANES:]], axis=0)


def _rglru_kernel(c_ref, cw_ref, cb_ref, wa_ref, ba_ref, wx_ref, bx_ref, lam_ref, conv0_ref, h0_ref,
                  y_ref, hl_ref, prev_scr, h_scr):
    t = pl.program_id(1)
    tb = c_ref.shape[0]

    @pl.when(t == 0)
    def _():
        prev_scr[...] = conv0_ref[...]
        h_scr[...] = h0_ref[...]

    x = c_ref[:, 0:W_C]
    gate = c_ref[:, W_C:2 * W_C]
    prev = prev_scr[...]
    xc = x * cw_ref[CONV_W - 1:CONV_W, :] + cb_ref[...]
    for j in range(1, CONV_W):
        xc = xc + _shift_rows(x, prev, j) * cw_ref[CONV_W - 1 - j:CONV_W - j, :]
    prev_scr[...] = x[tb - SUBLANES:tb]

    xb = xc.astype(BF16)
    r = jax.nn.sigmoid(_dot(xb, wa_ref[...]) + ba_ref[...])
    i = jax.nn.sigmoid(_dot(xb, wx_ref[...]) + bx_ref[...])
    log_a = -LRU_C * r * jax.nn.softplus(-lam_ref[...])
    A = jnp.exp(log_a)
    th = jnp.tanh(log_a)
    U = jnp.sqrt(-2.0 * th / (1.0 - th)) * (i * xc)
    row = lax.broadcasted_iota(I32, (tb, W_C), 0)
    d = 1
    while d < tb:
        keep = row >= d
        U = jnp.where(keep, U + A * pltpu.roll(U, d, 0), U)
        A = jnp.where(keep, A * pltpu.roll(A, d, 0), A)
        d *= 2
    h = U + A * h_scr[0:1, :]
    h_scr[...] = jnp.broadcast_to(h[tb - 1:tb, :], (SUBLANES, W_C))
    y_ref[...] = (h * jax.nn.gelu(gate)).astype(y_ref.dtype)

    @pl.when(t == pl.num_programs(1) - 1)
    def _():
        hl_ref[...] = h_scr[...]


def _rglru(c, cw, cb, wa, ba, wx, bx, lam, conv0, h0, nb, nt, tb):
    n = c.shape[0]
    vec = pl.BlockSpec((1, W_C), lambda b, t: (0, 0))
    mat = pl.BlockSpec((W_C, W_C), lambda b, t: (0, 0))
    st = pl.BlockSpec((None, SUBLANES, W_C), lambda b, t: (b, 0, 0))
    return pl.pallas_call(
        _rglru_kernel,
        grid=(nb, nt),
        in_specs=[pl.BlockSpec((tb, 2 * W_C), lambda b, t: (b * nt + t, 0)),
                  pl.BlockSpec((CONV_W, W_C), lambda b, t: (0, 0)), vec, mat, vec, mat, vec, vec, st, st],
        out_specs=[pl.BlockSpec((tb, W_C), lambda b, t: (b * nt + t, 0)), st],
        out_shape=[jax.ShapeDtypeStruct((n, W_C), BF16), jax.ShapeDtypeStruct((nb, SUBLANES, W_C), F32)],
        scratch_shapes=[pltpu.VMEM((SUBLANES, W_C), F32), pltpu.VMEM((SUBLANES, W_C), F32)],
        compiler_params=_cparams("parallel", "arbitrary"),
        name="rglru",
    )(c, cw, cb, wa, ba, wx, bx, lam, conv0, h0)


def _sort_key(score):
    score = jnp.where(score == 0.0, 0.0, score)
    bits = lax.bitcast_convert_type(score, I32)
    return bits ^ ((bits >> 31) & 0x7FFFFFFF)


def _kth_largest_key(count_ge, k, rows):
    kf = float(k)
    zero = jnp.zeros((rows, 1), I32)
    t0 = jnp.where(count_ge(zero) >= kf, zero, jnp.full((rows, 1), INT_MIN, I32))

    def body(bi, t):
        cand = t + jnp.left_shift(jnp.int32(1), 30 - bi)
        return jnp.where(count_ge(cand) >= kf, cand, t)

    return lax.fori_loop(0, 31, body, t0)


def _first_positions(count_eq_before, need, nbits, rows):
    def body(bi, x):
        cand = x + jnp.left_shift(jnp.int32(1), nbits - 1 - bi)
        return jnp.where(count_eq_before(cand) < need, cand, x)

    return lax.fori_loop(0, nbits, body, jnp.zeros((rows, 1), I32))


def _qi_heads(qi):
    lane = lax.broadcasted_iota(I32, (qi.shape[0], LANES), 1)
    low = lane < D_IDX
    out = []
    for h in range(H_IDX):
        pair = qi[:, (h // 2) * LANES:(h // 2 + 1) * LANES]
        if h % 2:
            pair = pltpu.roll(pair, D_IDX, 1)
        out.append(jnp.where(low, pair, 0.0).astype(BF16))
    return out


def _dsa_prompt_kernel(q_ref, s_ref, ki_ref, kb_ref, vb_ref, o_ref, key_scr, bias_scr, x_scr, lg_scr,
                       *, q_off, kt_size, topk):
    i = pl.program_id(1)
    QB = q_ref.shape[0]
    KT = kt_size
    nkt = key_scr.shape[0]
    lk = nkt * KT
    q0 = q_off + i * QB
    qih = _qi_heads(q_ref[:, W_B:2 * W_B])
    wi = s_ref[:, S_WI:S_WI + H_IDX]
    wcols = [wi[:, h:h + 1] for h in range(H_IDX)]
    rowpos = q0 + lax.broadcasted_iota(I32, (QB, KT), 0)
    colpos = lax.broadcasted_iota(I32, (QB, KT), 1)

    def score_tile(kt, carry):
        k0 = pl.multiple_of(kt * KT, KT)
        ki = ki_ref[pl.ds(k0, KT), :].astype(BF16)
        sc = jnp.zeros((QB, KT), F32)
        for h in range(H_IDX):
            sc = sc + wcols[h] * jnp.maximum(_dot_nt(qih[h], ki), 0.0)
        sc = jnp.where(colpos + k0 <= rowpos, sc, -jnp.inf)
        key_scr[kt] = _sort_key(sc)
        return carry

    lax.fori_loop(0, nkt, score_tile, 0, unroll=2)

    lanepos = lax.broadcasted_iota(I32, (QB, LANES), 1)

    def count(pred):
        def body(kt, acc):
            tile = key_scr[kt]
            k0 = kt * KT
            for j in range(KT // LANES):
                sl = slice(j * LANES, (j + 1) * LANES)
                acc = acc + jnp.where(pred(tile[:, sl], lanepos + (k0 + j * LANES)), 1.0, 0.0)
            return acc
        acc = lax.fori_loop(0, nkt, body, jnp.zeros((QB, LANES), F32))
        return jnp.sum(acc, axis=1, keepdims=True)

    thr = _kth_largest_key(lambda t: count(lambda key, pos: key >= t), topk, QB)
    cnt_ge = count(lambda key, pos: key >= thr)
    cnt_gt = count(lambda key, pos: key > thr)
    x_scr[...] = jnp.full(x_scr.shape, lk, I32)

    @pl.when(jnp.max(cnt_ge) > float(topk))
    def _():
        need = float(topk) - cnt_gt
        nbits = max(1, (lk - 1).bit_length())
        x = _first_positions(
            lambda c: count(lambda key, pos: jnp.logical_and(key == thr, pos < c)), need, nbits, QB)
        x_scr[...] = jnp.broadcast_to(x, x_scr.shape)

    xlim = x_scr[:, 0:1]

    def bias_tile(kt, carry):
        key = key_scr[kt]
        pos = colpos + kt * KT
        sel = jnp.logical_or(key > thr, jnp.logical_and(key == thr, pos <= xlim))
        bias_scr[kt] = jnp.where(jnp.logical_and(sel, pos <= rowpos), 0.0, NEG)
        return carry

    lax.fori_loop(0, nkt, bias_tile, 0)

    low = lanepos < DH_B
    ngrp = KT // LANES

    def group_fold(op, acc, x):
        for g in range(ngrp):
            acc = op(acc, x[:, g * LANES:(g + 1) * LANES])
        return acc

    for j in range(H_B // 2):
        qpair = q_ref[:, j * LANES:(j + 1) * LANES] * DH_B ** -0.5
        q0m = jnp.where(low, qpair, 0.0).astype(BF16)
        q1m = jnp.where(low, 0.0, qpair).astype(BF16)

        def pass1(kt, carry, q0m=q0m, q1m=q1m, j=j):
            mx0, mx1 = carry
            k0 = pl.multiple_of(kt * KT, KT)
            kk = kb_ref[pl.ds(k0, KT), j * LANES:(j + 1) * LANES]
            bias = bias_scr[kt]
            lg0 = _dot_nt(q0m, kk) + bias
            lg1 = _dot_nt(q1m, kk) + bias
            lg_scr[0, kt] = lg0
            lg_scr[1, kt] = lg1
            return group_fold(jnp.maximum, mx0, lg0), group_fold(jnp.maximum, mx1, lg1)

        neg = jnp.full((QB, LANES), NEG, F32)
        mx0, mx1 = lax.fori_loop(0, nkt, pass1, (neg, neg), unroll=2)
        m0 = jnp.max(mx0, axis=1, keepdims=True)
        m1 = jnp.max(mx1, axis=1, keepdims=True)

        def pass2(kt, carry, m0=m0, m1=m1, j=j):
            l0, l1, a0, a1 = carry
            k0 = pl.multiple_of(kt * KT, KT)
            vv = vb_ref[pl.ds(k0, KT), j * LANES:(j + 1) * LANES]
            p0 = jnp.exp(lg_scr[0, kt] - m0)
            p1 = jnp.exp(lg_scr[1, kt] - m1)
            a0 = a0 + _dot(p0.astype(BF16), vv)
            a1 = a1 + _dot(p1.astype(BF16), vv)
            return group_fold(jnp.add, l0, p0), group_fold(jnp.add, l1, p1), a0, a1

        zero = jnp.zeros((QB, LANES), F32)
        l0, l1, a0, a1 = lax.fori_loop(0, nkt, pass2, (zero, zero, zero, zero), unroll=2)
        out0 = a0 / jnp.sum(l0, axis=1, keepdims=True)
        out1 = a1 / jnp.sum(l1, axis=1, keepdims=True)
        o_ref[:, j * LANES:(j + 1) * LANES] = jnp.where(low, out0, out1).astype(o_ref.dtype)


def _dsa_prompt(bq, s, kb, vb, nb, t):
    n = bq.shape[0]
    QB = math.gcd(t, CHUNK)
    topk = min(TOPK_MAX, t // 4)
    nseg = 1
    for cand in (8, 4, 2):
        if t % cand == 0 and (t // cand) % 512 == 0:
            nseg = cand
            break
    seg = t // nseg
    KT = 512 if seg % 512 == 0 else seg
    s3 = s.reshape(nb, t, LANES)
    kb3 = kb.reshape(nb, t, W_B)
    vb3 = vb.reshape(nb, t, W_B)
    nqb = seg // QB
    outs = []
    for g in range(nseg):
        lk = (g + 1) * seg
        nkt = lk // KT
        row_blk = functools.partial(lambda b, i, g: (b * (t // QB) + g * nqb + i, 0), g=g)
        outs.append(pl.pallas_call(
            functools.partial(_dsa_prompt_kernel, q_off=g * seg, kt_size=KT, topk=topk),
            grid=(nb, nqb),
            in_specs=[pl.BlockSpec((QB, 2 * W_B), row_blk),
                      pl.BlockSpec((QB, LANES), row_blk),
                      pl.BlockSpec((None, lk, LANES), lambda b, i: (b, 0, 0)),
                      pl.BlockSpec((None, lk, W_B), lambda b, i: (b, 0, 0)),
                      pl.BlockSpec((None, lk, W_B), lambda b, i: (b, 0, 0))],
            out_specs=pl.BlockSpec((None, QB, W_B), lambda b, i: (b, i, 0)),
            out_shape=jax.ShapeDtypeStruct((nb, seg, W_B), BF16),
            scratch_shapes=[pltpu.VMEM((nkt, QB, KT), I32), pltpu.VMEM((nkt, QB, KT), F32),
                            pltpu.VMEM((QB, LANES), I32), pltpu.VMEM((2, nkt, QB, KT), F32)],
            compiler_params=_cparams("parallel", "arbitrary"),
            name=f"dsa_prompt_{g}",
        )(bq, s, s3, kb3, vb3))
    return jnp.concatenate(outs, axis=1).reshape(n, W_B)


SCORE_PAGES = 16
ATTN_PAGES = 8


def _dsa_sample_score_kernel(pt_ref, q_ref, s_ref, snew_ref, *rest):
    page_refs, o_ref = rest[:-1], rest[-1]
    p = pl.program_id(1)
    last = pl.num_programs(1) - 1
    T = q_ref.shape[0]
    qi = q_ref[:, W_B:2 * W_B]
    wi = s_ref[:, S_WI:S_WI + H_IDX]
    qst = jnp.concatenate([qi[:, h * D_IDX:(h + 1) * D_IDX] for h in range(H_IDX)], axis=0).astype(BF16)
    wcol = jnp.concatenate([wi[:, h:h + 1] for h in range(H_IDX)], axis=0)

    def scores(ki):
        r = wcol * jnp.maximum(_dot_nt(qst, ki), 0.0)
        sc = r[0:T]
        for h in range(1, H_IDX):
            sc = sc + r[h * T:(h + 1) * T]
        return sc

    @pl.when(p < last)
    def _():
        for j, page_ref in enumerate(page_refs):
            o_ref[j] = scores(page_ref[...].astype(BF16))

    @pl.when(p == last)
    def _():
        sc = scores(snew_ref[:, S_KI:S_KI + D_IDX].astype(BF16))
        r = lax.broadcasted_iota(I32, sc.shape, 0)
        c = lax.broadcasted_iota(I32, sc.shape, 1)
        o_ref[0] = jnp.where(c <= r, sc, -jnp.inf)
        for j in range(1, len(page_refs)):
            o_ref[j] = jnp.full(sc.shape, -jnp.inf, F32)


def _dsa_sample_attn_kernel(pt_ref, q_ref, sc_ref, knew_ref, vnew_ref, *rest, topk, npg, G):
    kpages, vpages, o_ref = rest[:G], rest[G:2 * G], rest[2 * G]
    thr_scr, x_scr, m_scr, l_scr, acc_scr, kf_scr, vf_scr = rest[2 * G + 1:]
    p = pl.program_id(1)
    last = pl.num_programs(1) - 1
    T = q_ref.shape[0]
    R = H_B * T
    ltot = sc_ref.shape[0] * PAGE_SIZE

    @pl.when(p == 0)
    def _():
        key = _sort_key(sc_ref[...])
        pos = (lax.broadcasted_iota(I32, key.shape, 0) * PAGE_SIZE
               + lax.broadcasted_iota(I32, key.shape, 2))

        def count(pred):
            per_lane = jnp.sum(jnp.where(pred(key, pos), 1.0, 0.0), axis=0)
            return jnp.sum(per_lane, axis=1, keepdims=True)

        thr = _kth_largest_key(lambda t: count(lambda k_, p_: k_ >= t), topk, T)
        cnt_gt = count(lambda k_, p_: k_ > thr)
        nbits = max(1, (ltot - 1).bit_length())
        x = _first_positions(lambda c: count(lambda k_, p_: jnp.logical_and(k_ == thr, p_ < c)),
                             float(topk) - cnt_gt, nbits, T)
        thr_scr[...] = jnp.broadcast_to(thr, thr_scr.shape)
        x_scr[...] = jnp.broadcast_to(x, x_scr.shape)
        m_scr[...] = jnp.full(m_scr.shape, NEG, F32)
        l_scr[...] = jnp.zeros_like(l_scr)
        acc_scr[...] = jnp.zeros_like(acc_scr)

    q = q_ref[:, 0:W_B] * DH_B ** -0.5
    qrep = jnp.concatenate([q] * H_B, axis=0)
    rr = lax.broadcasted_iota(I32, (R, W_B), 0) // T
    cc = lax.broadcasted_iota(I32, (R, W_B), 1) // DH_B
    diag = rr == cc
    qbd = jnp.where(diag, qrep, 0.0).astype(BF16)
    thr = thr_scr[:, 0:1]
    xlim = x_scr[:, 0:1]

    def selected(page):
        key = _sort_key(sc_ref[page])
        pos = page * PAGE_SIZE + lax.broadcasted_iota(I32, key.shape, 1)
        return jnp.logical_or(key > thr, jnp.logical_and(key == thr, pos <= xlim))

    def step(kk, vv, valid):
        bias = jnp.where(valid, 0.0, NEG)
        lg = _dot_nt(qbd, kk) + jnp.concatenate([bias] * H_B, axis=0)
        m = m_scr[:, 0:1]
        m_new = jnp.maximum(m, jnp.max(lg, axis=1, keepdims=True))
        pr = jnp.exp(lg - m_new)
        alpha = jnp.exp(m - m_new)
        l_scr[...] = jnp.broadcast_to(alpha * l_scr[:, 0:1] + jnp.sum(pr, axis=1, keepdims=True), l_scr.shape)
        acc_scr[...] = alpha * acc_scr[...] + _dot(pr.astype(BF16), vv)
        m_scr[...] = jnp.broadcast_to(m_new, m_scr.shape)

    @pl.when(p < last)
    def _():
        for j in range(G):
            for h in range(H_B):
                kf_scr[j, :, h * DH_B:(h + 1) * DH_B] = kpages[j][:, h, :].astype(BF16)
                vf_scr[j, :, h * DH_B:(h + 1) * DH_B] = vpages[j][:, h, :].astype(BF16)
            step(kf_scr[j], vf_scr[j], selected(p * G + j))

    @pl.when(p == last)
    def _():
        sel = selected(npg)
        r = lax.broadcasted_iota(I32, sel.shape, 0)
        c = lax.broadcasted_iota(I32, sel.shape, 1)
        step(knew_ref[...], vnew_ref[...], jnp.logical_and(sel, c <= r))
        out = jnp.where(diag, acc_scr[...] / l_scr[:, 0:1], 0.0)
        res = out[0:T]
        for h in range(1, H_B):
            res = res + out[h * T:(h + 1) * T]
        o_ref[...] = res.astype(o_ref.dtype)


def _dsa_sample(bq, s, kb, vb, pool_k, pool_v, pool_ki, page_table, nb, t):
    npg = page_table.shape[1]
    past = npg * PAGE_SIZE
    topk = min(TOPK_MAX, (past + t) // 4)
    gs, ga = math.gcd(npg, SCORE_PAGES), math.gcd(npg, ATTN_PAGES)
    pad = PAGE_SIZE - t
    s_new = jnp.pad(s.reshape(nb, t, LANES), ((0, 0), (0, pad), (0, 0)))
    k_new = jnp.pad(kb.reshape(nb, t, W_B), ((0, 0), (0, pad), (0, 0)))
    v_new = jnp.pad(vb.reshape(nb, t, W_B), ((0, 0), (0, pad), (0, 0)))
    lastp = npg - 1
    ptot = npg + gs

    def page_spec(tail, j, group):
        zeros = (0,) * len(tail)
        return pl.BlockSpec((None, None) + tail,
                            lambda b, p, pt: (0, pt[b, jnp.minimum(p * group + j, lastp)]) + zeros)

    scores = pl.pallas_call(
        _dsa_sample_score_kernel,
        grid_spec=pltpu.PrefetchScalarGridSpec(
            num_scalar_prefetch=1,
            grid=(nb, npg // gs + 1),
            in_specs=[pl.BlockSpec((t, 2 * W_B), lambda b, p, pt: (b, 0)),
                      pl.BlockSpec((t, LANES), lambda b, p, pt: (b, 0)),
                      pl.BlockSpec((None, PAGE_SIZE, LANES), lambda b, p, pt: (b, 0, 0))]
            + [page_spec((PAGE_SIZE, D_IDX), j, gs) for j in range(gs)],
            out_specs=pl.BlockSpec((None, gs, t, PAGE_SIZE), lambda b, p, pt: (b, p, 0, 0))),
        out_shape=jax.ShapeDtypeStruct((nb, ptot, t, PAGE_SIZE), F32),
        compiler_params=_cparams("parallel", "arbitrary"),
        name="dsa_sample_scores",
    )(page_table, bq, s, s_new, *([pool_ki] * gs))

    kv_tail = (PAGE_SIZE, H_B, DH_B)
    return pl.pallas_call(
        functools.partial(_dsa_sample_attn_kernel, topk=topk, npg=npg, G=ga),
        grid_spec=pltpu.PrefetchScalarGridSpec(
            num_scalar_prefetch=1,
            grid=(nb, npg // ga + 1),
            in_specs=[pl.BlockSpec((t, 2 * W_B), lambda b, p, pt: (b, 0)),
                      pl.BlockSpec((None, ptot, t, PAGE_SIZE), lambda b, p, pt: (b, 0, 0, 0)),
                      pl.BlockSpec((None, PAGE_SIZE, W_B), lambda b, p, pt: (b, 0, 0)),
                      pl.BlockSpec((None, PAGE_SIZE, W_B), lambda b, p, pt: (b, 0, 0))]
            + [page_spec(kv_tail, j, ga) for j in range(ga)]
            + [page_spec(kv_tail, j, ga) for j in range(ga)],
            out_specs=pl.BlockSpec((t, W_B), lambda b, p, pt: (b, 0)),
            scratch_shapes=[pltpu.VMEM((t, LANES), I32), pltpu.VMEM((t, LANES), I32),
                            pltpu.VMEM((H_B * t, LANES), F32), pltpu.VMEM((H_B * t, LANES), F32),
                            pltpu.VMEM((H_B * t, W_B), F32),
                            pltpu.VMEM((ga, PAGE_SIZE, W_B), BF16),
                            pltpu.VMEM((ga, PAGE_SIZE, W_B), BF16)]),
        out_shape=jax.ShapeDtypeStruct((nb * t, W_B), BF16),
        compiler_params=_cparams("parallel", "arbitrary"),
        name="dsa_sample_attn",
    )(page_table, bq, scores, k_new, v_new, *([pool_k] * ga), *([pool_v] * ga))


def _pad_chunks(a, nb, t, fill=None):
    c = a.shape[1]
    a3 = a.reshape(nb, t, c)
    if fill is None:
        a3 = jnp.pad(a3, ((0, 0), (0, CHUNK - t), (0, 0)))
    else:
        a3 = jnp.concatenate([a3, jnp.broadcast_to(fill, (nb, CHUNK - t, c))], axis=1)
    return a3.reshape(nb * CHUNK, c)


def _retention_tables(t_true, pos0, t_pad):
    L = CHUNK
    lt = min(L, t_true)
    lg = jnp.log1p(-jnp.exp2(-5.0 - jnp.arange(H_D, dtype=F32)))
    j = jnp.arange(L, dtype=F32)
    causal = jnp.tril(jnp.ones((L, L), dtype=bool))
    dmat = jnp.exp(jnp.where(causal, (j[:, None] - j[None, :]) * lg[:, None, None], -jnp.inf))
    qdec = jnp.exp((j + 1.0) * lg[:, None])
    kdec = jnp.where(j < lt, jnp.exp((lt - 1.0 - j) * lg[:, None]), 0.0)
    sdec = jnp.broadcast_to(jnp.exp(lt * lg)[:, None], (H_D, L))
    dec = jnp.zeros((H_D, L, LANES), F32)
    dec = dec.at[:, :, 0].set(qdec).at[:, :, 1].set(kdec).at[:, :, 2].set(sdec)
    half = DK_D // 2
    freq = ROPE_BASE ** (-jnp.arange(half, dtype=F32) / half)
    pos = (pos0 + jnp.arange(t_pad)).astype(F32)
    ang = pos[:, None] * freq[None, :]
    cos, sin = jnp.cos(ang), jnp.sin(ang)
    return dmat, dec, jnp.concatenate([cos, cos], -1), jnp.concatenate([-sin, sin], -1)


def _forward(x3, st, prm):
    nb, t, d = x3.shape
    n = nb * t
    x = x3.reshape(n, d)
    short = t < CHUNK
    t_pad = CHUNK if short else t
    nc = t_pad // CHUNK
    new = {}

    a, bq, k, v, s, kb, vb = _inproj_e(x, prm["w_in_e"], prm["bias_e"])
    if st is None:
        c0 = jnp.zeros((nb, H_A, DK_A, DV_A), F32)
        n0 = jnp.zeros((nb, H_A, DK_A), F32)
        m0 = jnp.zeros((nb, H_A), F32)
    else:
        c0, n0, m0 = st["mlstm_C"][0], st["mlstm_n"][0], st["mlstm_m"][0]
    m0b = jnp.broadcast_to(m0[:, :, None], (nb, H_A, LANES))
    if short:
        lane = jnp.arange(LANES)
        fill = jnp.where((lane >= S_IG) & (lane < S_IG + H_A), NEG,
                         jnp.where((lane >= S_FG) & (lane < S_FG + H_A), 1e4, 0.0)).astype(F32)
        a_m, s_m = _pad_chunks(a, nb, t), _pad_chunks(s, nb, t, fill)
    else:
        a_m, s_m = a, s
    ya, c1, n1, m1 = _mlstm(a_m, s_m, prm["a_norm_g"], c0, n0, m0b, nb, nc)
    if short:
        ya = ya.reshape(nb, CHUNK, W_A)[:, :t].reshape(n, W_A)
    if st is None:
        yb = _dsa_prompt(bq, s, kb, vb, nb, t)
    else:
        yb = _dsa_sample(bq, s, kb, vb, st["pool_k"], st["pool_v"], st["pool_ki"], st["page_table"], nb, t)
    new["mlstm_C"], new["mlstm_n"], new["mlstm_m"] = c1[None], n1[None], m1[None, :, :, 0]
    new["k"] = k.reshape(1, nb, t, H_B, DH_B)
    new["v"] = v.reshape(1, nb, t, H_B, DH_B)
    new["kidx"] = s[:, S_KI:S_KI + D_IDX].reshape(1, nb, t, D_IDX)
    x = _outproj_ln(ya, yb, prm["w_out_e"], x, prm["ln_g"][0, 0], prm["ln_b"][0, 0])
    x = _moe_ln(x, prm["router_w"], prm["router_b"], prm["moe_wg"][0], prm["moe_wu"][0], prm["moe_wd"][0],
                prm["ln_g"][0, 1], prm["ln_b"][0, 1])

    c, dd = _inproj_o(x, prm["w_in_o"])
    if st is None:
        conv0 = jnp.zeros((nb, SUBLANES, W_C), F32)
        h0 = jnp.zeros((nb, SUBLANES, W_C), F32)
        s0 = jnp.zeros((nb, H_D, DK_D, DV_D), F32)
        pos0 = 0
    else:
        conv0 = jnp.pad(st["conv"][0], ((0, 0), (SUBLANES - (CONV_W - 1), 0), (0, 0)))
        h0 = jnp.broadcast_to(st["lru_h"][0][:, None, :], (nb, SUBLANES, W_C))
        s0 = st["ret_S"][0]
        pos0 = st["page_table"].shape[1] * PAGE_SIZE
    tb = _row_block(t, 256)
    yc, hl = _rglru(c, prm["c_conv_w"], prm["c_conv_b"], prm["c_wa"], prm["c_ba"], prm["c_wx"], prm["c_bx"],
                    prm["c_lambda"], conv0, h0, nb, t // tb, tb)
    dmat, dec, cos2, sin2 = _retention_tables(t, pos0, t_pad)
    d_m = _pad_chunks(dd, nb, t) if short else dd
    yd, s1 = _retention(d_m, cos2, sin2, dmat, dec, prm["d_norm_g"], s0, nb, nc)
    if short:
        yd = yd.reshape(nb, CHUNK, W_D)[:, :t].reshape(n, W_D)
    new["conv"] = c[:, 0:W_C].reshape(nb, t, W_C)[None, :, t - (CONV_W - 1):]
    new["lru_h"] = hl[None, :, 0]
    new["ret_S"] = s1[None]
    x = _outproj_ln(yc, yd, prm["w_out_o"], x, prm["ln_g"][1, 0], prm["ln_b"][1, 0])
    x = _moe_ln(x, prm["router_w"], prm["router_b"], prm["moe_wg"][1], prm["moe_wu"][1], prm["moe_wd"][1],
                prm["ln_g"][1, 1], prm["ln_b"][1, 1])
    return x.reshape(nb, t, d), new


def _block_diag(w):
    nblk, blk, _ = w.shape
    eye = jnp.eye(nblk, dtype=w.dtype)
    return (eye[:, None, :, None] * w[:, :, None, :]).reshape(nblk * blk, nblk * blk)


def _prepare_params(w_in_e, b_if_e, a_norm_g, w_out_e, w_in_o, c_conv_w, c_conv_b, c_wa, c_ba, c_wx, c_bx,
                    c_lambda, d_norm_g, w_out_o, router_w, router_bias, moe_w_gate, moe_w_up, moe_w_down,
                    ln_g, ln_b):
    d = w_in_e.shape[1]
    o = [0]
    for width in (W_A, W_A, W_A, W_A, 2 * H_A, W_B, W_B, W_B, H_IDX * D_IDX, D_IDX, H_IDX):
        o.append(o[-1] + width)
    we = w_in_e[0]
    col = lambda i: we[:, o[i]:o[i + 1]]
    slab = jnp.zeros((d, LANES), F32)
    slab = slab.at[:, S_KI:S_KI + D_IDX].set(col(9)).at[:, S_IG:S_IG + 2 * H_A].set(col(4))
    slab = slab.at[:, S_WI:S_WI + H_IDX].set(col(10))
    w_e = jnp.concatenate([col(0), col(1), col(2), col(3), col(5), col(8), col(6), col(7), slab], axis=1)
    bias_e = jnp.zeros((1, LANES), F32).at[0, S_IG:S_IG + 2 * H_A].set(b_if_e[0])
    rw = jnp.zeros((d, LANES), F32).at[:, :N_EXPERTS].set(router_w)
    rb = jnp.zeros((1, LANES), F32).at[0, :N_EXPERTS].set(router_bias)
    row = lambda a: a.reshape(1, -1).astype(F32)
    return {
        "w_in_e": w_e.astype(BF16), "bias_e": bias_e, "a_norm_g": row(a_norm_g[0]),
        "w_out_e": w_out_e[0].astype(BF16), "w_in_o": w_in_o[0].astype(BF16),
        "c_conv_w": c_conv_w[0].astype(F32), "c_conv_b": row(c_conv_b[0]),
        "c_wa": _block_diag(c_wa[0]).astype(BF16), "c_ba": row(c_ba[0]),
        "c_wx": _block_diag(c_wx[0]).astype(BF16), "c_bx": row(c_bx[0]),
        "c_lambda": row(c_lambda[0]), "d_norm_g": row(d_norm_g[0]), "w_out_o": w_out_o[0].astype(BF16),
        "router_w": rw, "router_b": rb,
        "moe_wg": moe_w_gate.astype(BF16), "moe_wu": moe_w_up.astype(BF16), "moe_wd": moe_w_down.astype(BF16),
        "ln_g": ln_g.reshape(DEPTH, 2, 1, -1).astype(F32), "ln_b": ln_b.reshape(DEPTH, 2, 1, -1).astype(F32),
    }


def kernel(x_prompt, x_sample, state_mlstm_C, state_mlstm_n, state_mlstm_m, cache_k, cache_v, cache_kidx,
           page_table, state_conv, state_lru_h, state_ret_S, w_in_e, b_if_e, a_norm_g, w_out_e, w_in_o,
           c_conv_w, c_conv_b, c_wa, c_ba, c_wx, c_bx, c_lambda, d_norm_g, w_out_o, router_w, router_bias,
           moe_w_gate, moe_w_up, moe_w_down, ln_g, ln_b):
    prm = _prepare_params(w_in_e, b_if_e, a_norm_g, w_out_e, w_in_o, c_conv_w, c_conv_b, c_wa, c_ba, c_wx,
                          c_bx, c_lambda, d_norm_g, w_out_o, router_w, router_bias, moe_w_gate, moe_w_up,
                          moe_w_down, ln_g, ln_b)
    st = {"mlstm_C": state_mlstm_C, "mlstm_n": state_mlstm_n, "mlstm_m": state_mlstm_m,
          "pool_k": cache_k, "pool_v": cache_v, "pool_ki": cache_kidx, "page_table": page_table,
          "conv": state_conv, "lru_h": state_lru_h, "ret_S": state_ret_S}
    y_p, nsp = _forward(x_prompt, None, prm)
    y_s, nss = _forward(x_sample, st, prm)
    names = ("mlstm_C", "mlstm_n", "mlstm_m", "k", "v", "kidx", "conv", "lru_h", "ret_S")
    return (y_p, y_s) + tuple(nsp[k] for k in names) + tuple(nss[k] for k in names)
```

```python
import functools
import math

import jax
import jax.numpy as jnp
from jax import lax
from jax.experimental import pallas as pl
from jax.experimental.pallas import tpu as pltpu

F32 = jnp.float32
BF16 = jnp.bfloat16
I32 = jnp.int32

DEPTH = 2
PAGE_SIZE = 128
H_A, DK_A, DV_A = 4, 128, 128
W_A = H_A * DV_A
H_B, DH_B = 8, 64
W_B = H_B * DH_B
H_IDX, D_IDX = 8, 64
TOPK_MAX = 256
W_C, N_BLK_C, CONV_W, LRU_C = 512, 8, 4, 8.0
BLK_C = W_C // N_BLK_C
H_D, DK_D, DV_D = 4, 128, 128
W_D = H_D * DV_D
ROPE_BASE = 10000.0
N_EXPERTS, N_GROUPS, TOP_K_EXP, D_FF_EXP = 16, 4, 2, 512
EXP_PER_GROUP = N_EXPERTS // N_GROUPS
ALPHA = (2 * DEPTH) ** 0.25
LN_EPS = 1e-5
HN_EPS = 1e-6

CHUNK = 128
LANES = 128
SUBLANES = 8
NEG = -1e30
INT_MIN = -2 ** 31
VMEM_LIMIT = 56 * 1024 * 1024

S_KI = 0
S_IG = 64
S_FG = 68
S_WI = 72


def _cparams(*sem):
    return pltpu.CompilerParams(dimension_semantics=sem, vmem_limit_bytes=VMEM_LIMIT)


def _precision(a, b):
    return lax.Precision.HIGHEST if a.dtype == F32 and b.dtype == F32 else None


def _dot(a, b):
    return jnp.dot(a, b, preferred_element_type=F32, precision=_precision(a, b))


def _dot_nt(a, b):
    return lax.dot_general(a, b, (((1,), (1,)), ((), ())), preferred_element_type=F32, precision=_precision(a, b))


def _dot_tn(a, b):
    return lax.dot_general(a, b, (((0,), (0,)), ((), ())), preferred_element_type=F32, precision=_precision(a, b))


def _row_block(n, target):
    t = min(n, target)
    while n % t:
        t //= 2
    return t


def _layer_norm(z, g, b):
    mu = jnp.mean(z, -1, keepdims=True)
    zc = z - mu
    var = jnp.mean(zc * zc, -1, keepdims=True)
    return zc * lax.rsqrt(var + LN_EPS) * g + b


def _head_norm(h):
    mu = jnp.mean(h, -1, keepdims=True)
    hc = h - mu
    var = jnp.mean(hc * hc, -1, keepdims=True)
    return hc * lax.rsqrt(var + HN_EPS)


def _inproj_e_kernel(x_ref, w_ref, bias_ref, a_ref, bq_ref, k_ref, v_ref, s_ref, kb_ref, vb_ref):
    x = x_ref[...].astype(w_ref.dtype)

    def mm(lo, hi):
        return _dot(x, w_ref[:, lo:hi])

    a_ref[:, 0:W_A] = mm(0, W_A)
    a_ref[:, W_A:2 * W_A] = mm(W_A, 2 * W_A) * DK_A ** -0.5
    a_ref[:, 2 * W_A:4 * W_A] = mm(2 * W_A, 4 * W_A)
    o = 4 * W_A
    bq_ref[...] = mm(o, o + 2 * W_B)
    o += 2 * W_B
    k = mm(o, o + W_B)
    k_ref[...] = k
    kb_ref[...] = k.astype(kb_ref.dtype)
    o += W_B
    v = mm(o, o + W_B)
    v_ref[...] = v
    vb_ref[...] = v.astype(vb_ref.dtype)
    o += W_B
    s_ref[...] = mm(o, o + LANES) + bias_ref[...]


def _inproj_e(x, w, bias):
    n, d = x.shape
    tm = _row_block(n, 256)
    wcols = w.shape[1]
    outs = [(4 * W_A, F32), (2 * W_B, F32), (W_B, F32), (W_B, F32), (LANES, F32), (W_B, w.dtype), (W_B, w.dtype)]
    return pl.pallas_call(
        _inproj_e_kernel,
        grid=(n // tm,),
        in_specs=[pl.BlockSpec((tm, d), lambda i: (i, 0)),
                  pl.BlockSpec((d, wcols), lambda i: (0, 0)),
                  pl.BlockSpec((1, LANES), lambda i: (0, 0))],
        out_specs=[pl.BlockSpec((tm, c), lambda i: (i, 0)) for c, _ in outs],
        out_shape=[jax.ShapeDtypeStruct((n, c), dt) for c, dt in outs],
        compiler_params=_cparams("parallel"),
        name="inproj_even",
    )(x, w, bias)


def _inproj_o_kernel(x_ref, w_ref, c_ref, d_ref):
    x = x_ref[...].astype(w_ref.dtype)
    c_ref[...] = _dot(x, w_ref[:, 0:2 * W_C])
    d_ref[...] = _dot(x, w_ref[:, 2 * W_C:2 * W_C + 4 * W_D])


def _inproj_o(x, w):
    n, d = x.shape
    tm = _row_block(n, 256)
    return pl.pallas_call(
        _inproj_o_kernel,
        grid=(n // tm,),
        in_specs=[pl.BlockSpec((tm, d), lambda i: (i, 0)),
                  pl.BlockSpec(w.shape, lambda i: (0, 0))],
        out_specs=[pl.BlockSpec((tm, 2 * W_C), lambda i: (i, 0)),
                   pl.BlockSpec((tm, 4 * W_D), lambda i: (i, 0))],
        out_shape=[jax.ShapeDtypeStruct((n, 2 * W_C), F32), jax.ShapeDtypeStruct((n, 4 * W_D), F32)],
        compiler_params=_cparams("parallel"),
        name="inproj_odd",
    )(x, w)


def _outproj_ln_kernel(y1_ref, y2_ref, w_ref, x_ref, g_ref, b_ref, o_ref):
    half = y1_ref.shape[1]
    y = _dot(y1_ref[...], w_ref[0:half, :]) + _dot(y2_ref[...], w_ref[half:2 * half, :])
    o_ref[...] = _layer_norm(ALPHA * x_ref[...] + y, g_ref[...], b_ref[...])


def _outproj_ln(y1, y2, w, x, g, b):
    n, d = x.shape
    tm = _row_block(n, 512)
    half = y1.shape[1]
    return pl.pallas_call(
        _outproj_ln_kernel,
        grid=(n // tm,),
        in_specs=[pl.BlockSpec((tm, half), lambda i: (i, 0)),
                  pl.BlockSpec((tm, half), lambda i: (i, 0)),
                  pl.BlockSpec(w.shape, lambda i: (0, 0)),
                  pl.BlockSpec((tm, d), lambda i: (i, 0)),
                  pl.BlockSpec((1, d), lambda i: (0, 0)),
                  pl.BlockSpec((1, d), lambda i: (0, 0))],
        out_specs=pl.BlockSpec((tm, d), lambda i: (i, 0)),
        out_shape=jax.ShapeDtypeStruct((n, d), F32),
        compiler_params=_cparams("parallel"),
        name="outproj_ln",
    )(y1, y2, w, x, g, b)


def _route(logits, bias):
    lane = lax.broadcasted_iota(I32, logits.shape, 1)
    valid = lane < N_EXPERTS
    pos = lane % EXP_PER_GROUP
    grp = (lane // EXP_PER_GROUP).astype(F32)
    s = jax.nn.sigmoid(logits)
    sel = jnp.where(valid, s + bias, NEG)
    rank = jnp.zeros(logits.shape, F32)
    for d in range(1, EXP_PER_GROUP):
        lo = pltpu.roll(sel, d, 1)
        hi = pltpu.roll(sel, LANES - d, 1)
        rank = rank + jnp.where(jnp.logical_and(pos >= d, lo >= sel), 1.0, 0.0)
        rank = rank + jnp.where(jnp.logical_and(pos + d < EXP_PER_GROUP, hi > sel), 1.0, 0.0)
    top2 = jnp.logical_and(rank < TOP_K_EXP, valid)
    contrib = jnp.where(top2, sel, 0.0)
    gs = contrib
    for d in range(1, EXP_PER_GROUP):
        lo = pltpu.roll(contrib, d, 1)
        hi = pltpu.roll(contrib, LANES - d, 1)
        gs = gs + jnp.where(pos >= d, lo, 0.0) + jnp.where(pos + d < EXP_PER_GROUP, hi, 0.0)
    gs = jnp.where(valid, gs, NEG)
    gmax = jnp.max(gs, axis=1, keepdims=True)
    best = jnp.min(jnp.where(gs == gmax, grp, 1e9), axis=1, keepdims=True)
    chosen = jnp.logical_and(top2, grp == best)
    s_sel = jnp.where(chosen, s, 0.0)
    return s_sel / jnp.sum(s_sel, axis=1, keepdims=True)


def _moe_kernel(x_ref, rw_ref, rb_ref, wg_ref, wu_ref, wd_ref, g_ref, b_ref, o_ref,
                xb_scr, comb_scr, acc_scr, *, sub):
    e = pl.program_id(1)
    tm = x_ref.shape[0]

    @pl.when(e == 0)
    def _():
        xb_scr[...] = x_ref[...].astype(xb_scr.dtype)
        logits = jnp.dot(x_ref[...], rw_ref[...], precision=lax.Precision.HIGHEST, preferred_element_type=F32)
        comb_scr[...] = _route(logits, rb_ref[...])
        acc_scr[...] = jnp.zeros_like(acc_scr)

    def rows(r, carry):
        r0 = pl.multiple_of(r * sub, sub)
        xb = xb_scr[pl.ds(r0, sub), :]
        gate = _dot(xb, wg_ref[...])
        h = gate * jax.nn.sigmoid(gate) * _dot(xb, wu_ref[...])
        y = _dot(h.astype(wd_ref.dtype), wd_ref[...])
        comb = comb_scr[pl.ds(r0, sub), :]
        lane = lax.broadcasted_iota(I32, comb.shape, 1)
        c_e = jnp.sum(jnp.where(lane == e, comb, 0.0), axis=1, keepdims=True)
        acc_scr[pl.ds(r0, sub), :] += c_e * y
        return carry

    lax.fori_loop(0, tm // sub, rows, 0)

    @pl.when(e == N_EXPERTS - 1)
    def _():
        o_ref[...] = _layer_norm(ALPHA * x_ref[...] + acc_scr[...], g_ref[...], b_ref[...])


def _moe_ln(x, rw, rb, wg, wu, wd, g, b):
    n, d = x.shape
    tm = _row_block(n, 1024)
    sub = _row_block(tm, 256)
    f = wg.shape[2]
    return pl.pallas_call(
        functools.partial(_moe_kernel, sub=sub),
        grid=(n // tm, N_EXPERTS),
        in_specs=[pl.BlockSpec((tm, d), lambda i, e: (i, 0)),
                  pl.BlockSpec((d, LANES), lambda i, e: (0, 0)),
                  pl.BlockSpec((1, LANES), lambda i, e: (0, 0)),
                  pl.BlockSpec((None, d, f), lambda i, e: (e, 0, 0)),
                  pl.BlockSpec((None, d, f), lambda i, e: (e, 0, 0)),
                  pl.BlockSpec((None, f, d), lambda i, e: (e, 0, 0)),
                  pl.BlockSpec((1, d), lambda i, e: (0, 0)),
                  pl.BlockSpec((1, d), lambda i, e: (0, 0))],
        out_specs=pl.BlockSpec((tm, d), lambda i, e: (i, 0)),
        out_shape=jax.ShapeDtypeStruct((n, d), F32),
        scratch_shapes=[pltpu.VMEM((tm, d), wg.dtype), pltpu.VMEM((tm, LANES), F32), pltpu.VMEM((tm, d), F32)],
        compiler_params=_cparams("parallel", "arbitrary"),
        name="moe_ln",
    )(x, rw, rb, wg, wu, wd, g, b)


def _mlstm_kernel(a_ref, s_ref, g_ref, c0_ref, n0_ref, m0_ref, y_ref, c1_ref, n1_ref, m1_ref,
                  c_scr, n_scr, m_scr):
    c = pl.program_id(1)
    L = a_ref.shape[0]
    mdt = y_ref.dtype

    @pl.when(c == 0)
    def _():
        c_scr[...] = c0_ref[...]
        n_scr[...] = n0_ref[...]
        m_scr[...] = m0_ref[...]

    S = s_ref[...]
    lane = lax.broadcasted_iota(I32, S.shape, 1)
    is_f = jnp.logical_and(lane >= S_FG, lane < S_FG + H_A)
    lf = jnp.where(is_f, jax.nn.log_sigmoid(S), 0.0)
    row = lax.broadcasted_iota(I32, (L, L), 0)
    col = lax.broadcasted_iota(I32, (L, L), 1)
    causal = row >= col
    Fs = jnp.dot(causal.astype(F32), lf, precision=lax.Precision.HIGHEST, preferred_element_type=F32)
    Fa = pltpu.roll(Fs, LANES - (S_FG - S_IG), 1)
    AT = jnp.transpose(S - Fa)

    for h in range(H_A):
        ig = S[:, S_IG + h:S_IG + h + 1]
        F = Fa[:, S_IG + h:S_IG + h + 1]
        a_row = AT[S_IG + h:S_IG + h + 1, :]
        m_prev = m_scr[h:h + 1, 0:1]
        cm = jnp.max(jnp.where(causal, a_row, NEG), axis=1, keepdims=True)
        m_t = F + jnp.maximum(m_prev, cm)
        dmat = jnp.exp(jnp.where(causal, (F - m_t) + a_row, NEG))
        inter = jnp.exp(F + m_prev - m_t)
        q = a_ref[:, h * DK_A:(h + 1) * DK_A].astype(mdt)
        kf = a_ref[:, W_A + h * DK_A:W_A + (h + 1) * DK_A]
        k = kf.astype(mdt)
        v = a_ref[:, 2 * W_A + h * DV_A:2 * W_A + (h + 1) * DV_A].astype(mdt)
        C = c_scr[h]
        n = n_scr[h:h + 1, :]
        s = _dot_nt(q, k) * dmat
        num = _dot(s.astype(mdt), v) + inter * _dot(q, C.astype(mdt))
        qn = jnp.sum(q.astype(F32) * n.astype(mdt).astype(F32), axis=1, keepdims=True)
        den = jnp.sum(s, axis=1, keepdims=True) + inter * qn
        hout = num / jnp.maximum(jnp.abs(den), jnp.exp(-m_t))
        o = a_ref[:, 3 * W_A + h * DV_A:3 * W_A + (h + 1) * DV_A]
        y = _head_norm(hout) * g_ref[:, h * DV_A:(h + 1) * DV_A] * jax.nn.sigmoid(o)
        y_ref[:, h * DV_A:(h + 1) * DV_A] = y.astype(y_ref.dtype)
        m_new = m_t[L - 1:L, :]
        F_last = F[L - 1:L, :]
        w_s = jnp.exp(F_last - F + ig - m_new)
        decay = jnp.exp(F_last + m_prev - m_new)
        kw = kf * w_s
        c_scr[h] = decay * C + _dot_tn(kw.astype(mdt), v)
        n_scr[h:h + 1, :] = decay * n + jnp.sum(kw, axis=0, keepdims=True)
        m_scr[h:h + 1, :] = jnp.broadcast_to(m_new, (1, LANES))

    @pl.when(c == pl.num_programs(1) - 1)
    def _():
        c1_ref[...] = c_scr[...]
        n1_ref[...] = n_scr[...]
        m1_ref[...] = m_scr[...]


def _mlstm(a, s, gnorm, c0, n0, m0b, nb, nc, odt):
    n = a.shape[0]
    L = CHUNK
    return pl.pallas_call(
        _mlstm_kernel,
        grid=(nb, nc),
        in_specs=[pl.BlockSpec((L, 4 * W_A), lambda b, c: (b * nc + c, 0)),
                  pl.BlockSpec((L, LANES), lambda b, c: (b * nc + c, 0)),
                  pl.BlockSpec((1, W_A), lambda b, c: (0, 0)),
                  pl.BlockSpec((None, H_A, DK_A, DV_A), lambda b, c: (b, 0, 0, 0)),
                  pl.BlockSpec((None, H_A, DK_A), lambda b, c: (b, 0, 0)),
                  pl.BlockSpec((None, H_A, LANES), lambda b, c: (b, 0, 0))],
        out_specs=[pl.BlockSpec((L, W_A), lambda b, c: (b * nc + c, 0)),
                   pl.BlockSpec((None, H_A, DK_A, DV_A), lambda b, c: (b, 0, 0, 0)),
                   pl.BlockSpec((None, H_A, DK_A), lambda b, c: (b, 0, 0)),
                   pl.BlockSpec((None, H_A, LANES), lambda b, c: (b, 0, 0))],
        out_shape=[jax.ShapeDtypeStruct((n, W_A), odt),
                   jax.ShapeDtypeStruct((nb, H_A, DK_A, DV_A), F32),
                   jax.ShapeDtypeStruct((nb, H_A, DK_A), F32),
                   jax.ShapeDtypeStruct((nb, H_A, LANES), F32)],
        scratch_shapes=[pltpu.VMEM((H_A, DK_A, DV_A), F32), pltpu.VMEM((H_A, DK_A), F32),
                        pltpu.VMEM((H_A, LANES), F32)],
        compiler_params=_cparams("parallel", "arbitrary"),
        name="mlstm",
    )(a, s, gnorm, c0, n0, m0b)


def _retention_kernel(d_ref, cos_ref, sin_ref, dm_ref, dec_ref, g_ref, s0_ref, y_ref, s1_ref, s_scr):
    c = pl.program_id(1)

    @pl.when(c == 0)
    def _():
        s_scr[...] = s0_ref[...]

    cos2 = cos_ref[...]
    sin2 = sin_ref[...]
    mdt = y_ref.dtype
    for h in range(H_D):
        qf = d_ref[:, h * DK_D:(h + 1) * DK_D]
        kf = d_ref[:, W_D + h * DK_D:W_D + (h + 1) * DK_D]
        q = qf * cos2 + pltpu.roll(qf, DK_D // 2, 1) * sin2
        k = (kf * cos2 + pltpu.roll(kf, DK_D // 2, 1) * sin2) * DK_D ** -0.5
        v = d_ref[:, 2 * W_D + h * DV_D:2 * W_D + (h + 1) * DV_D].astype(mdt)
        qdec = dec_ref[h, :, 0:1]
        kdec = dec_ref[h, :, 1:2]
        sdec = dec_ref[h, 0:1, 2:3]
        S = s_scr[h]
        qb = q.astype(mdt)
        att = _dot_nt(qb, k.astype(mdt)) * dm_ref[h]
        o = _dot(att.astype(mdt), v) + qdec * _dot(qb, S.astype(mdt))
        s_scr[h] = sdec * S + _dot_tn((k * kdec).astype(mdt), v)
        gt = d_ref[:, 3 * W_D + h * DV_D:3 * W_D + (h + 1) * DV_D]
        y = _head_norm(o) * g_ref[:, h * DV_D:(h + 1) * DV_D] * (gt * jax.nn.sigmoid(gt))
        y_ref[:, h * DV_D:(h + 1) * DV_D] = y.astype(y_ref.dtype)

    @pl.when(c == pl.num_programs(1) - 1)
    def _():
        s1_ref[...] = s_scr[...]


def _retention(d, cos2, sin2, dmat, dec, gnorm, s0, nb, nc, odt):
    n = d.shape[0]
    L = CHUNK
    return pl.pallas_call(
        _retention_kernel,
        grid=(nb, nc),
        in_specs=[pl.BlockSpec((L, 4 * W_D), lambda b, c: (b * nc + c, 0)),
                  pl.BlockSpec((L, DK_D), lambda b, c: (c, 0)),
                  pl.BlockSpec((L, DK_D), lambda b, c: (c, 0)),
                  pl.BlockSpec((H_D, L, L), lambda b, c: (0, 0, 0)),
                  pl.BlockSpec((H_D, L, LANES), lambda b, c: (0, 0, 0)),
                  pl.BlockSpec((1, W_D), lambda b, c: (0, 0)),
                  pl.BlockSpec((None, H_D, DK_D, DV_D), lambda b, c: (b, 0, 0, 0))],
        out_specs=[pl.BlockSpec((L, W_D), lambda b, c: (b * nc + c, 0)),
                   pl.BlockSpec((None, H_D, DK_D, DV_D), lambda b, c: (b, 0, 0, 0))],
        out_shape=[jax.ShapeDtypeStruct((n, W_D), odt),
                   jax.ShapeDtypeStruct((nb, H_D, DK_D, DV_D), F32)],
        scratch_shapes=[pltpu.VMEM((H_D, DK_D, DV_D), F32)],
        compiler_params=_cparams("parallel", "arbitrary"),
        name="retention",
    )(d, cos2, sin2, dmat, dec, gnorm, s0)


def _shift_rows(x, prev, j):
    tb = x.shape[0]
    xs = pltpu.roll(x, j, 0)
    pr = pltpu.roll(prev, j, 0)
    row = lax.broadcasted_iota(I32, pr.shape, 0)
    first = jnp.where(row < j, pr, xs[0:SUBLANES])
    if tb == SUBLANES:
        return first
    return jnp.concatenate([first, xs[SUBLANES:]], axis=0)


def _rglru_kernel(c_ref, cw_ref, cb_ref, wa_ref, ba_ref, wx_ref, bx_ref, lam_ref, conv0_ref, h0_ref,
                  y_ref, hl_ref, prev_scr, h_scr):
    t = pl.program_id(1)
    tb = c_ref.shape[0]

    @pl.when(t == 0)
    def _():
        prev_scr[...] = conv0_ref[...]
        h_scr[...] = h0_ref[...]

    x = c_ref[:, 0:W_C]
    gate = c_ref[:, W_C:2 * W_C]
    prev = prev_scr[...]
    xc = x * cw_ref[CONV_W - 1:CONV_W, :] + cb_ref[...]
    for j in range(1, CONV_W):
        xc = xc + _shift_rows(x, prev, j) * cw_ref[CONV_W - 1 - j:CONV_W - j, :]
    prev_scr[...] = x[tb - SUBLANES:tb]

    xb = xc.astype(wa_ref.dtype)
    r = jax.nn.sigmoid(_dot(xb, wa_ref[...]) + ba_ref[...])
    i = jax.nn.sigmoid(_dot(xb, wx_ref[...]) + bx_ref[...])
    log_a = -LRU_C * r * jax.nn.softplus(-lam_ref[...])
    A = jnp.exp(log_a)
    th = jnp.tanh(log_a)
    U = jnp.sqrt(-2.0 * th / (1.0 - th)) * (i * xc)
    row = lax.broadcasted_iota(I32, (tb, W_C), 0)
    d = 1
    while d < tb:
        keep = row >= d
        U = jnp.where(keep, U + A * pltpu.roll(U, d, 0), U)
        A = jnp.where(keep, A * pltpu.roll(A, d, 0), A)
        d *= 2
    h = U + A * h_scr[0:1, :]
    h_scr[...] = jnp.broadcast_to(h[tb - 1:tb, :], (SUBLANES, W_C))
    y_ref[...] = (h * jax.nn.gelu(gate)).astype(y_ref.dtype)

    @pl.when(t == pl.num_programs(1) - 1)
    def _():
        hl_ref[...] = h_scr[...]


def _rglru(c, cw, cb, wa, ba, wx, bx, lam, conv0, h0, nb, nt, tb, odt):
    n = c.shape[0]
    vec = pl.BlockSpec((1, W_C), lambda b, t: (0, 0))
    mat = pl.BlockSpec((W_C, W_C), lambda b, t: (0, 0))
    st = pl.BlockSpec((None, SUBLANES, W_C), lambda b, t: (b, 0, 0))
    return pl.pallas_call(
        _rglru_kernel,
        grid=(nb, nt),
        in_specs=[pl.BlockSpec((tb, 2 * W_C), lambda b, t: (b * nt + t, 0)),
                  pl.BlockSpec((CONV_W, W_C), lambda b, t: (0, 0)), vec, mat, vec, mat, vec, vec, st, st],
        out_specs=[pl.BlockSpec((tb, W_C), lambda b, t: (b * nt + t, 0)), st],
        out_shape=[jax.ShapeDtypeStruct((n, W_C), odt), jax.ShapeDtypeStruct((nb, SUBLANES, W_C), F32)],
        scratch_shapes=[pltpu.VMEM((SUBLANES, W_C), F32), pltpu.VMEM((SUBLANES, W_C), F32)],
        compiler_params=_cparams("parallel", "arbitrary"),
        name="rglru",
    )(c, cw, cb, wa, ba, wx, bx, lam, conv0, h0)


def _sort_key(score):
    score = jnp.where(score == 0.0, 0.0, score)
    bits = lax.bitcast_convert_type(score, I32)
    return bits ^ ((bits >> 31) & 0x7FFFFFFF)


def _kth_largest_key(count_ge, k, rows):
    kf = float(k)
    zero = jnp.zeros((rows, 1), I32)
    t0 = jnp.where(count_ge(zero) >= kf, zero, jnp.full((rows, 1), INT_MIN, I32))

    def body(bi, t):
        cand = t + jnp.left_shift(jnp.int32(1), 30 - bi)
        return jnp.where(count_ge(cand) >= kf, cand, t)

    return lax.fori_loop(0, 31, body, t0)


def _first_positions(count_eq_before, need, nbits, rows):
    def body(bi, x):
        cand = x + jnp.left_shift(jnp.int32(1), nbits - 1 - bi)
        return jnp.where(count_eq_before(cand) < need, cand, x)

    return lax.fori_loop(0, nbits, body, jnp.zeros((rows, 1), I32))


def _qi_heads(qi):
    lane = lax.broadcasted_iota(I32, (qi.shape[0], LANES), 1)
    low = lane < D_IDX
    out = []
    for h in range(H_IDX):
        pair = qi[:, (h // 2) * LANES:(h // 2 + 1) * LANES]
        if h % 2:
            pair = pltpu.roll(pair, D_IDX, 1)
        out.append(jnp.where(low, pair, 0.0).astype(BF16))
    return out


def _dsa_prompt_kernel(q_ref, s_ref, ki_ref, kb_ref, vb_ref, o_ref, key_scr, bias_scr, x_scr, lg_scr,
                       *, q_off, kt_size, topk):
    i = pl.program_id(1)
    QB = q_ref.shape[0]
    KT = kt_size
    nkt = key_scr.shape[0]
    lk = nkt * KT
    q0 = q_off + i * QB
    qih = _qi_heads(q_ref[:, W_B:2 * W_B])
    wi = s_ref[:, S_WI:S_WI + H_IDX]
    wcols = [wi[:, h:h + 1] for h in range(H_IDX)]
    rowpos = q0 + lax.broadcasted_iota(I32, (QB, KT), 0)
    colpos = lax.broadcasted_iota(I32, (QB, KT), 1)

    def score_tile(kt, carry):
        k0 = pl.multiple_of(kt * KT, KT)
        ki = ki_ref[pl.ds(k0, KT), :].astype(BF16)
        sc = jnp.zeros((QB, KT), F32)
        for h in range(H_IDX):
            sc = sc + wcols[h] * jnp.maximum(_dot_nt(qih[h], ki), 0.0)
        sc = jnp.where(colpos + k0 <= rowpos, sc, -jnp.inf)
        key_scr[kt] = _sort_key(sc)
        return carry

    lax.fori_loop(0, nkt, score_tile, 0, unroll=2)

    lanepos = lax.broadcasted_iota(I32, (QB, LANES), 1)

    def count(pred):
        def body(kt, acc):
            tile = key_scr[kt]
            k0 = kt * KT
            for j in range(KT // LANES):
                sl = slice(j * LANES, (j + 1) * LANES)
                acc = acc + jnp.where(pred(tile[:, sl], lanepos + (k0 + j * LANES)), 1.0, 0.0)
            return acc
        acc = lax.fori_loop(0, nkt, body, jnp.zeros((QB, LANES), F32))
        return jnp.sum(acc, axis=1, keepdims=True)

    thr = _kth_largest_key(lambda t: count(lambda key, pos: key >= t), topk, QB)
    cnt_ge = count(lambda key, pos: key >= thr)
    cnt_gt = count(lambda key, pos: key > thr)
    x_scr[...] = jnp.full(x_scr.shape, lk, I32)

    @pl.when(jnp.max(cnt_ge) > float(topk))
    def _():
        need = float(topk) - cnt_gt
        nbits = max(1, (lk - 1).bit_length())
        x = _first_positions(
            lambda c: count(lambda key, pos: jnp.logical_and(key == thr, pos < c)), need, nbits, QB)
        x_scr[...] = jnp.broadcast_to(x, x_scr.shape)

    xlim = x_scr[:, 0:1]

    def bias_tile(kt, carry):
        key = key_scr[kt]
        pos = colpos + kt * KT
        sel = jnp.logical_or(key > thr, jnp.logical_and(key == thr, pos <= xlim))
        bias_scr[kt] = jnp.where(jnp.logical_and(sel, pos <= rowpos), 0.0, NEG)
        return carry

    lax.fori_loop(0, nkt, bias_tile, 0)

    low = lanepos < DH_B
    ngrp = KT // LANES

    def group_fold(op, acc, x):
        for g in range(ngrp):
            acc = op(acc, x[:, g * LANES:(g + 1) * LANES])
        return acc

    for j in range(H_B // 2):
        qpair = q_ref[:, j * LANES:(j + 1) * LANES] * DH_B ** -0.5
        q0m = jnp.where(low, qpair, 0.0).astype(BF16)
        q1m = jnp.where(low, 0.0, qpair).astype(BF16)

        def pass1(kt, carry, q0m=q0m, q1m=q1m, j=j):
            mx0, mx1 = carry
            k0 = pl.multiple_of(kt * KT, KT)
            kk = kb_ref[pl.ds(k0, KT), j * LANES:(j + 1) * LANES]
            bias = bias_scr[kt]
            lg0 = _dot_nt(q0m, kk) + bias
            lg1 = _dot_nt(q1m, kk) + bias
            lg_scr[0, kt] = lg0
            lg_scr[1, kt] = lg1
            return group_fold(jnp.maximum, mx0, lg0), group_fold(jnp.maximum, mx1, lg1)

        neg = jnp.full((QB, LANES), NEG, F32)
        mx0, mx1 = lax.fori_loop(0, nkt, pass1, (neg, neg), unroll=2)
        m0 = jnp.max(mx0, axis=1, keepdims=True)
        m1 = jnp.max(mx1, axis=1, keepdims=True)

        def pass2(kt, carry, m0=m0, m1=m1, j=j):
            l0, l1, a0, a1 = carry
            k0 = pl.multiple_of(kt * KT, KT)
            vv = vb_ref[pl.ds(k0, KT), j * LANES:(j + 1) * LANES]
            p0 = jnp.exp(lg_scr[0, kt] - m0)
            p1 = jnp.exp(lg_scr[1, kt] - m1)
            a0 = a0 + _dot(p0.astype(BF16), vv)
            a1 = a1 + _dot(p1.astype(BF16), vv)
            return group_fold(jnp.add, l0, p0), group_fold(jnp.add, l1, p1), a0, a1

        zero = jnp.zeros((QB, LANES), F32)
        l0, l1, a0, a1 = lax.fori_loop(0, nkt, pass2, (zero, zero, zero, zero), unroll=2)
        out0 = a0 / jnp.sum(l0, axis=1, keepdims=True)
        out1 = a1 / jnp.sum(l1, axis=1, keepdims=True)
        o_ref[:, j * LANES:(j + 1) * LANES] = jnp.where(low, out0, out1).astype(o_ref.dtype)


def _dsa_prompt(bq, s, kb, vb, nb, t):
    n = bq.shape[0]
    QB = math.gcd(t, CHUNK)
    topk = min(TOPK_MAX, t // 4)
    nseg = 1
    for cand in (8, 4, 2):
        if t % cand == 0 and (t // cand) % 512 == 0:
            nseg = cand
            break
    seg = t // nseg
    KT = 512 if seg % 512 == 0 else seg
    s3 = s.reshape(nb, t, LANES)
    kb3 = kb.reshape(nb, t, W_B)
    vb3 = vb.reshape(nb, t, W_B)
    nqb = seg // QB
    outs = []
    for g in range(nseg):
        lk = (g + 1) * seg
        nkt = lk // KT
        row_blk = functools.partial(lambda b, i, g: (b * (t // QB) + g * nqb + i, 0), g=g)
        outs.append(pl.pallas_call(
            functools.partial(_dsa_prompt_kernel, q_off=g * seg, kt_size=KT, topk=topk),
            grid=(nb, nqb),
            in_specs=[pl.BlockSpec((QB, 2 * W_B), row_blk),
                      pl.BlockSpec((QB, LANES), row_blk),
                      pl.BlockSpec((None, lk, LANES), lambda b, i: (b, 0, 0)),
                      pl.BlockSpec((None, lk, W_B), lambda b, i: (b, 0, 0)),
                      pl.BlockSpec((None, lk, W_B), lambda b, i: (b, 0, 0))],
            out_specs=pl.BlockSpec((None, QB, W_B), lambda b, i: (b, i, 0)),
            out_shape=jax.ShapeDtypeStruct((nb, seg, W_B), BF16),
            scratch_shapes=[pltpu.VMEM((nkt, QB, KT), I32), pltpu.VMEM((nkt, QB, KT), F32),
                            pltpu.VMEM((QB, LANES), I32), pltpu.VMEM((2, nkt, QB, KT), F32)],
            compiler_params=_cparams("parallel", "arbitrary"),
            name=f"dsa_prompt_{g}",
        )(bq, s, s3, kb3, vb3))
    return jnp.concatenate(outs, axis=1).reshape(n, W_B)


SCORE_PAGES = 16
ATTN_PAGES = 8


def _dsa_sample_score_kernel(pt_ref, q_ref, s_ref, snew_ref, *rest):
    page_refs, o_ref = rest[:-1], rest[-1]
    p = pl.program_id(1)
    last = pl.num_programs(1) - 1
    T = q_ref.shape[0]
    qi = q_ref[:, W_B:2 * W_B]
    wi = s_ref[:, S_WI:S_WI + H_IDX]
    qst = jnp.concatenate([qi[:, h * D_IDX:(h + 1) * D_IDX] for h in range(H_IDX)], axis=0)
    wcol = jnp.concatenate([wi[:, h:h + 1] for h in range(H_IDX)], axis=0)

    def scores(dots):
        r = wcol * jnp.maximum(dots, 0.0)
        sc = r[0:T]
        for h in range(1, H_IDX):
            sc = sc + r[h * T:(h + 1) * T]
        return sc

    @pl.when(p < last)
    def _():
        for j, page_ref in enumerate(page_refs):
            o_ref[j] = scores(_dot(qst, page_ref[...]))

    @pl.when(p == last)
    def _():
        sc = scores(_dot_nt(qst, snew_ref[:, S_KI:S_KI + D_IDX]))
        r = lax.broadcasted_iota(I32, sc.shape, 0)
        c = lax.broadcasted_iota(I32, sc.shape, 1)
        o_ref[0] = jnp.where(c <= r, sc, -jnp.inf)
        for j in range(1, len(page_refs)):
            o_ref[j] = jnp.full(sc.shape, -jnp.inf, F32)


def _dsa_sample_attn_kernel(pt_ref, q_ref, sc_ref, knew_ref, vnew_ref, *rest, topk, npg, G):
    kpages, vpages, o_ref = rest[:G], rest[G:2 * G], rest[2 * G]
    thr_scr, x_scr, m_scr, l_scr, acc_scr = rest[2 * G + 1:]
    p = pl.program_id(1)
    last = pl.num_programs(1) - 1
    T = q_ref.shape[0]
    R = H_B * T
    ltot = sc_ref.shape[0] * PAGE_SIZE

    @pl.when(p == 0)
    def _():
        key = _sort_key(sc_ref[...])
        pos = (lax.broadcasted_iota(I32, key.shape, 0) * PAGE_SIZE
               + lax.broadcasted_iota(I32, key.shape, 2))

        def count(pred):
            per_lane = jnp.sum(jnp.where(pred(key, pos), 1.0, 0.0), axis=0)
            return jnp.sum(per_lane, axis=1, keepdims=True)

        thr = _kth_largest_key(lambda t: count(lambda k_, p_: k_ >= t), topk, T)
        cnt_gt = count(lambda k_, p_: k_ > thr)
        nbits = max(1, (ltot - 1).bit_length())
        x = _first_positions(lambda c: count(lambda k_, p_: jnp.logical_and(k_ == thr, p_ < c)),
                             float(topk) - cnt_gt, nbits, T)
        thr_scr[...] = jnp.broadcast_to(thr, thr_scr.shape)
        x_scr[...] = jnp.broadcast_to(x, x_scr.shape)
        m_scr[...] = jnp.full(m_scr.shape, NEG, F32)
        l_scr[...] = jnp.zeros_like(l_scr)
        acc_scr[...] = jnp.zeros_like(acc_scr)

    q = q_ref[:, 0:W_B] * DH_B ** -0.5
    qrep = jnp.concatenate([q] * H_B, axis=0)
    rr = lax.broadcasted_iota(I32, (R, W_B), 0) // T
    cc = lax.broadcasted_iota(I32, (R, W_B), 1) // DH_B
    diag = rr == cc
    qbd = jnp.where(diag, qrep, 0.0)
    thr = thr_scr[:, 0:1]
    xlim = x_scr[:, 0:1]

    def selected(page):
        key = _sort_key(sc_ref[page])
        pos = page * PAGE_SIZE + lax.broadcasted_iota(I32, key.shape, 1)
        return jnp.logical_or(key > thr, jnp.logical_and(key == thr, pos <= xlim))

    def masked(logits, valid):
        bias = jnp.where(valid, 0.0, NEG)
        return logits + jnp.concatenate([bias] * H_B, axis=0)

    def update(lgs, pv):
        mx = lgs[0]
        for lg in lgs[1:]:
            mx = jnp.maximum(mx, lg)
        m = m_scr[:, 0:1]
        m_new = jnp.maximum(m, jnp.max(mx, axis=1, keepdims=True))
        alpha = jnp.exp(m - m_new)
        ps = [jnp.exp(lg - m_new) for lg in lgs]
        psum = ps[0]
        for pr in ps[1:]:
            psum = psum + pr
        acc = pv(0, ps[0])
        for j in range(1, len(ps)):
            acc = acc + pv(j, ps[j])
        l_scr[...] = jnp.broadcast_to(alpha * l_scr[:, 0:1] + jnp.sum(psum, axis=1, keepdims=True), l_scr.shape)
        acc_scr[...] = alpha * acc_scr[...] + acc
        m_scr[...] = jnp.broadcast_to(m_new, m_scr.shape)

    @pl.when(p < last)
    def _():
        lgs = [masked(_dot(qbd, kpages[j][...]), selected(p * G + j)) for j in range(G)]
        update(lgs, lambda j, pr: _dot_nt(pr, vpages[j][...]))

    @pl.when(p == last)
    def _():
        sel = selected(npg)
        r = lax.broadcasted_iota(I32, sel.shape, 0)
        c = lax.broadcasted_iota(I32, sel.shape, 1)
        lg = masked(_dot_nt(qbd, knew_ref[...]), jnp.logical_and(sel, c <= r))
        update([lg], lambda j, pr: _dot(pr, vnew_ref[...]))
        out = jnp.where(diag, acc_scr[...] / l_scr[:, 0:1], 0.0)
        res = out[0:T]
        for h in range(1, H_B):
            res = res + out[h * T:(h + 1) * T]
        o_ref[...] = res.astype(o_ref.dtype)


def _dsa_sample(bq, s, k, v, pool_k, pool_v, pool_ki, page_table, nb, t):
    n_pool = pool_k.shape[1]
    pool_k = jnp.transpose(pool_k, (0, 1, 3, 4, 2)).reshape(1, n_pool, W_B, PAGE_SIZE)
    pool_v = jnp.transpose(pool_v, (0, 1, 3, 4, 2)).reshape(1, n_pool, W_B, PAGE_SIZE)
    pool_ki = jnp.transpose(pool_ki, (0, 1, 3, 2))
    npg = page_table.shape[1]
    past = npg * PAGE_SIZE
    topk = min(TOPK_MAX, (past + t) // 4)
    gs, ga = math.gcd(npg, SCORE_PAGES), math.gcd(npg, ATTN_PAGES)
    pad = PAGE_SIZE - t
    s_new = jnp.pad(s.reshape(nb, t, LANES), ((0, 0), (0, pad), (0, 0)))
    k_new = jnp.pad(k.reshape(nb, t, W_B), ((0, 0), (0, pad), (0, 0)))
    v_new = jnp.pad(v.reshape(nb, t, W_B), ((0, 0), (0, pad), (0, 0)))
    lastp = npg - 1
    ptot = npg + gs

    def page_spec(tail, j, group):
        zeros = (0,) * len(tail)
        return pl.BlockSpec((None, None) + tail,
                            lambda b, p, pt: (0, pt[b, jnp.minimum(p * group + j, lastp)]) + zeros)

    scores = pl.pallas_call(
        _dsa_sample_score_kernel,
        grid_spec=pltpu.PrefetchScalarGridSpec(
            num_scalar_prefetch=1,
            grid=(nb, npg // gs + 1),
            in_specs=[pl.BlockSpec((t, 2 * W_B), lambda b, p, pt: (b, 0)),
                      pl.BlockSpec((t, LANES), lambda b, p, pt: (b, 0)),
                      pl.BlockSpec((None, PAGE_SIZE, LANES), lambda b, p, pt: (b, 0, 0))]
            + [page_spec((D_IDX, PAGE_SIZE), j, gs) for j in range(gs)],
            out_specs=pl.BlockSpec((None, gs, t, PAGE_SIZE), lambda b, p, pt: (b, p, 0, 0))),
        out_shape=jax.ShapeDtypeStruct((nb, ptot, t, PAGE_SIZE), F32),
        compiler_params=_cparams("parallel", "arbitrary"),
        name="dsa_sample_scores",
    )(page_table, bq, s, s_new, *([pool_ki] * gs))

    kv_tail = (W_B, PAGE_SIZE)
    return pl.pallas_call(
        functools.partial(_dsa_sample_attn_kernel, topk=topk, npg=npg, G=ga),
        grid_spec=pltpu.PrefetchScalarGridSpec(
            num_scalar_prefetch=1,
            grid=(nb, npg // ga + 1),
            in_specs=[pl.BlockSpec((t, 2 * W_B), lambda b, p, pt: (b, 0)),
                      pl.BlockSpec((None, ptot, t, PAGE_SIZE), lambda b, p, pt: (b, 0, 0, 0)),
                      pl.BlockSpec((None, PAGE_SIZE, W_B), lambda b, p, pt: (b, 0, 0)),
                      pl.BlockSpec((None, PAGE_SIZE, W_B), lambda b, p, pt: (b, 0, 0))]
            + [page_spec(kv_tail, j, ga) for j in range(ga)]
            + [page_spec(kv_tail, j, ga) for j in range(ga)],
            out_specs=pl.BlockSpec((t, W_B), lambda b, p, pt: (b, 0)),
            scratch_shapes=[pltpu.VMEM((t, LANES), I32), pltpu.VMEM((t, LANES), I32),
                            pltpu.VMEM((H_B * t, LANES), F32), pltpu.VMEM((H_B * t, LANES), F32),
                            pltpu.VMEM((H_B * t, W_B), F32)]),
        out_shape=jax.ShapeDtypeStruct((nb * t, W_B), F32),
        compiler_params=_cparams("parallel", "arbitrary"),
        name="dsa_sample_attn",
    )(page_table, bq, scores, k_new, v_new, *([pool_k] * ga), *([pool_v] * ga))


def _pad_chunks(a, nb, t, fill=None):
    c = a.shape[1]
    a3 = a.reshape(nb, t, c)
    if fill is None:
        a3 = jnp.pad(a3, ((0, 0), (0, CHUNK - t), (0, 0)))
    else:
        a3 = jnp.concatenate([a3, jnp.broadcast_to(fill, (nb, CHUNK - t, c))], axis=1)
    return a3.reshape(nb * CHUNK, c)


def _retention_tables(t_true, pos0, t_pad):
    L = CHUNK
    lt = min(L, t_true)
    lg = jnp.log1p(-jnp.exp2(-5.0 - jnp.arange(H_D, dtype=F32)))
    j = jnp.arange(L, dtype=F32)
    causal = jnp.tril(jnp.ones((L, L), dtype=bool))
    dmat = jnp.exp(jnp.where(causal, (j[:, None] - j[None, :]) * lg[:, None, None], -jnp.inf))
    qdec = jnp.exp((j + 1.0) * lg[:, None])
    kdec = jnp.where(j < lt, jnp.exp((lt - 1.0 - j) * lg[:, None]), 0.0)
    sdec = jnp.broadcast_to(jnp.exp(lt * lg)[:, None], (H_D, L))
    dec = jnp.zeros((H_D, L, LANES), F32)
    dec = dec.at[:, :, 0].set(qdec).at[:, :, 1].set(kdec).at[:, :, 2].set(sdec)
    half = DK_D // 2
    freq = ROPE_BASE ** (-jnp.arange(half, dtype=F32) / half)
    pos = (pos0 + jnp.arange(t_pad)).astype(F32)
    ang = pos[:, None] * freq[None, :]
    cos, sin = jnp.cos(ang), jnp.sin(ang)
    return dmat, dec, jnp.concatenate([cos, cos], -1), jnp.concatenate([-sin, sin], -1)


def _forward(x3, st, prm):
    nb, t, d = x3.shape
    n = nb * t
    x = x3.reshape(n, d)
    short = t < CHUNK
    t_pad = CHUNK if short else t
    nc = t_pad // CHUNK
    new = {}
    odt = prm["w_in_e"].dtype

    a, bq, k, v, s, kb, vb = _inproj_e(x, prm["w_in_e"], prm["bias_e"])
    if st is None:
        c0 = jnp.zeros((nb, H_A, DK_A, DV_A), F32)
        n0 = jnp.zeros((nb, H_A, DK_A), F32)
        m0 = jnp.zeros((nb, H_A), F32)
    else:
        c0, n0, m0 = st["mlstm_C"][0], st["mlstm_n"][0], st["mlstm_m"][0]
    m0b = jnp.broadcast_to(m0[:, :, None], (nb, H_A, LANES))
    if short:
        lane = jnp.arange(LANES)
        fill = jnp.where((lane >= S_IG) & (lane < S_IG + H_A), NEG,
                         jnp.where((lane >= S_FG) & (lane < S_FG + H_A), 1e4, 0.0)).astype(F32)
        a_m, s_m = _pad_chunks(a, nb, t), _pad_chunks(s, nb, t, fill)
    else:
        a_m, s_m = a, s
    ya, c1, n1, m1 = _mlstm(a_m, s_m, prm["a_norm_g"], c0, n0, m0b, nb, nc, odt)
    if short:
        ya = ya.reshape(nb, CHUNK, W_A)[:, :t].reshape(n, W_A)
    if st is None:
        yb = _dsa_prompt(bq, s, kb, vb, nb, t)
    else:
        yb = _dsa_sample(bq, s, k, v, st["pool_k"], st["pool_v"], st["pool_ki"], st["page_table"], nb, t)
    new["mlstm_C"], new["mlstm_n"], new["mlstm_m"] = c1[None], n1[None], m1[None, :, :, 0]
    new["k"] = k.reshape(1, nb, t, H_B, DH_B)
    new["v"] = v.reshape(1, nb, t, H_B, DH_B)
    new["kidx"] = s[:, S_KI:S_KI + D_IDX].reshape(1, nb, t, D_IDX)
    x = _outproj_ln(ya, yb, prm["w_out_e"], x, prm["ln_g"][0, 0], prm["ln_b"][0, 0])
    x = _moe_ln(x, prm["router_w"], prm["router_b"], prm["moe_wg"][0], prm["moe_wu"][0], prm["moe_wd"][0],
                prm["ln_g"][0, 1], prm["ln_b"][0, 1])

    c, dd = _inproj_o(x, prm["w_in_o"])
    if st is None:
        conv0 = jnp.zeros((nb, SUBLANES, W_C), F32)
        h0 = jnp.zeros((nb, SUBLANES, W_C), F32)
        s0 = jnp.zeros((nb, H_D, DK_D, DV_D), F32)
        pos0 = 0
    else:
        conv0 = jnp.pad(st["conv"][0], ((0, 0), (SUBLANES - (CONV_W - 1), 0), (0, 0)))
        h0 = jnp.broadcast_to(st["lru_h"][0][:, None, :], (nb, SUBLANES, W_C))
        s0 = st["ret_S"][0]
        pos0 = st["page_table"].shape[1] * PAGE_SIZE
    tb = _row_block(t, 256)
    yc, hl = _rglru(c, prm["c_conv_w"], prm["c_conv_b"], prm["c_wa"], prm["c_ba"], prm["c_wx"], prm["c_bx"],
                    prm["c_lambda"], conv0, h0, nb, t // tb, tb, odt)
    dmat, dec, cos2, sin2 = _retention_tables(t, pos0, t_pad)
    d_m = _pad_chunks(dd, nb, t) if short else dd
    yd, s1 = _retention(d_m, cos2, sin2, dmat, dec, prm["d_norm_g"], s0, nb, nc, odt)
    if short:
        yd = yd.reshape(nb, CHUNK, W_D)[:, :t].reshape(n, W_D)
    new["conv"] = c[:, 0:W_C].reshape(nb, t, W_C)[None, :, t - (CONV_W - 1):]
    new["lru_h"] = hl[None, :, 0]
    new["ret_S"] = s1[None]
    x = _outproj_ln(yc, yd, prm["w_out_o"], x, prm["ln_g"][1, 0], prm["ln_b"][1, 0])
    x = _moe_ln(x, prm["router_w"], prm["router_b"], prm["moe_wg"][1], prm["moe_wu"][1], prm["moe_wd"][1],
                prm["ln_g"][1, 1], prm["ln_b"][1, 1])
    return x.reshape(nb, t, d), new


def _block_diag(w):
    nblk, blk, _ = w.shape
    eye = jnp.eye(nblk, dtype=w.dtype)
    return (eye[:, None, :, None] * w[:, :, None, :]).reshape(nblk * blk, nblk * blk)


def _prepare_params(w_in_e, b_if_e, a_norm_g, w_out_e, w_in_o, c_conv_w, c_conv_b, c_wa, c_ba, c_wx, c_bx,
                    c_lambda, d_norm_g, w_out_o, router_w, router_bias, moe_w_gate, moe_w_up, moe_w_down,
                    ln_g, ln_b, wdt):
    d = w_in_e.shape[1]
    o = [0]
    for width in (W_A, W_A, W_A, W_A, 2 * H_A, W_B, W_B, W_B, H_IDX * D_IDX, D_IDX, H_IDX):
        o.append(o[-1] + width)
    we = w_in_e[0]
    col = lambda i: we[:, o[i]:o[i + 1]]
    slab = jnp.zeros((d, LANES), F32)
    slab = slab.at[:, S_KI:S_KI + D_IDX].set(col(9)).at[:, S_IG:S_IG + 2 * H_A].set(col(4))
    slab = slab.at[:, S_WI:S_WI + H_IDX].set(col(10))
    w_e = jnp.concatenate([col(0), col(1), col(2), col(3), col(5), col(8), col(6), col(7), slab], axis=1)
    bias_e = jnp.zeros((1, LANES), F32).at[0, S_IG:S_IG + 2 * H_A].set(b_if_e[0])
    rw = jnp.zeros((d, LANES), F32).at[:, :N_EXPERTS].set(router_w)
    rb = jnp.zeros((1, LANES), F32).at[0, :N_EXPERTS].set(router_bias)
    row = lambda a: a.reshape(1, -1).astype(F32)
    return {
        "w_in_e": w_e.astype(wdt), "bias_e": bias_e, "a_norm_g": row(a_norm_g[0]),
        "w_out_e": w_out_e[0].astype(wdt), "w_in_o": w_in_o[0].astype(wdt),
        "c_conv_w": c_conv_w[0].astype(F32), "c_conv_b": row(c_conv_b[0]),
        "c_wa": _block_diag(c_wa[0]).astype(wdt), "c_ba": row(c_ba[0]),
        "c_wx": _block_diag(c_wx[0]).astype(wdt), "c_bx": row(c_bx[0]),
        "c_lambda": row(c_lambda[0]), "d_norm_g": row(d_norm_g[0]), "w_out_o": w_out_o[0].astype(wdt),
        "router_w": rw, "router_b": rb,
        "moe_wg": moe_w_gate.astype(wdt), "moe_wu": moe_w_up.astype(wdt), "moe_wd": moe_w_down.astype(wdt),
        "ln_g": ln_g.reshape(DEPTH, 2, 1, -1).astype(F32), "ln_b": ln_b.reshape(DEPTH, 2, 1, -1).astype(F32),
    }


def kernel(x_prompt, x_sample, state_mlstm_C, state_mlstm_n, state_mlstm_m, cache_k, cache_v, cache_kidx,
           page_table, state_conv, state_lru_h, state_ret_S, w_in_e, b_if_e, a_norm_g, w_out_e, w_in_o,
           c_conv_w, c_conv_b, c_wa, c_ba, c_wx, c_bx, c_lambda, d_norm_g, w_out_o, router_w, router_bias,
           moe_w_gate, moe_w_up, moe_w_down, ln_g, ln_b):
    weights = (w_in_e, b_if_e, a_norm_g, w_out_e, w_in_o, c_conv_w, c_conv_b, c_wa, c_ba, c_wx, c_bx, c_lambda,
               d_norm_g, w_out_o, router_w, router_bias, moe_w_gate, moe_w_up, moe_w_down, ln_g, ln_b)
    prm_prompt = _prepare_params(*weights, BF16)
    prm_sample = _prepare_params(*weights, F32)
    st = {"mlstm_C": state_mlstm_C, "mlstm_n": state_mlstm_n, "mlstm_m": state_mlstm_m,
          "pool_k": cache_k, "pool_v": cache_v, "pool_ki": cache_kidx, "page_table": page_table,
          "conv": state_conv, "lru_h": state_lru_h, "ret_S": state_ret_S}
    y_p, nsp = _forward(x_prompt, None, prm_prompt)
    y_s, nss = _forward(x_sample, st, prm_sample)
    names = ("mlstm_C", "mlstm_n", "mlstm_m", "k", "v", "kidx", "conv", "lru_h", "ret_S")
    return (y_p, y_s) + tuple(nsp[k] for k in names) + tuple(nss[k] for k in names)
```

```python
import functools
import math

import jax
import jax.numpy as jnp
from jax import lax
from jax.experimental import pallas as pl
from jax.experimental.pallas import tpu as pltpu

F32 = jnp.float32
BF16 = jnp.bfloat16
I32 = jnp.int32

DEPTH = 2
PAGE_SIZE = 128
H_A, DK_A, DV_A = 4, 128, 128
W_A = H_A * DV_A
H_B, DH_B = 8, 64
W_B = H_B * DH_B
H_IDX, D_IDX = 8, 64
TOPK_MAX = 256
W_C, N_BLK_C, CONV_W, LRU_C = 512, 8, 4, 8.0
BLK_C = W_C // N_BLK_C
H_D, DK_D, DV_D = 4, 128, 128
W_D = H_D * DV_D
ROPE_BASE = 10000.0
N_EXPERTS, N_GROUPS, TOP_K_EXP, D_FF_EXP = 16, 4, 2, 512
EXP_PER_GROUP = N_EXPERTS // N_GROUPS
ALPHA = (2 * DEPTH) ** 0.25
LN_EPS = 1e-5
HN_EPS = 1e-6

CHUNK = 128
LANES = 128
SUBLANES = 8
NEG = -1e30
INT_MIN = -2 ** 31
VMEM_LIMIT = 56 * 1024 * 1024

S_KI = 0
S_IG = 64
S_FG = 68
S_WI = 72


def _cparams(*sem):
    return pltpu.CompilerParams(dimension_semantics=sem, vmem_limit_bytes=VMEM_LIMIT)


def _precision(a, b):
    return lax.Precision.HIGHEST if a.dtype == F32 and b.dtype == F32 else None


def _dot(a, b):
    return jnp.dot(a, b, preferred_element_type=F32, precision=_precision(a, b))


def _dot_nt(a, b):
    return lax.dot_general(a, b, (((1,), (1,)), ((), ())), preferred_element_type=F32, precision=_precision(a, b))


def _dot_tn(a, b):
    return lax.dot_general(a, b, (((0,), (0,)), ((), ())), preferred_element_type=F32, precision=_precision(a, b))


def _row_block(n, target):
    t = min(n, target)
    while n % t:
        t //= 2
    return t


def _layer_norm(z, g, b):
    mu = jnp.mean(z, -1, keepdims=True)
    zc = z - mu
    var = jnp.mean(zc * zc, -1, keepdims=True)
    return zc * lax.rsqrt(var + LN_EPS) * g + b


def _head_norm(h):
    mu = jnp.mean(h, -1, keepdims=True)
    hc = h - mu
    var = jnp.mean(hc * hc, -1, keepdims=True)
    return hc * lax.rsqrt(var + HN_EPS)


def _inproj_e_kernel(x_ref, w_ref, bias_ref, a_ref, bq_ref, k_ref, v_ref, s_ref, kb_ref, vb_ref):
    x = x_ref[...].astype(w_ref.dtype)

    def mm(lo, hi):
        return _dot(x, w_ref[:, lo:hi])

    a_ref[:, 0:W_A] = mm(0, W_A)
    a_ref[:, W_A:2 * W_A] = mm(W_A, 2 * W_A) * DK_A ** -0.5
    a_ref[:, 2 * W_A:4 * W_A] = mm(2 * W_A, 4 * W_A)
    o = 4 * W_A
    bq_ref[...] = mm(o, o + 2 * W_B)
    o += 2 * W_B
    k = mm(o, o + W_B)
    k_ref[...] = k
    kb_ref[...] = k.astype(kb_ref.dtype)
    o += W_B
    v = mm(o, o + W_B)
    v_ref[...] = v
    vb_ref[...] = v.astype(vb_ref.dtype)
    o += W_B
    s_ref[...] = mm(o, o + LANES) + bias_ref[...]


def _inproj_e(x, w, bias):
    n, d = x.shape
    tm = _row_block(n, 256)
    wcols = w.shape[1]
    outs = [(4 * W_A, F32), (2 * W_B, F32), (W_B, F32), (W_B, F32), (LANES, F32), (W_B, w.dtype), (W_B, w.dtype)]
    return pl.pallas_call(
        _inproj_e_kernel,
        grid=(n // tm,),
        in_specs=[pl.BlockSpec((tm, d), lambda i: (i, 0)),
                  pl.BlockSpec((d, wcols), lambda i: (0, 0)),
                  pl.BlockSpec((1, LANES), lambda i: (0, 0))],
        out_specs=[pl.BlockSpec((tm, c), lambda i: (i, 0)) for c, _ in outs],
        out_shape=[jax.ShapeDtypeStruct((n, c), dt) for c, dt in outs],
        compiler_params=_cparams("parallel"),
        name="inproj_even",
    )(x, w, bias)


def _inproj_o_kernel(x_ref, w_ref, c_ref, d_ref):
    x = x_ref[...].astype(w_ref.dtype)
    c_ref[...] = _dot(x, w_ref[:, 0:2 * W_C])
    d_ref[...] = _dot(x, w_ref[:, 2 * W_C:2 * W_C + 4 * W_D])


def _inproj_o(x, w):
    n, d = x.shape
    tm = _row_block(n, 256)
    return pl.pallas_call(
        _inproj_o_kernel,
        grid=(n // tm,),
        in_specs=[pl.BlockSpec((tm, d), lambda i: (i, 0)),
                  pl.BlockSpec(w.shape, lambda i: (0, 0))],
        out_specs=[pl.BlockSpec((tm, 2 * W_C), lambda i: (i, 0)),
                   pl.BlockSpec((tm, 4 * W_D), lambda i: (i, 0))],
        out_shape=[jax.ShapeDtypeStruct((n, 2 * W_C), F32), jax.ShapeDtypeStruct((n, 4 * W_D), F32)],
        compiler_params=_cparams("parallel"),
        name="inproj_odd",
    )(x, w)


def _outproj_ln_kernel(y1_ref, y2_ref, w_ref, x_ref, g_ref, b_ref, o_ref):
    half = y1_ref.shape[1]
    y = _dot(y1_ref[...], w_ref[0:half, :]) + _dot(y2_ref[...], w_ref[half:2 * half, :])
    o_ref[...] = _layer_norm(ALPHA * x_ref[...] + y, g_ref[...], b_ref[...])


def _outproj_ln(y1, y2, w, x, g, b):
    n, d = x.shape
    tm = _row_block(n, 512)
    half = y1.shape[1]
    return pl.pallas_call(
        _outproj_ln_kernel,
        grid=(n // tm,),
        in_specs=[pl.BlockSpec((tm, half), lambda i: (i, 0)),
                  pl.BlockSpec((tm, half), lambda i: (i, 0)),
                  pl.BlockSpec(w.shape, lambda i: (0, 0)),
                  pl.BlockSpec((tm, d), lambda i: (i, 0)),
                  pl.BlockSpec((1, d), lambda i: (0, 0)),
                  pl.BlockSpec((1, d), lambda i: (0, 0))],
        out_specs=pl.BlockSpec((tm, d), lambda i: (i, 0)),
        out_shape=jax.ShapeDtypeStruct((n, d), F32),
        compiler_params=_cparams("parallel"),
        name="outproj_ln",
    )(y1, y2, w, x, g, b)


def _route(logits, bias):
    lane = lax.broadcasted_iota(I32, logits.shape, 1)
    valid = lane < N_EXPERTS
    pos = lane % EXP_PER_GROUP
    grp = (lane // EXP_PER_GROUP).astype(F32)
    s = jax.nn.sigmoid(logits)
    sel = jnp.where(valid, s + bias, NEG)
    rank = jnp.zeros(logits.shape, F32)
    for d in range(1, EXP_PER_GROUP):
        lo = pltpu.roll(sel, d, 1)
        hi = pltpu.roll(sel, LANES - d, 1)
        rank = rank + jnp.where(jnp.logical_and(pos >= d, lo >= sel), 1.0, 0.0)
        rank = rank + jnp.where(jnp.logical_and(pos + d < EXP_PER_GROUP, hi > sel), 1.0, 0.0)
    top2 = jnp.logical_and(rank < TOP_K_EXP, valid)
    contrib = jnp.where(top2, sel, 0.0)
    gs = contrib
    for d in range(1, EXP_PER_GROUP):
        lo = pltpu.roll(contrib, d, 1)
        hi = pltpu.roll(contrib, LANES - d, 1)
        gs = gs + jnp.where(pos >= d, lo, 0.0) + jnp.where(pos + d < EXP_PER_GROUP, hi, 0.0)
    gs = jnp.where(valid, gs, NEG)
    gmax = jnp.max(gs, axis=1, keepdims=True)
    best = jnp.min(jnp.where(gs == gmax, grp, 1e9), axis=1, keepdims=True)
    chosen = jnp.logical_and(top2, grp == best)
    s_sel = jnp.where(chosen, s, 0.0)
    return s_sel / jnp.sum(s_sel, axis=1, keepdims=True)


def _moe_kernel(x_ref, rw_ref, rb_ref, wg_ref, wu_ref, wd_ref, g_ref, b_ref, o_ref,
                xb_scr, comb_scr, acc_scr, *, sub):
    e = pl.program_id(1)
    tm = x_ref.shape[0]

    @pl.when(e == 0)
    def _():
        xb_scr[...] = x_ref[...].astype(xb_scr.dtype)
        logits = jnp.dot(x_ref[...], rw_ref[...], precision=lax.Precision.HIGHEST, preferred_element_type=F32)
        comb_scr[...] = _route(logits, rb_ref[...])
        acc_scr[...] = jnp.zeros_like(acc_scr)

    def rows(r, carry):
        r0 = pl.multiple_of(r * sub, sub)
        xb = xb_scr[pl.ds(r0, sub), :]
        gate = _dot(xb, wg_ref[...])
        h = gate * jax.nn.sigmoid(gate) * _dot(xb, wu_ref[...])
        y = _dot(h.astype(wd_ref.dtype), wd_ref[...])
        comb = comb_scr[pl.ds(r0, sub), :]
        lane = lax.broadcasted_iota(I32, comb.shape, 1)
        c_e = jnp.sum(jnp.where(lane == e, comb, 0.0), axis=1, keepdims=True)
        acc_scr[pl.ds(r0, sub), :] += c_e * y
        return carry

    lax.fori_loop(0, tm // sub, rows, 0)

    @pl.when(e == N_EXPERTS - 1)
    def _():
        o_ref[...] = _layer_norm(ALPHA * x_ref[...] + acc_scr[...], g_ref[...], b_ref[...])


def _moe_ln(x, rw, rb, wg, wu, wd, g, b):
    n, d = x.shape
    tm = _row_block(n, 1024)
    sub = _row_block(tm, 256)
    f = wg.shape[2]
    return pl.pallas_call(
        functools.partial(_moe_kernel, sub=sub),
        grid=(n // tm, N_EXPERTS),
        in_specs=[pl.BlockSpec((tm, d), lambda i, e: (i, 0)),
                  pl.BlockSpec((d, LANES), lambda i, e: (0, 0)),
                  pl.BlockSpec((1, LANES), lambda i, e: (0, 0)),
                  pl.BlockSpec((None, d, f), lambda i, e: (e, 0, 0)),
                  pl.BlockSpec((None, d, f), lambda i, e: (e, 0, 0)),
                  pl.BlockSpec((None, f, d), lambda i, e: (e, 0, 0)),
                  pl.BlockSpec((1, d), lambda i, e: (0, 0)),
                  pl.BlockSpec((1, d), lambda i, e: (0, 0))],
        out_specs=pl.BlockSpec((tm, d), lambda i, e: (i, 0)),
        out_shape=jax.ShapeDtypeStruct((n, d), F32),
        scratch_shapes=[pltpu.VMEM((tm, d), wg.dtype), pltpu.VMEM((tm, LANES), F32), pltpu.VMEM((tm, d), F32)],
        compiler_params=_cparams("parallel", "arbitrary"),
        name="moe_ln",
    )(x, rw, rb, wg, wu, wd, g, b)


def _route_t(logits, bias):
    s = jax.nn.sigmoid(logits)
    sel = s + bias
    top2, gscore = [], []
    for g in range(N_GROUPS):
        v = [sel[g * EXP_PER_GROUP + i:g * EXP_PER_GROUP + i + 1] for i in range(EXP_PER_GROUP)]
        flags = []
        for i in range(EXP_PER_GROUP):
            rank = jnp.zeros(v[i].shape, F32)
            for j in range(EXP_PER_GROUP):
                if j < i:
                    rank = rank + jnp.where(v[j] >= v[i], 1.0, 0.0)
                elif j > i:
                    rank = rank + jnp.where(v[j] > v[i], 1.0, 0.0)
            flags.append(rank < TOP_K_EXP)
        gs = jnp.where(flags[0], v[0], 0.0)
        for i in range(1, EXP_PER_GROUP):
            gs = gs + jnp.where(flags[i], v[i], 0.0)
        top2.append(flags)
        gscore.append(gs)
    gmax = gscore[0]
    for g in range(1, N_GROUPS):
        gmax = jnp.maximum(gmax, gscore[g])
    chosen, taken = [], None
    for g in range(N_GROUPS):
        c = gscore[g] == gmax
        if taken is not None:
            c = jnp.logical_and(c, jnp.logical_not(taken))
        taken = c if taken is None else jnp.logical_or(taken, c)
        chosen.append(c)
    gates = []
    for g in range(N_GROUPS):
        for i in range(EXP_PER_GROUP):
            e = g * EXP_PER_GROUP + i
            gates.append(jnp.where(jnp.logical_and(chosen[g], top2[g][i]), s[e:e + 1], 0.0))
    den = gates[0]
    for gt in gates[1:]:
        den = den + gt
    return [gt / den for gt in gates], chosen


MOE_TILE = 128

R_CHOSEN = EXP_PER_GROUP
R_RANK = EXP_PER_GROUP + 1


def _moe_group_kernel(x_ref, rwt_ref, rbt_ref, tri_ref, wg_ref, wu_ref, wd_ref, g_ref, b_ref, o_ref,
                      xb_scr, info_scr, acc_scr):
    g = pl.program_id(1)
    tb = x_ref.shape[0]
    mdt = xb_scr.dtype

    @pl.when(g == 0)
    def _():
        xb_scr[...] = x_ref[...].astype(mdt)
        logits = lax.dot_general(rwt_ref[...], x_ref[...], (((1,), (1,)), ((), ())),
                                 precision=lax.Precision.HIGHEST, preferred_element_type=F32)
        comb, chosen = _route_t(logits[0:N_EXPERTS], rbt_ref[0:N_EXPERTS, 0:1])
        flags = [jnp.where(c, 1.0, 0.0) for c in chosen]
        pad = jnp.zeros((SUBLANES - N_GROUPS, tb), F32)
        prefix = _dot(jnp.concatenate(flags + [pad], axis=0).astype(BF16), tri_ref[...])
        fill = jnp.zeros((SUBLANES - EXP_PER_GROUP - 2, tb), F32)
        for gg in range(N_GROUPS):
            rows = comb[gg * EXP_PER_GROUP:(gg + 1) * EXP_PER_GROUP] + [flags[gg], prefix[gg:gg + 1] - 1.0, fill]
            info_scr[gg] = jnp.concatenate(rows, axis=0)
        acc_scr[...] = jnp.zeros_like(acc_scr)

    info = info_scr[g]
    chosen_row = info[R_CHOSEN:R_CHOSEN + 1] > 0.5
    rank_row = info[R_RANK:R_RANK + 1]
    count = jnp.max(jnp.where(chosen_row, rank_row + 1.0, 0.0)).astype(I32)

    def tile(s, carry):
        rid = (lax.broadcasted_iota(I32, (MOE_TILE, tb), 0) + s * MOE_TILE).astype(F32)
        onehot = jnp.where(jnp.logical_and(rank_row == rid, chosen_row), 1.0, 0.0)
        pb = onehot.astype(mdt)
        xg = _dot(pb, xb_scr[...]).astype(mdt)
        out = jnp.zeros((MOE_TILE, x_ref.shape[1]), F32)
        for i in range(EXP_PER_GROUP):
            gate = jnp.sum(onehot * info[i:i + 1], axis=1, keepdims=True)
            gt = _dot(xg, wg_ref[i])
            h = gt * jax.nn.sigmoid(gt) * _dot(xg, wu_ref[i])
            out = out + gate * _dot(h.astype(mdt), wd_ref[i])
        hi = out.astype(mdt)
        lo = (out - hi.astype(F32)).astype(mdt)
        acc_scr[...] += _dot_tn(pb, hi) + _dot_tn(pb, lo)
        return carry

    lax.fori_loop(0, (count + MOE_TILE - 1) // MOE_TILE, tile, 0)

    @pl.when(g == N_GROUPS - 1)
    def _():
        o_ref[...] = _layer_norm(ALPHA * x_ref[...] + acc_scr[...], g_ref[...], b_ref[...])


def _moe_ln_grouped(x, rwt, rbt, wg, wu, wd, g, b):
    n, d = x.shape
    tb = _row_block(n, 512)
    f = wg.shape[2]
    tri = jnp.triu(jnp.ones((tb, tb), BF16))
    return pl.pallas_call(
        _moe_group_kernel,
        grid=(n // tb, N_GROUPS),
        in_specs=[pl.BlockSpec((tb, d), lambda i, q: (i, 0)),
                  pl.BlockSpec((LANES, d), lambda i, q: (0, 0)),
                  pl.BlockSpec((LANES, LANES), lambda i, q: (0, 0)),
                  pl.BlockSpec((tb, tb), lambda i, q: (0, 0)),
                  pl.BlockSpec((EXP_PER_GROUP, d, f), lambda i, q: (q, 0, 0)),
                  pl.BlockSpec((EXP_PER_GROUP, d, f), lambda i, q: (q, 0, 0)),
                  pl.BlockSpec((EXP_PER_GROUP, f, d), lambda i, q: (q, 0, 0)),
                  pl.BlockSpec((1, d), lambda i, q: (0, 0)),
                  pl.BlockSpec((1, d), lambda i, q: (0, 0))],
        out_specs=pl.BlockSpec((tb, d), lambda i, q: (i, 0)),
        out_shape=jax.ShapeDtypeStruct((n, d), F32),
        scratch_shapes=[pltpu.VMEM((tb, d), wg.dtype), pltpu.VMEM((N_GROUPS, SUBLANES, tb), F32),
                        pltpu.VMEM((tb, d), F32)],
        compiler_params=_cparams("parallel", "arbitrary"),
        name="moe_group_ln",
    )(x, rwt, rbt, tri, wg, wu, wd, g, b)


def _mlstm_kernel(a_ref, s_ref, g_ref, c0_ref, n0_ref, m0_ref, y_ref, c1_ref, n1_ref, m1_ref,
                  c_scr, n_scr, m_scr):
    c = pl.program_id(1)
    L = a_ref.shape[0]
    mdt = y_ref.dtype

    @pl.when(c == 0)
    def _():
        c_scr[...] = c0_ref[...]
        n_scr[...] = n0_ref[...]
        m_scr[...] = m0_ref[...]

    S = s_ref[...]
    lane = lax.broadcasted_iota(I32, S.shape, 1)
    is_f = jnp.logical_and(lane >= S_FG, lane < S_FG + H_A)
    lf = jnp.where(is_f, jax.nn.log_sigmoid(S), 0.0)
    row = lax.broadcasted_iota(I32, (L, L), 0)
    col = lax.broadcasted_iota(I32, (L, L), 1)
    causal = row >= col
    Fs = jnp.dot(causal.astype(F32), lf, precision=lax.Precision.HIGHEST, preferred_element_type=F32)
    Fa = pltpu.roll(Fs, LANES - (S_FG - S_IG), 1)
    AT = jnp.transpose(S - Fa)

    for h in range(H_A):
        ig = S[:, S_IG + h:S_IG + h + 1]
        F = Fa[:, S_IG + h:S_IG + h + 1]
        a_row = AT[S_IG + h:S_IG + h + 1, :]
        m_prev = m_scr[h:h + 1, 0:1]
        cm = jnp.max(jnp.where(causal, a_row, NEG), axis=1, keepdims=True)
        m_t = F + jnp.maximum(m_prev, cm)
        dmat = jnp.exp(jnp.where(causal, (F - m_t) + a_row, NEG))
        inter = jnp.exp(F + m_prev - m_t)
        q = a_ref[:, h * DK_A:(h + 1) * DK_A].astype(mdt)
        kf = a_ref[:, W_A + h * DK_A:W_A + (h + 1) * DK_A]
        k = kf.astype(mdt)
        v = a_ref[:, 2 * W_A + h * DV_A:2 * W_A + (h + 1) * DV_A].astype(mdt)
        C = c_scr[h]
        n = n_scr[h:h + 1, :]
        s = _dot_nt(q, k) * dmat
        num = _dot(s.astype(mdt), v) + inter * _dot(q, C.astype(mdt))
        qn = jnp.sum(q.astype(F32) * n.astype(mdt).astype(F32), axis=1, keepdims=True)
        den = jnp.sum(s, axis=1, keepdims=True) + inter * qn
        hout = num / jnp.maximum(jnp.abs(den), jnp.exp(-m_t))
        o = a_ref[:, 3 * W_A + h * DV_A:3 * W_A + (h + 1) * DV_A]
        y = _head_norm(hout) * g_ref[:, h * DV_A:(h + 1) * DV_A] * jax.nn.sigmoid(o)
        y_ref[:, h * DV_A:(h + 1) * DV_A] = y.astype(y_ref.dtype)
        m_new = m_t[L - 1:L, :]
        F_last = F[L - 1:L, :]
        w_s = jnp.exp(F_last - F + ig - m_new)
        decay = jnp.exp(F_last + m_prev - m_new)
        kw = kf * w_s
        c_scr[h] = decay * C + _dot_tn(kw.astype(mdt), v)
        n_scr[h:h + 1, :] = decay * n + jnp.sum(kw, axis=0, keepdims=True)
        m_scr[h:h + 1, :] = jnp.broadcast_to(m_new, (1, LANES))

    @pl.when(c == pl.num_programs(1) - 1)
    def _():
        c1_ref[...] = c_scr[...]
        n1_ref[...] = n_scr[...]
        m1_ref[...] = m_scr[...]


def _mlstm(a, s, gnorm, c0, n0, m0b, nb, nc, odt):
    n = a.shape[0]
    L = CHUNK
    return pl.pallas_call(
        _mlstm_kernel,
        grid=(nb, nc),
        in_specs=[pl.BlockSpec((L, 4 * W_A), lambda b, c: (b * nc + c, 0)),
                  pl.BlockSpec((L, LANES), lambda b, c: (b * nc + c, 0)),
                  pl.BlockSpec((1, W_A), lambda b, c: (0, 0)),
                  pl.BlockSpec((None, H_A, DK_A, DV_A), lambda b, c: (b, 0, 0, 0)),
                  pl.BlockSpec((None, H_A, DK_A), lambda b, c: (b, 0, 0)),
                  pl.BlockSpec((None, H_A, LANES), lambda b, c: (b, 0, 0))],
        out_specs=[pl.BlockSpec((L, W_A), lambda b, c: (b * nc + c, 0)),
                   pl.BlockSpec((None, H_A, DK_A, DV_A), lambda b, c: (b, 0, 0, 0)),
                   pl.BlockSpec((None, H_A, DK_A), lambda b, c: (b, 0, 0)),
                   pl.BlockSpec((None, H_A, LANES), lambda b, c: (b, 0, 0))],
        out_shape=[jax.ShapeDtypeStruct((n, W_A), odt),
                   jax.ShapeDtypeStruct((nb, H_A, DK_A, DV_A), F32),
                   jax.ShapeDtypeStruct((nb, H_A, DK_A), F32),
                   jax.ShapeDtypeStruct((nb, H_A, LANES), F32)],
        scratch_shapes=[pltpu.VMEM((H_A, DK_A, DV_A), F32), pltpu.VMEM((H_A, DK_A), F32),
                        pltpu.VMEM((H_A, LANES), F32)],
        compiler_params=_cparams("parallel", "arbitrary"),
        name="mlstm",
    )(a, s, gnorm, c0, n0, m0b)


def _retention_kernel(d_ref, cos_ref, sin_ref, dm_ref, dec_ref, g_ref, s0_ref, y_ref, s1_ref, s_scr):
    c = pl.program_id(1)

    @pl.when(c == 0)
    def _():
        s_scr[...] = s0_ref[...]

    cos2 = cos_ref[...]
    sin2 = sin_ref[...]
    mdt = y_ref.dtype
    for h in range(H_D):
        qf = d_ref[:, h * DK_D:(h + 1) * DK_D]
        kf = d_ref[:, W_D + h * DK_D:W_D + (h + 1) * DK_D]
        q = qf * cos2 + pltpu.roll(qf, DK_D // 2, 1) * sin2
        k = (kf * cos2 + pltpu.roll(kf, DK_D // 2, 1) * sin2) * DK_D ** -0.5
        v = d_ref[:, 2 * W_D + h * DV_D:2 * W_D + (h + 1) * DV_D].astype(mdt)
        qdec = dec_ref[h, :, 0:1]
        kdec = dec_ref[h, :, 1:2]
        sdec = dec_ref[h, 0:1, 2:3]
        S = s_scr[h]
        qb = q.astype(mdt)
        att = _dot_nt(qb, k.astype(mdt)) * dm_ref[h]
        o = _dot(att.astype(mdt), v) + qdec * _dot(qb, S.astype(mdt))
        s_scr[h] = sdec * S + _dot_tn((k * kdec).astype(mdt), v)
        gt = d_ref[:, 3 * W_D + h * DV_D:3 * W_D + (h + 1) * DV_D]
        y = _head_norm(o) * g_ref[:, h * DV_D:(h + 1) * DV_D] * (gt * jax.nn.sigmoid(gt))
        y_ref[:, h * DV_D:(h + 1) * DV_D] = y.astype(y_ref.dtype)

    @pl.when(c == pl.num_programs(1) - 1)
    def _():
        s1_ref[...] = s_scr[...]


def _retention(d, cos2, sin2, dmat, dec, gnorm, s0, nb, nc, odt):
    n = d.shape[0]
    L = CHUNK
    return pl.pallas_call(
        _retention_kernel,
        grid=(nb, nc),
        in_specs=[pl.BlockSpec((L, 4 * W_D), lambda b, c: (b * nc + c, 0)),
                  pl.BlockSpec((L, DK_D), lambda b, c: (c, 0)),
                  pl.BlockSpec((L, DK_D), lambda b, c: (c, 0)),
                  pl.BlockSpec((H_D, L, L), lambda b, c: (0, 0, 0)),
                  pl.BlockSpec((H_D, L, LANES), lambda b, c: (0, 0, 0)),
                  pl.BlockSpec((1, W_D), lambda b, c: (0, 0)),
                  pl.BlockSpec((None, H_D, DK_D, DV_D), lambda b, c: (b, 0, 0, 0))],
        out_specs=[pl.BlockSpec((L, W_D), lambda b, c: (b * nc + c, 0)),
                   pl.BlockSpec((None, H_D, DK_D, DV_D), lambda b, c: (b, 0, 0, 0))],
        out_shape=[jax.ShapeDtypeStruct((n, W_D), odt),
                   jax.ShapeDtypeStruct((nb, H_D, DK_D, DV_D), F32)],
        scratch_shapes=[pltpu.VMEM((H_D, DK_D, DV_D), F32)],
        compiler_params=_cparams("parallel", "arbitrary"),
        name="retention",
    )(d, cos2, sin2, dmat, dec, gnorm, s0)


def _shift_rows(x, prev, j):
    tb = x.shape[0]
    xs = pltpu.roll(x, j, 0)
    pr = pltpu.roll(prev, j, 0)
    row = lax.broadcasted_iota(I32, pr.shape, 0)
    first = jnp.where(row < j, pr, xs[0:SUBLANES])
    if tb == SUBLANES:
        return first
    return jnp.concatenate([first, xs[SUBLANES:]], axis=0)


def _rglru_kernel(c_ref, cw_ref, cb_ref, wa_ref, ba_ref, wx_ref, bx_ref, lam_ref, conv0_ref, h0_ref,
                  y_ref, hl_ref, prev_scr, h_scr):
    t = pl.program_id(1)
    tb = c_ref.shape[0]

    @pl.when(t == 0)
    def _():
        prev_scr[...] = conv0_ref[...]
        h_scr[...] = h0_ref[...]

    x = c_ref[:, 0:W_C]
    gate = c_ref[:, W_C:2 * W_C]
    prev = prev_scr[...]
    xc = x * cw_ref[CONV_W - 1:CONV_W, :] + cb_ref[...]
    for j in range(1, CONV_W):
        xc = xc + _shift_rows(x, prev, j) * cw_ref[CONV_W - 1 - j:CONV_W - j, :]
    prev_scr[...] = x[tb - SUBLANES:tb]

    xb = xc.astype(wa_ref.dtype)
    r = jax.nn.sigmoid(_dot(xb, wa_ref[...]) + ba_ref[...])
    i = jax.nn.sigmoid(_dot(xb, wx_ref[...]) + bx_ref[...])
    log_a = -LRU_C * r * jax.nn.softplus(-lam_ref[...])
    A = jnp.exp(log_a)
    th = jnp.tanh(log_a)
    U = jnp.sqrt(-2.0 * th / (1.0 - th)) * (i * xc)
    row = lax.broadcasted_iota(I32, (tb, W_C), 0)
    d = 1
    while d < tb:
        keep = row >= d
        U = jnp.where(keep, U + A * pltpu.roll(U, d, 0), U)
        A = jnp.where(keep, A * pltpu.roll(A, d, 0), A)
        d *= 2
    h = U + A * h_scr[0:1, :]
    h_scr[...] = jnp.broadcast_to(h[tb - 1:tb, :], (SUBLANES, W_C))
    y_ref[...] = (h * jax.nn.gelu(gate)).astype(y_ref.dtype)

    @pl.when(t == pl.num_programs(1) - 1)
    def _():
        hl_ref[...] = h_scr[...]


def _rglru(c, cw, cb, wa, ba, wx, bx, lam, conv0, h0, nb, nt, tb, odt):
    n = c.shape[0]
    vec = pl.BlockSpec((1, W_C), lambda b, t: (0, 0))
    mat = pl.BlockSpec((W_C, W_C), lambda b, t: (0, 0))
    st = pl.BlockSpec((None, SUBLANES, W_C), lambda b, t: (b, 0, 0))
    return pl.pallas_call(
        _rglru_kernel,
        grid=(nb, nt),
        in_specs=[pl.BlockSpec((tb, 2 * W_C), lambda b, t: (b * nt + t, 0)),
                  pl.BlockSpec((CONV_W, W_C), lambda b, t: (0, 0)), vec, mat, vec, mat, vec, vec, st, st],
        out_specs=[pl.BlockSpec((tb, W_C), lambda b, t: (b * nt + t, 0)), st],
        out_shape=[jax.ShapeDtypeStruct((n, W_C), odt), jax.ShapeDtypeStruct((nb, SUBLANES, W_C), F32)],
        scratch_shapes=[pltpu.VMEM((SUBLANES, W_C), F32), pltpu.VMEM((SUBLANES, W_C), F32)],
        compiler_params=_cparams("parallel", "arbitrary"),
        name="rglru",
    )(c, cw, cb, wa, ba, wx, bx, lam, conv0, h0)


def _sort_key(score):
    score = jnp.where(score == 0.0, 0.0, score)
    bits = lax.bitcast_convert_type(score, I32)
    return bits ^ ((bits >> 31) & 0x7FFFFFFF)


def _kth_largest_key(count_ge, k, rows):
    kf = float(k)
    zero = jnp.zeros((rows, 1), I32)
    t0 = jnp.where(count_ge(zero) >= kf, zero, jnp.full((rows, 1), INT_MIN, I32))

    def body(bi, t):
        cand = t + jnp.left_shift(jnp.int32(1), 30 - bi)
        return jnp.where(count_ge(cand) >= kf, cand, t)

    return lax.fori_loop(0, 31, body, t0)


def _first_positions(count_eq_before, need, nbits, rows):
    def body(bi, x):
        cand = x + jnp.left_shift(jnp.int32(1), nbits - 1 - bi)
        return jnp.where(count_eq_before(cand) < need, cand, x)

    return lax.fori_loop(0, nbits, body, jnp.zeros((rows, 1), I32))


def _qi_heads(qi):
    lane = lax.broadcasted_iota(I32, (qi.shape[0], LANES), 1)
    low = lane < D_IDX
    out = []
    for h in range(H_IDX):
        pair = qi[:, (h // 2) * LANES:(h // 2 + 1) * LANES]
        if h % 2:
            pair = pltpu.roll(pair, D_IDX, 1)
        out.append(jnp.where(low, pair, 0.0).astype(BF16))
    return out


def _dsa_prompt_kernel(q_ref, s_ref, ki_ref, kb_ref, vb_ref, o_ref, key_scr, bias_scr, x_scr, lg_scr,
                       *, q_off, kt_size, topk):
    i = pl.program_id(1)
    QB = q_ref.shape[0]
    KT = kt_size
    nkt = key_scr.shape[0]
    lk = nkt * KT
    q0 = q_off + i * QB
    qih = _qi_heads(q_ref[:, W_B:2 * W_B])
    wi = s_ref[:, S_WI:S_WI + H_IDX]
    wcols = [wi[:, h:h + 1] for h in range(H_IDX)]
    rowpos = q0 + lax.broadcasted_iota(I32, (QB, KT), 0)
    colpos = lax.broadcasted_iota(I32, (QB, KT), 1)

    def score_tile(kt, carry):
        k0 = pl.multiple_of(kt * KT, KT)
        ki = ki_ref[pl.ds(k0, KT), :].astype(BF16)
        sc = jnp.zeros((QB, KT), F32)
        for h in range(H_IDX):
            sc = sc + wcols[h] * jnp.maximum(_dot_nt(qih[h], ki), 0.0)
        sc = jnp.where(colpos + k0 <= rowpos, sc, -jnp.inf)
        key_scr[kt] = _sort_key(sc)
        return carry

    lax.fori_loop(0, nkt, score_tile, 0, unroll=2)

    lanepos = lax.broadcasted_iota(I32, (QB, LANES), 1)

    def count(pred):
        def body(kt, acc):
            tile = key_scr[kt]
            k0 = kt * KT
            for j in range(KT // LANES):
                sl = slice(j * LANES, (j + 1) * LANES)
                acc = acc + jnp.where(pred(tile[:, sl], lanepos + (k0 + j * LANES)), 1.0, 0.0)
            return acc
        acc = lax.fori_loop(0, nkt, body, jnp.zeros((QB, LANES), F32))
        return jnp.sum(acc, axis=1, keepdims=True)

    thr = _kth_largest_key(lambda t: count(lambda key, pos: key >= t), topk, QB)
    cnt_ge = count(lambda key, pos: key >= thr)
    cnt_gt = count(lambda key, pos: key > thr)
    x_scr[...] = jnp.full(x_scr.shape, lk, I32)

    @pl.when(jnp.max(cnt_ge) > float(topk))
    def _():
        need = float(topk) - cnt_gt
        nbits = max(1, (lk - 1).bit_length())
        x = _first_positions(
            lambda c: count(lambda key, pos: jnp.logical_and(key == thr, pos < c)), need, nbits, QB)
        x_scr[...] = jnp.broadcast_to(x, x_scr.shape)

    xlim = x_scr[:, 0:1]

    def bias_tile(kt, carry):
        key = key_scr[kt]
        pos = colpos + kt * KT
        sel = jnp.logical_or(key > thr, jnp.logical_and(key == thr, pos <= xlim))
        bias_scr[kt] = jnp.where(jnp.logical_and(sel, pos <= rowpos), 0.0, NEG)
        return carry

    lax.fori_loop(0, nkt, bias_tile, 0)

    low = lanepos < DH_B
    ngrp = KT // LANES

    def group_fold(op, acc, x):
        for g in range(ngrp):
            acc = op(acc, x[:, g * LANES:(g + 1) * LANES])
        return acc

    for j in range(H_B // 2):
        qpair = q_ref[:, j * LANES:(j + 1) * LANES] * DH_B ** -0.5
        q0m = jnp.where(low, qpair, 0.0).astype(BF16)
        q1m = jnp.where(low, 0.0, qpair).astype(BF16)

        def pass1(kt, carry, q0m=q0m, q1m=q1m, j=j):
            mx0, mx1 = carry
            k0 = pl.multiple_of(kt * KT, KT)
            kk = kb_ref[pl.ds(k0, KT), j * LANES:(j + 1) * LANES]
            bias = bias_scr[kt]
            lg0 = _dot_nt(q0m, kk) + bias
            lg1 = _dot_nt(q1m, kk) + bias
            lg_scr[0, kt] = lg0
            lg_scr[1, kt] = lg1
            return group_fold(jnp.maximum, mx0, lg0), group_fold(jnp.maximum, mx1, lg1)

        neg = jnp.full((QB, LANES), NEG, F32)
        mx0, mx1 = lax.fori_loop(0, nkt, pass1, (neg, neg), unroll=2)
        m0 = jnp.max(mx0, axis=1, keepdims=True)
        m1 = jnp.max(mx1, axis=1, keepdims=True)

        def pass2(kt, carry, m0=m0, m1=m1, j=j):
            l0, l1, a0, a1 = carry
            k0 = pl.multiple_of(kt * KT, KT)
            vv = vb_ref[pl.ds(k0, KT), j * LANES:(j + 1) * LANES]
            p0 = jnp.exp(lg_scr[0, kt] - m0)
            p1 = jnp.exp(lg_scr[1, kt] - m1)
            a0 = a0 + _dot(p0.astype(BF16), vv)
            a1 = a1 + _dot(p1.astype(BF16), vv)
            return group_fold(jnp.add, l0, p0), group_fold(jnp.add, l1, p1), a0, a1

        zero = jnp.zeros((QB, LANES), F32)
        l0, l1, a0, a1 = lax.fori_loop(0, nkt, pass2, (zero, zero, zero, zero), unroll=2)
        out0 = a0 / jnp.sum(l0, axis=1, keepdims=True)
        out1 = a1 / jnp.sum(l1, axis=1, keepdims=True)
        o_ref[:, j * LANES:(j + 1) * LANES] = jnp.where(low, out0, out1).astype(o_ref.dtype)


def _dsa_prompt(bq, s, kb, vb, nb, t):
    n = bq.shape[0]
    QB = math.gcd(t, CHUNK)
    topk = min(TOPK_MAX, t // 4)
    nseg = 1
    for cand in (8, 4, 2):
        if t % cand == 0 and (t // cand) % 512 == 0:
            nseg = cand
            break
    seg = t // nseg
    KT = 512 if seg % 512 == 0 else seg
    s3 = s.reshape(nb, t, LANES)
    kb3 = kb.reshape(nb, t, W_B)
    vb3 = vb.reshape(nb, t, W_B)
    nqb = seg // QB
    outs = []
    for g in range(nseg):
        lk = (g + 1) * seg
        nkt = lk // KT
        row_blk = functools.partial(lambda b, i, g: (b * (t // QB) + g * nqb + i, 0), g=g)
        outs.append(pl.pallas_call(
            functools.partial(_dsa_prompt_kernel, q_off=g * seg, kt_size=KT, topk=topk),
            grid=(nb, nqb),
            in_specs=[pl.BlockSpec((QB, 2 * W_B), row_blk),
                      pl.BlockSpec((QB, LANES), row_blk),
                      pl.BlockSpec((None, lk, LANES), lambda b, i: (b, 0, 0)),
                      pl.BlockSpec((None, lk, W_B), lambda b, i: (b, 0, 0)),
                      pl.BlockSpec((None, lk, W_B), lambda b, i: (b, 0, 0))],
            out_specs=pl.BlockSpec((None, QB, W_B), lambda b, i: (b, i, 0)),
            out_shape=jax.ShapeDtypeStruct((nb, seg, W_B), BF16),
            scratch_shapes=[pltpu.VMEM((nkt, QB, KT), I32), pltpu.VMEM((nkt, QB, KT), F32),
                            pltpu.VMEM((QB, LANES), I32), pltpu.VMEM((2, nkt, QB, KT), F32)],
            compiler_params=_cparams("parallel", "arbitrary"),
            name=f"dsa_prompt_{g}",
        )(bq, s, s3, kb3, vb3))
    return jnp.concatenate(outs, axis=1).reshape(n, W_B)


SCORE_PAGES = 16
ATTN_PAGES = 8


def _dsa_sample_score_kernel(pt_ref, q_ref, s_ref, snew_ref, *rest):
    page_refs, o_ref = rest[:-1], rest[-1]
    p = pl.program_id(1)
    last = pl.num_programs(1) - 1
    T = q_ref.shape[0]
    qi = q_ref[:, W_B:2 * W_B]
    wi = s_ref[:, S_WI:S_WI + H_IDX]
    qst = jnp.concatenate([qi[:, h * D_IDX:(h + 1) * D_IDX] for h in range(H_IDX)], axis=0)
    wcol = jnp.concatenate([wi[:, h:h + 1] for h in range(H_IDX)], axis=0)

    def scores(dots):
        r = wcol * jnp.maximum(dots, 0.0)
        sc = r[0:T]
        for h in range(1, H_IDX):
            sc = sc + r[h * T:(h + 1) * T]
        return sc

    @pl.when(p < last)
    def _():
        for j, page_ref in enumerate(page_refs):
            o_ref[j] = scores(_dot(qst, page_ref[...]))

    @pl.when(p == last)
    def _():
        sc = scores(_dot_nt(qst, snew_ref[:, S_KI:S_KI + D_IDX]))
        r = lax.broadcasted_iota(I32, sc.shape, 0)
        c = lax.broadcasted_iota(I32, sc.shape, 1)
        o_ref[0] = jnp.where(c <= r, sc, -jnp.inf)
        for j in range(1, len(page_refs)):
            o_ref[j] = jnp.full(sc.shape, -jnp.inf, F32)


def _dsa_sample_attn_kernel(pt_ref, q_ref, sc_ref, knew_ref, vnew_ref, *rest, topk, npg, G):
    kpages, vpages, o_ref = rest[:G], rest[G:2 * G], rest[2 * G]
    thr_scr, x_scr, m_scr, l_scr, acc_scr = rest[2 * G + 1:]
    p = pl.program_id(1)
    last = pl.num_programs(1) - 1
    T = q_ref.shape[0]
    R = H_B * T
    ltot = sc_ref.shape[0] * PAGE_SIZE

    @pl.when(p == 0)
    def _():
        key = _sort_key(sc_ref[...])
        pos = (lax.broadcasted_iota(I32, key.shape, 0) * PAGE_SIZE
               + lax.broadcasted_iota(I32, key.shape, 2))

        def count(pred):
            per_lane = jnp.sum(jnp.where(pred(key, pos), 1.0, 0.0), axis=0)
            return jnp.sum(per_lane, axis=1, keepdims=True)

        thr = _kth_largest_key(lambda t: count(lambda k_, p_: k_ >= t), topk, T)
        cnt_gt = count(lambda k_, p_: k_ > thr)
        nbits = max(1, (ltot - 1).bit_length())
        x = _first_positions(lambda c: count(lambda k_, p_: jnp.logical_and(k_ == thr, p_ < c)),
                             float(topk) - cnt_gt, nbits, T)
        thr_scr[...] = jnp.broadcast_to(thr, thr_scr.shape)
        x_scr[...] = jnp.broadcast_to(x, x_scr.shape)
        m_scr[...] = jnp.full(m_scr.shape, NEG, F32)
        l_scr[...] = jnp.zeros_like(l_scr)
        acc_scr[...] = jnp.zeros_like(acc_scr)

    q = q_ref[:, 0:W_B] * DH_B ** -0.5
    qrep = jnp.concatenate([q] * H_B, axis=0)
    rr = lax.broadcasted_iota(I32, (R, W_B), 0) // T
    cc = lax.broadcasted_iota(I32, (R, W_B), 1) // DH_B
    diag = rr == cc
    qbd = jnp.where(diag, qrep, 0.0)
    thr = thr_scr[:, 0:1]
    xlim = x_scr[:, 0:1]

    def selected(page):
        key = _sort_key(sc_ref[page])
        pos = page * PAGE_SIZE + lax.broadcasted_iota(I32, key.shape, 1)
        return jnp.logical_or(key > thr, jnp.logical_and(key == thr, pos <= xlim))

    def masked(logits, valid):
        bias = jnp.where(valid, 0.0, NEG)
        return logits + jnp.concatenate([bias] * H_B, axis=0)

    def update(lgs, pv):
        mx = lgs[0]
        for lg in lgs[1:]:
            mx = jnp.maximum(mx, lg)
        m = m_scr[:, 0:1]
        m_new = jnp.maximum(m, jnp.max(mx, axis=1, keepdims=True))
        alpha = jnp.exp(m - m_new)
        ps = [jnp.exp(lg - m_new) for lg in lgs]
        psum = ps[0]
        for pr in ps[1:]:
            psum = psum + pr
        acc = pv(0, ps[0])
        for j in range(1, len(ps)):
            acc = acc + pv(j, ps[j])
        l_scr[...] = jnp.broadcast_to(alpha * l_scr[:, 0:1] + jnp.sum(psum, axis=1, keepdims=True), l_scr.shape)
        acc_scr[...] = alpha * acc_scr[...] + acc
        m_scr[...] = jnp.broadcast_to(m_new, m_scr.shape)

    @pl.when(p < last)
    def _():
        lgs = [masked(_dot(qbd, kpages[j][...]), selected(p * G + j)) for j in range(G)]
        update(lgs, lambda j, pr: _dot_nt(pr, vpages[j][...]))

    @pl.when(p == last)
    def _():
        sel = selected(npg)
        r = lax.broadcasted_iota(I32, sel.shape, 0)
        c = lax.broadcasted_iota(I32, sel.shape, 1)
        lg = masked(_dot_nt(qbd, knew_ref[...]), jnp.logical_and(sel, c <= r))
        update([lg], lambda j, pr: _dot(pr, vnew_ref[...]))
        out = jnp.where(diag, acc_scr[...] / l_scr[:, 0:1], 0.0)
        res = out[0:T]
        for h in range(1, H_B):
            res = res + out[h * T:(h + 1) * T]
        o_ref[...] = res.astype(o_ref.dtype)


def _dsa_sample(bq, s, k, v, pool_k, pool_v, pool_ki, page_table, nb, t):
    n_pool = pool_k.shape[1]
    pool_k = jnp.transpose(pool_k, (0, 1, 3, 4, 2)).reshape(1, n_pool, W_B, PAGE_SIZE)
    pool_v = jnp.transpose(pool_v, (0, 1, 3, 4, 2)).reshape(1, n_pool, W_B, PAGE_SIZE)
    pool_ki = jnp.transpose(pool_ki, (0, 1, 3, 2))
    npg = page_table.shape[1]
    past = npg * PAGE_SIZE
    topk = min(TOPK_MAX, (past + t) // 4)
    gs, ga = math.gcd(npg, SCORE_PAGES), math.gcd(npg, ATTN_PAGES)
    pad = PAGE_SIZE - t
    s_new = jnp.pad(s.reshape(nb, t, LANES), ((0, 0), (0, pad), (0, 0)))
    k_new = jnp.pad(k.reshape(nb, t, W_B), ((0, 0), (0, pad), (0, 0)))
    v_new = jnp.pad(v.reshape(nb, t, W_B), ((0, 0), (0, pad), (0, 0)))
    lastp = npg - 1
    ptot = npg + gs

    def page_spec(tail, j, group):
        zeros = (0,) * len(tail)
        return pl.BlockSpec((None, None) + tail,
                            lambda b, p, pt: (0, pt[b, jnp.minimum(p * group + j, lastp)]) + zeros)

    scores = pl.pallas_call(
        _dsa_sample_score_kernel,
        grid_spec=pltpu.PrefetchScalarGridSpec(
            num_scalar_prefetch=1,
            grid=(nb, npg // gs + 1),
            in_specs=[pl.BlockSpec((t, 2 * W_B), lambda b, p, pt: (b, 0)),
                      pl.BlockSpec((t, LANES), lambda b, p, pt: (b, 0)),
                      pl.BlockSpec((None, PAGE_SIZE, LANES), lambda b, p, pt: (b, 0, 0))]
            + [page_spec((D_IDX, PAGE_SIZE), j, gs) for j in range(gs)],
            out_specs=pl.BlockSpec((None, gs, t, PAGE_SIZE), lambda b, p, pt: (b, p, 0, 0))),
        out_shape=jax.ShapeDtypeStruct((nb, ptot, t, PAGE_SIZE), F32),
        compiler_params=_cparams("parallel", "arbitrary"),
        name="dsa_sample_scores",
    )(page_table, bq, s, s_new, *([pool_ki] * gs))

    kv_tail = (W_B, PAGE_SIZE)
    return pl.pallas_call(
        functools.partial(_dsa_sample_attn_kernel, topk=topk, npg=npg, G=ga),
        grid_spec=pltpu.PrefetchScalarGridSpec(
            num_scalar_prefetch=1,
            grid=(nb, npg // ga + 1),
            in_specs=[pl.BlockSpec((t, 2 * W_B), lambda b, p, pt: (b, 0)),
                      pl.BlockSpec((None, ptot, t, PAGE_SIZE), lambda b, p, pt: (b, 0, 0, 0)),
                      pl.BlockSpec((None, PAGE_SIZE, W_B), lambda b, p, pt: (b, 0, 0)),
                      pl.BlockSpec((None, PAGE_SIZE, W_B), lambda b, p, pt: (b, 0, 0))]
            + [page_spec(kv_tail, j, ga) for j in range(ga)]
            + [page_spec(kv_tail, j, ga) for j in range(ga)],
            out_specs=pl.BlockSpec((t, W_B), lambda b, p, pt: (b, 0)),
            scratch_shapes=[pltpu.VMEM((t, LANES), I32), pltpu.VMEM((t, LANES), I32),
                            pltpu.VMEM((H_B * t, LANES), F32), pltpu.VMEM((H_B * t, LANES), F32),
                            pltpu.VMEM((H_B * t, W_B), F32)]),
        out_shape=jax.ShapeDtypeStruct((nb * t, W_B), F32),
        compiler_params=_cparams("parallel", "arbitrary"),
        name="dsa_sample_attn",
    )(page_table, bq, scores, k_new, v_new, *([pool_k] * ga), *([pool_v] * ga))


def _moe(x, prm, layer):
    wg, wu, wd = prm["moe_wg"][layer], prm["moe_wu"][layer], prm["moe_wd"][layer]
    g, b = prm["ln_g"][layer, 1], prm["ln_b"][layer, 1]
    if wg.dtype == BF16:
        return _moe_ln_grouped(x, prm["router_wt"], prm["router_bt"], wg, wu, wd, g, b)
    return _moe_ln(x, prm["router_w"], prm["router_b"], wg, wu, wd, g, b)


def _pad_chunks(a, nb, t, fill=None):
    c = a.shape[1]
    a3 = a.reshape(nb, t, c)
    if fill is None:
        a3 = jnp.pad(a3, ((0, 0), (0, CHUNK - t), (0, 0)))
    else:
        a3 = jnp.concatenate([a3, jnp.broadcast_to(fill, (nb, CHUNK - t, c))], axis=1)
    return a3.reshape(nb * CHUNK, c)


def _retention_tables(t_true, pos0, t_pad):
    L = CHUNK
    lt = min(L, t_true)
    lg = jnp.log1p(-jnp.exp2(-5.0 - jnp.arange(H_D, dtype=F32)))
    j = jnp.arange(L, dtype=F32)
    causal = jnp.tril(jnp.ones((L, L), dtype=bool))
    dmat = jnp.exp(jnp.where(causal, (j[:, None] - j[None, :]) * lg[:, None, None], -jnp.inf))
    qdec = jnp.exp((j + 1.0) * lg[:, None])
    kdec = jnp.where(j < lt, jnp.exp((lt - 1.0 - j) * lg[:, None]), 0.0)
    sdec = jnp.broadcast_to(jnp.exp(lt * lg)[:, None], (H_D, L))
    dec = jnp.zeros((H_D, L, LANES), F32)
    dec = dec.at[:, :, 0].set(qdec).at[:, :, 1].set(kdec).at[:, :, 2].set(sdec)
    half = DK_D // 2
    freq = ROPE_BASE ** (-jnp.arange(half, dtype=F32) / half)
    pos = (pos0 + jnp.arange(t_pad)).astype(F32)
    ang = pos[:, None] * freq[None, :]
    cos, sin = jnp.cos(ang), jnp.sin(ang)
    return dmat, dec, jnp.concatenate([cos, cos], -1), jnp.concatenate([-sin, sin], -1)


def _forward(x3, st, prm):
    nb, t, d = x3.shape
    n = nb * t
    x = x3.reshape(n, d)
    short = t < CHUNK
    t_pad = CHUNK if short else t
    nc = t_pad // CHUNK
    new = {}
    odt = prm["w_in_e"].dtype

    a, bq, k, v, s, kb, vb = _inproj_e(x, prm["w_in_e"], prm["bias_e"])
    if st is None:
        c0 = jnp.zeros((nb, H_A, DK_A, DV_A), F32)
        n0 = jnp.zeros((nb, H_A, DK_A), F32)
        m0 = jnp.zeros((nb, H_A), F32)
    else:
        c0, n0, m0 = st["mlstm_C"][0], st["mlstm_n"][0], st["mlstm_m"][0]
    m0b = jnp.broadcast_to(m0[:, :, None], (nb, H_A, LANES))
    if short:
        lane = jnp.arange(LANES)
        fill = jnp.where((lane >= S_IG) & (lane < S_IG + H_A), NEG,
                         jnp.where((lane >= S_FG) & (lane < S_FG + H_A), 1e4, 0.0)).astype(F32)
        a_m, s_m = _pad_chunks(a, nb, t), _pad_chunks(s, nb, t, fill)
    else:
        a_m, s_m = a, s
    ya, c1, n1, m1 = _mlstm(a_m, s_m, prm["a_norm_g"], c0, n0, m0b, nb, nc, odt)
    if short:
        ya = ya.reshape(nb, CHUNK, W_A)[:, :t].reshape(n, W_A)
    if st is None:
        yb = _dsa_prompt(bq, s, kb, vb, nb, t)
    else:
        yb = _dsa_sample(bq, s, k, v, st["pool_k"], st["pool_v"], st["pool_ki"], st["page_table"], nb, t)
    new["mlstm_C"], new["mlstm_n"], new["mlstm_m"] = c1[None], n1[None], m1[None, :, :, 0]
    new["k"] = k.reshape(1, nb, t, H_B, DH_B)
    new["v"] = v.reshape(1, nb, t, H_B, DH_B)
    new["kidx"] = s[:, S_KI:S_KI + D_IDX].reshape(1, nb, t, D_IDX)
    x = _outproj_ln(ya, yb, prm["w_out_e"], x, prm["ln_g"][0, 0], prm["ln_b"][0, 0])
    x = _moe(x, prm, 0)

    c, dd = _inproj_o(x, prm["w_in_o"])
    if st is None:
        conv0 = jnp.zeros((nb, SUBLANES, W_C), F32)
        h0 = jnp.zeros((nb, SUBLANES, W_C), F32)
        s0 = jnp.zeros((nb, H_D, DK_D, DV_D), F32)
        pos0 = 0
    else:
        conv0 = jnp.pad(st["conv"][0], ((0, 0), (SUBLANES - (CONV_W - 1), 0), (0, 0)))
        h0 = jnp.broadcast_to(st["lru_h"][0][:, None, :], (nb, SUBLANES, W_C))
        s0 = st["ret_S"][0]
        pos0 = st["page_table"].shape[1] * PAGE_SIZE
    tb = _row_block(t, 256)
    yc, hl = _rglru(c, prm["c_conv_w"], prm["c_conv_b"], prm["c_wa"], prm["c_ba"], prm["c_wx"], prm["c_bx"],
                    prm["c_lambda"], conv0, h0, nb, t // tb, tb, odt)
    dmat, dec, cos2, sin2 = _retention_tables(t, pos0, t_pad)
    d_m = _pad_chunks(dd, nb, t) if short else dd
    yd, s1 = _retention(d_m, cos2, sin2, dmat, dec, prm["d_norm_g"], s0, nb, nc, odt)
    if short:
        yd = yd.reshape(nb, CHUNK, W_D)[:, :t].reshape(n, W_D)
    new["conv"] = c[:, 0:W_C].reshape(nb, t, W_C)[None, :, t - (CONV_W - 1):]
    new["lru_h"] = hl[None, :, 0]
    new["ret_S"] = s1[None]
    x = _outproj_ln(yc, yd, prm["w_out_o"], x, prm["ln_g"][1, 0], prm["ln_b"][1, 0])
    x = _moe(x, prm, 1)
    return x.reshape(nb, t, d), new


def _block_diag(w):
    nblk, blk, _ = w.shape
    eye = jnp.eye(nblk, dtype=w.dtype)
    return (eye[:, None, :, None] * w[:, :, None, :]).reshape(nblk * blk, nblk * blk)


def _prepare_params(w_in_e, b_if_e, a_norm_g, w_out_e, w_in_o, c_conv_w, c_conv_b, c_wa, c_ba, c_wx, c_bx,
                    c_lambda, d_norm_g, w_out_o, router_w, router_bias, moe_w_gate, moe_w_up, moe_w_down,
                    ln_g, ln_b, wdt):
    d = w_in_e.shape[1]
    o = [0]
    for width in (W_A, W_A, W_A, W_A, 2 * H_A, W_B, W_B, W_B, H_IDX * D_IDX, D_IDX, H_IDX):
        o.append(o[-1] + width)
    we = w_in_e[0]
    col = lambda i: we[:, o[i]:o[i + 1]]
    slab = jnp.zeros((d, LANES), F32)
    slab = slab.at[:, S_KI:S_KI + D_IDX].set(col(9)).at[:, S_IG:S_IG + 2 * H_A].set(col(4))
    slab = slab.at[:, S_WI:S_WI + H_IDX].set(col(10))
    w_e = jnp.concatenate([col(0), col(1), col(2), col(3), col(5), col(8), col(6), col(7), slab], axis=1)
    bias_e = jnp.zeros((1, LANES), F32).at[0, S_IG:S_IG + 2 * H_A].set(b_if_e[0])
    rw = jnp.zeros((d, LANES), F32).at[:, :N_EXPERTS].set(router_w)
    rb = jnp.zeros((1, LANES), F32).at[0, :N_EXPERTS].set(router_bias)
    row = lambda a: a.reshape(1, -1).astype(F32)
    return {
        "w_in_e": w_e.astype(wdt), "bias_e": bias_e, "a_norm_g": row(a_norm_g[0]),
        "w_out_e": w_out_e[0].astype(wdt), "w_in_o": w_in_o[0].astype(wdt),
        "c_conv_w": c_conv_w[0].astype(F32), "c_conv_b": row(c_conv_b[0]),
        "c_wa": _block_diag(c_wa[0]).astype(wdt), "c_ba": row(c_ba[0]),
        "c_wx": _block_diag(c_wx[0]).astype(wdt), "c_bx": row(c_bx[0]),
        "c_lambda": row(c_lambda[0]), "d_norm_g": row(d_norm_g[0]), "w_out_o": w_out_o[0].astype(wdt),
        "router_w": rw, "router_b": rb,
        "router_wt": rw.T, "router_bt": jnp.broadcast_to(rb.reshape(LANES, 1), (LANES, LANES)),
        "moe_wg": moe_w_gate.astype(wdt), "moe_wu": moe_w_up.astype(wdt), "moe_wd": moe_w_down.astype(wdt),
        "ln_g": ln_g.reshape(DEPTH, 2, 1, -1).astype(F32), "ln_b": ln_b.reshape(DEPTH, 2, 1, -1).astype(F32),
    }


def kernel(x_prompt, x_sample, state_mlstm_C, state_mlstm_n, state_mlstm_m, cache_k, cache_v, cache_kidx,
           page_table, state_conv, state_lru_h, state_ret_S, w_in_e, b_if_e, a_norm_g, w_out_e, w_in_o,
           c_conv_w, c_conv_b, c_wa, c_ba, c_wx, c_bx, c_lambda, d_norm_g, w_out_o, router_w, router_bias,
           moe_w_gate, moe_w_up, moe_w_down, ln_g, ln_b):
    weights = (w_in_e, b_if_e, a_norm_g, w_out_e, w_in_o, c_conv_w, c_conv_b, c_wa, c_ba, c_wx, c_bx, c_lambda,
               d_norm_g, w_out_o, router_w, router_bias, moe_w_gate, moe_w_up, moe_w_down, ln_g, ln_b)
    prm_prompt = _prepare_params(*weights, BF16)
    prm_sample = _prepare_params(*weights, F32)
    st = {"mlstm_C": state_mlstm_C, "mlstm_n": state_mlstm_n, "mlstm_m": state_mlstm_m,
          "pool_k": cache_k, "pool_v": cache_v, "pool_ki": cache_kidx, "page_table": page_table,
          "conv": state_conv, "lru_h": state_lru_h, "ret_S": state_ret_S}
    y_p, nsp = _forward(x_prompt, None, prm_prompt)
    y_s, nss = _forward(x_sample, st, prm_sample)
    names = ("mlstm_C", "mlstm_n", "mlstm_m", "k", "v", "kidx", "conv", "lru_h", "ret_S")
    return (y_p, y_s) + tuple(nsp[k] for k in names) + tuple(nss[k] for k in names)
```

```python
import functools
import math

import jax
import jax.numpy as jnp
from jax import lax
from jax.experimental import pallas as pl
from jax.experimental.pallas import tpu as pltpu

F32 = jnp.float32
BF16 = jnp.bfloat16
I32 = jnp.int32

DEPTH = 2
PAGE_SIZE = 128
H_A, DK_A, DV_A = 4, 128, 128
W_A = H_A * DV_A
H_B, DH_B = 8, 64
W_B = H_B * DH_B
H_IDX, D_IDX = 8, 64
TOPK_MAX = 256
W_C, N_BLK_C, CONV_W, LRU_C = 512, 8, 4, 8.0
BLK_C = W_C // N_BLK_C
H_D, DK_D, DV_D = 4, 128, 128
W_D = H_D * DV_D
ROPE_BASE = 10000.0
N_EXPERTS, N_GROUPS, TOP_K_EXP, D_FF_EXP = 16, 4, 2, 512
EXP_PER_GROUP = N_EXPERTS // N_GROUPS
ALPHA = (2 * DEPTH) ** 0.25
LN_EPS = 1e-5
HN_EPS = 1e-6

CHUNK = 128
LANES = 128
SUBLANES = 8
NEG = -1e30
INT_MIN = -2 ** 31
VMEM_LIMIT = 56 * 1024 * 1024

S_KI = 0
S_IG = 64
S_FG = 68
S_WI = 72


def _cparams(*sem):
    return pltpu.CompilerParams(dimension_semantics=sem, vmem_limit_bytes=VMEM_LIMIT)


def _precision(a, b):
    return lax.Precision.HIGHEST if a.dtype == F32 and b.dtype == F32 else None


def _dot(a, b):
    return jnp.dot(a, b, preferred_element_type=F32, precision=_precision(a, b))


def _dot_nt(a, b):
    return lax.dot_general(a, b, (((1,), (1,)), ((), ())), preferred_element_type=F32, precision=_precision(a, b))


def _dot_tn(a, b):
    return lax.dot_general(a, b, (((0,), (0,)), ((), ())), preferred_element_type=F32, precision=_precision(a, b))


def _split2(a):
    hi = a.astype(BF16)
    return hi, (a - hi.astype(F32)).astype(BF16)


def _dot3(a2, b2, dot):
    return dot(a2[0], b2[0]) + (dot(a2[0], b2[1]) + dot(a2[1], b2[0]))


def _row_block(n, target):
    t = min(n, target)
    while n % t:
        t //= 2
    return t


def _layer_norm(z, g, b):
    mu = jnp.mean(z, -1, keepdims=True)
    zc = z - mu
    var = jnp.mean(zc * zc, -1, keepdims=True)
    return zc * lax.rsqrt(var + LN_EPS) * g + b


def _head_norm(h):
    mu = jnp.mean(h, -1, keepdims=True)
    hc = h - mu
    var = jnp.mean(hc * hc, -1, keepdims=True)
    return hc * lax.rsqrt(var + HN_EPS)


def _inproj_e_kernel(x_ref, w_ref, bias_ref, a_ref, bq_ref, k_ref, v_ref, s_ref, kb_ref, vb_ref):
    x = x_ref[...].astype(w_ref.dtype)

    def mm(lo, hi):
        return _dot(x, w_ref[:, lo:hi])

    a_ref[:, 0:W_A] = mm(0, W_A)
    a_ref[:, W_A:2 * W_A] = mm(W_A, 2 * W_A) * DK_A ** -0.5
    a_ref[:, 2 * W_A:4 * W_A] = mm(2 * W_A, 4 * W_A)
    o = 4 * W_A
    bq_ref[...] = mm(o, o + 2 * W_B)
    o += 2 * W_B
    k = mm(o, o + W_B)
    k_ref[...] = k
    kb_ref[...] = k.astype(kb_ref.dtype)
    o += W_B
    v = mm(o, o + W_B)
    v_ref[...] = v
    vb_ref[...] = v.astype(vb_ref.dtype)
    o += W_B
    s_ref[...] = mm(o, o + LANES) + bias_ref[...]


def _inproj_e(x, w, bias):
    n, d = x.shape
    tm = _row_block(n, 256)
    wcols = w.shape[1]
    outs = [(4 * W_A, F32), (2 * W_B, F32), (W_B, F32), (W_B, F32), (LANES, F32), (W_B, w.dtype), (W_B, w.dtype)]
    return pl.pallas_call(
        _inproj_e_kernel,
        grid=(n // tm,),
        in_specs=[pl.BlockSpec((tm, d), lambda i: (i, 0)),
                  pl.BlockSpec((d, wcols), lambda i: (0, 0)),
                  pl.BlockSpec((1, LANES), lambda i: (0, 0))],
        out_specs=[pl.BlockSpec((tm, c), lambda i: (i, 0)) for c, _ in outs],
        out_shape=[jax.ShapeDtypeStruct((n, c), dt) for c, dt in outs],
        compiler_params=_cparams("parallel"),
        name="inproj_even",
    )(x, w, bias)


def _inproj_o_kernel(x_ref, w_ref, c_ref, d_ref):
    x = x_ref[...].astype(w_ref.dtype)
    c_ref[...] = _dot(x, w_ref[:, 0:2 * W_C])
    d_ref[...] = _dot(x, w_ref[:, 2 * W_C:2 * W_C + 4 * W_D])


def _inproj_o(x, w):
    n, d = x.shape
    tm = _row_block(n, 256)
    return pl.pallas_call(
        _inproj_o_kernel,
        grid=(n // tm,),
        in_specs=[pl.BlockSpec((tm, d), lambda i: (i, 0)),
                  pl.BlockSpec(w.shape, lambda i: (0, 0))],
        out_specs=[pl.BlockSpec((tm, 2 * W_C), lambda i: (i, 0)),
                   pl.BlockSpec((tm, 4 * W_D), lambda i: (i, 0))],
        out_shape=[jax.ShapeDtypeStruct((n, 2 * W_C), F32), jax.ShapeDtypeStruct((n, 4 * W_D), F32)],
        compiler_params=_cparams("parallel"),
        name="inproj_odd",
    )(x, w)


def _outproj_ln_kernel(y1_ref, y2_ref, w_ref, x_ref, g_ref, b_ref, o_ref):
    half = y1_ref.shape[1]
    y = _dot(y1_ref[...], w_ref[0:half, :]) + _dot(y2_ref[...], w_ref[half:2 * half, :])
    o_ref[...] = _layer_norm(ALPHA * x_ref[...] + y, g_ref[...], b_ref[...])


def _outproj_ln(y1, y2, w, x, g, b):
    n, d = x.shape
    tm = _row_block(n, 512)
    half = y1.shape[1]
    return pl.pallas_call(
        _outproj_ln_kernel,
        grid=(n // tm,),
        in_specs=[pl.BlockSpec((tm, half), lambda i: (i, 0)),
                  pl.BlockSpec((tm, half), lambda i: (i, 0)),
                  pl.BlockSpec(w.shape, lambda i: (0, 0)),
                  pl.BlockSpec((tm, d), lambda i: (i, 0)),
                  pl.BlockSpec((1, d), lambda i: (0, 0)),
                  pl.BlockSpec((1, d), lambda i: (0, 0))],
        out_specs=pl.BlockSpec((tm, d), lambda i: (i, 0)),
        out_shape=jax.ShapeDtypeStruct((n, d), F32),
        compiler_params=_cparams("parallel"),
        name="outproj_ln",
    )(y1, y2, w, x, g, b)


def _route(logits, bias):
    lane = lax.broadcasted_iota(I32, logits.shape, 1)
    valid = lane < N_EXPERTS
    pos = lane % EXP_PER_GROUP
    grp = (lane // EXP_PER_GROUP).astype(F32)
    s = jax.nn.sigmoid(logits)
    sel = jnp.where(valid, s + bias, NEG)
    rank = jnp.zeros(logits.shape, F32)
    for d in range(1, EXP_PER_GROUP):
        lo = pltpu.roll(sel, d, 1)
        hi = pltpu.roll(sel, LANES - d, 1)
        rank = rank + jnp.where(jnp.logical_and(pos >= d, lo >= sel), 1.0, 0.0)
        rank = rank + jnp.where(jnp.logical_and(pos + d < EXP_PER_GROUP, hi > sel), 1.0, 0.0)
    top2 = jnp.logical_and(rank < TOP_K_EXP, valid)
    contrib = jnp.where(top2, sel, 0.0)
    gs = contrib
    for d in range(1, EXP_PER_GROUP):
        lo = pltpu.roll(contrib, d, 1)
        hi = pltpu.roll(contrib, LANES - d, 1)
        gs = gs + jnp.where(pos >= d, lo, 0.0) + jnp.where(pos + d < EXP_PER_GROUP, hi, 0.0)
    gs = jnp.where(valid, gs, NEG)
    gmax = jnp.max(gs, axis=1, keepdims=True)
    best = jnp.min(jnp.where(gs == gmax, grp, 1e9), axis=1, keepdims=True)
    chosen = jnp.logical_and(top2, grp == best)
    s_sel = jnp.where(chosen, s, 0.0)
    return s_sel / jnp.sum(s_sel, axis=1, keepdims=True)


def _moe_kernel(x_ref, rw_ref, rb_ref, wg_ref, wu_ref, wd_ref, g_ref, b_ref, o_ref,
                xb_scr, comb_scr, acc_scr, *, sub):
    e = pl.program_id(1)
    tm = x_ref.shape[0]

    @pl.when(e == 0)
    def _():
        xb_scr[...] = x_ref[...].astype(xb_scr.dtype)
        logits = jnp.dot(x_ref[...], rw_ref[...], precision=lax.Precision.HIGHEST, preferred_element_type=F32)
        comb_scr[...] = _route(logits, rb_ref[...])
        acc_scr[...] = jnp.zeros_like(acc_scr)

    def rows(r, carry):
        r0 = pl.multiple_of(r * sub, sub)
        xb = xb_scr[pl.ds(r0, sub), :]
        gate = _dot(xb, wg_ref[...])
        h = gate * jax.nn.sigmoid(gate) * _dot(xb, wu_ref[...])
        y = _dot(h.astype(wd_ref.dtype), wd_ref[...])
        comb = comb_scr[pl.ds(r0, sub), :]
        lane = lax.broadcasted_iota(I32, comb.shape, 1)
        c_e = jnp.sum(jnp.where(lane == e, comb, 0.0), axis=1, keepdims=True)
        acc_scr[pl.ds(r0, sub), :] += c_e * y
        return carry

    lax.fori_loop(0, tm // sub, rows, 0)

    @pl.when(e == N_EXPERTS - 1)
    def _():
        o_ref[...] = _layer_norm(ALPHA * x_ref[...] + acc_scr[...], g_ref[...], b_ref[...])


def _moe_ln(x, rw, rb, wg, wu, wd, g, b):
    n, d = x.shape
    tm = _row_block(n, 1024)
    sub = _row_block(tm, 256)
    f = wg.shape[2]
    return pl.pallas_call(
        functools.partial(_moe_kernel, sub=sub),
        grid=(n // tm, N_EXPERTS),
        in_specs=[pl.BlockSpec((tm, d), lambda i, e: (i, 0)),
                  pl.BlockSpec((d, LANES), lambda i, e: (0, 0)),
                  pl.BlockSpec((1, LANES), lambda i, e: (0, 0)),
                  pl.BlockSpec((None, d, f), lambda i, e: (e, 0, 0)),
                  pl.BlockSpec((None, d, f), lambda i, e: (e, 0, 0)),
                  pl.BlockSpec((None, f, d), lambda i, e: (e, 0, 0)),
                  pl.BlockSpec((1, d), lambda i, e: (0, 0)),
                  pl.BlockSpec((1, d), lambda i, e: (0, 0))],
        out_specs=pl.BlockSpec((tm, d), lambda i, e: (i, 0)),
        out_shape=jax.ShapeDtypeStruct((n, d), F32),
        scratch_shapes=[pltpu.VMEM((tm, d), wg.dtype), pltpu.VMEM((tm, LANES), F32), pltpu.VMEM((tm, d), F32)],
        compiler_params=_cparams("parallel", "arbitrary"),
        name="moe_ln",
    )(x, rw, rb, wg, wu, wd, g, b)


def _route_t(logits, bias):
    s = jax.nn.sigmoid(logits)
    sel = s + bias
    top2, gscore = [], []
    for g in range(N_GROUPS):
        v = [sel[g * EXP_PER_GROUP + i:g * EXP_PER_GROUP + i + 1] for i in range(EXP_PER_GROUP)]
        flags = []
        for i in range(EXP_PER_GROUP):
            rank = jnp.zeros(v[i].shape, F32)
            for j in range(EXP_PER_GROUP):
                if j < i:
                    rank = rank + jnp.where(v[j] >= v[i], 1.0, 0.0)
                elif j > i:
                    rank = rank + jnp.where(v[j] > v[i], 1.0, 0.0)
            flags.append(rank < TOP_K_EXP)
        gs = jnp.where(flags[0], v[0], 0.0)
        for i in range(1, EXP_PER_GROUP):
            gs = gs + jnp.where(flags[i], v[i], 0.0)
        top2.append(flags)
        gscore.append(gs)
    gmax = gscore[0]
    for g in range(1, N_GROUPS):
        gmax = jnp.maximum(gmax, gscore[g])
    chosen, taken = [], None
    for g in range(N_GROUPS):
        c = gscore[g] == gmax
        if taken is not None:
            c = jnp.logical_and(c, jnp.logical_not(taken))
        taken = c if taken is None else jnp.logical_or(taken, c)
        chosen.append(c)
    gates = []
    for g in range(N_GROUPS):
        for i in range(EXP_PER_GROUP):
            e = g * EXP_PER_GROUP + i
            gates.append(jnp.where(jnp.logical_and(chosen[g], top2[g][i]), s[e:e + 1], 0.0))
    den = gates[0]
    for gt in gates[1:]:
        den = den + gt
    return [gt / den for gt in gates], chosen


MOE_TILE = 128
MOE_TAIL = 32

R_CHOSEN = EXP_PER_GROUP
R_RANK = EXP_PER_GROUP + 1


def _moe_group_kernel(x_ref, rwt_ref, rbt_ref, tri_ref, wg_ref, wu_ref, wd_ref, g_ref, b_ref, o_ref,
                      xb_scr, info_scr, acc_scr):
    g = pl.program_id(1)
    tb = x_ref.shape[0]
    mdt = xb_scr.dtype

    @pl.when(g == 0)
    def _():
        xb_scr[...] = x_ref[...].astype(mdt)
        logits = lax.dot_general(rwt_ref[...], x_ref[...], (((1,), (1,)), ((), ())),
                                 precision=lax.Precision.HIGHEST, preferred_element_type=F32)
        comb, chosen = _route_t(logits[0:N_EXPERTS], rbt_ref[0:N_EXPERTS, 0:1])
        flags = [jnp.where(c, 1.0, 0.0) for c in chosen]
        pad = jnp.zeros((SUBLANES - N_GROUPS, tb), F32)
        prefix = _dot(jnp.concatenate(flags + [pad], axis=0).astype(BF16), tri_ref[...])
        fill = jnp.zeros((SUBLANES - EXP_PER_GROUP - 2, tb), F32)
        for gg in range(N_GROUPS):
            rows = comb[gg * EXP_PER_GROUP:(gg + 1) * EXP_PER_GROUP] + [flags[gg], prefix[gg:gg + 1] - 1.0, fill]
            info_scr[gg] = jnp.concatenate(rows, axis=0)
        acc_scr[...] = jnp.zeros_like(acc_scr)

    info = info_scr[g]
    chosen_row = info[R_CHOSEN:R_CHOSEN + 1] > 0.5
    rank_row = info[R_RANK:R_RANK + 1]
    count = jnp.max(jnp.where(chosen_row, rank_row + 1.0, 0.0)).astype(I32)

    def tile(first, rows):
        rid = (lax.broadcasted_iota(I32, (rows, tb), 0) + first).astype(F32)
        onehot = jnp.where(jnp.logical_and(rank_row == rid, chosen_row), 1.0, 0.0)
        pb = onehot.astype(mdt)
        xg = _dot(pb, xb_scr[...]).astype(mdt)
        out = jnp.zeros((rows, x_ref.shape[1]), F32)
        for i in range(EXP_PER_GROUP):
            gate = jnp.sum(onehot * info[i:i + 1], axis=1, keepdims=True)
            gt = _dot(xg, wg_ref[i])
            h = gt * jax.nn.sigmoid(gt) * _dot(xg, wu_ref[i])
            out = out + gate * _dot(h.astype(mdt), wd_ref[i])
        hi = out.astype(mdt)
        lo = (out - hi.astype(F32)).astype(mdt)
        acc_scr[...] += _dot_tn(pb, hi) + _dot_tn(pb, lo)

    def full_tile(s, carry):
        tile(s * MOE_TILE, MOE_TILE)
        return carry

    nfull = count // MOE_TILE
    rem = count - nfull * MOE_TILE
    lax.fori_loop(0, nfull, full_tile, 0)

    @pl.when(rem > MOE_TAIL)
    def _():
        tile(nfull * MOE_TILE, MOE_TILE)

    @pl.when(jnp.logical_and(rem > 0, rem <= MOE_TAIL))
    def _():
        tile(nfull * MOE_TILE, MOE_TAIL)

    @pl.when(g == N_GROUPS - 1)
    def _():
        o_ref[...] = _layer_norm(ALPHA * x_ref[...] + acc_scr[...], g_ref[...], b_ref[...])


def _moe_ln_grouped(x, rwt, rbt, wg, wu, wd, g, b):
    n, d = x.shape
    tb = _row_block(n, 512)
    f = wg.shape[2]
    tri = jnp.triu(jnp.ones((tb, tb), BF16))
    return pl.pallas_call(
        _moe_group_kernel,
        grid=(n // tb, N_GROUPS),
        in_specs=[pl.BlockSpec((tb, d), lambda i, q: (i, 0)),
                  pl.BlockSpec((LANES, d), lambda i, q: (0, 0)),
                  pl.BlockSpec((LANES, LANES), lambda i, q: (0, 0)),
                  pl.BlockSpec((tb, tb), lambda i, q: (0, 0)),
                  pl.BlockSpec((EXP_PER_GROUP, d, f), lambda i, q: (q, 0, 0)),
                  pl.BlockSpec((EXP_PER_GROUP, d, f), lambda i, q: (q, 0, 0)),
                  pl.BlockSpec((EXP_PER_GROUP, f, d), lambda i, q: (q, 0, 0)),
                  pl.BlockSpec((1, d), lambda i, q: (0, 0)),
                  pl.BlockSpec((1, d), lambda i, q: (0, 0))],
        out_specs=pl.BlockSpec((tb, d), lambda i, q: (i, 0)),
        out_shape=jax.ShapeDtypeStruct((n, d), F32),
        scratch_shapes=[pltpu.VMEM((tb, d), wg.dtype), pltpu.VMEM((N_GROUPS, SUBLANES, tb), F32),
                        pltpu.VMEM((tb, d), F32)],
        compiler_params=_cparams("parallel", "arbitrary"),
        name="moe_group_ln",
    )(x, rwt, rbt, tri, wg, wu, wd, g, b)


def _mlstm_kernel(a_ref, s_ref, g_ref, c0_ref, n0_ref, m0_ref, y_ref, c1_ref, n1_ref, m1_ref,
                  c_scr, n_scr, m_scr):
    c = pl.program_id(1)
    L = a_ref.shape[0]
    mdt = y_ref.dtype

    @pl.when(c == 0)
    def _():
        c_scr[...] = c0_ref[...]
        n_scr[...] = n0_ref[...]
        m_scr[...] = m0_ref[...]

    S = s_ref[...]
    lane = lax.broadcasted_iota(I32, S.shape, 1)
    is_f = jnp.logical_and(lane >= S_FG, lane < S_FG + H_A)
    lf = jnp.where(is_f, jax.nn.log_sigmoid(S), 0.0)
    row = lax.broadcasted_iota(I32, (L, L), 0)
    col = lax.broadcasted_iota(I32, (L, L), 1)
    causal = row >= col
    Fs = jnp.dot(causal.astype(F32), lf, precision=lax.Precision.HIGHEST, preferred_element_type=F32)
    Fa = pltpu.roll(Fs, LANES - (S_FG - S_IG), 1)
    AT = jnp.transpose(S - Fa)

    for h in range(H_A):
        ig = S[:, S_IG + h:S_IG + h + 1]
        F = Fa[:, S_IG + h:S_IG + h + 1]
        a_row = AT[S_IG + h:S_IG + h + 1, :]
        m_prev = m_scr[h:h + 1, 0:1]
        cm = jnp.max(jnp.where(causal, a_row, NEG), axis=1, keepdims=True)
        m_t = F + jnp.maximum(m_prev, cm)
        dmat = jnp.exp(jnp.where(causal, (F - m_t) + a_row, NEG))
        inter = jnp.exp(F + m_prev - m_t)
        q = a_ref[:, h * DK_A:(h + 1) * DK_A].astype(mdt)
        kf = a_ref[:, W_A + h * DK_A:W_A + (h + 1) * DK_A]
        k = kf.astype(mdt)
        v = a_ref[:, 2 * W_A + h * DV_A:2 * W_A + (h + 1) * DV_A].astype(mdt)
        C = c_scr[h]
        n = n_scr[h:h + 1, :]
        s = _dot_nt(q, k) * dmat
        num = _dot(s.astype(mdt), v) + inter * _dot(q, C.astype(mdt))
        qn = jnp.sum(q.astype(F32) * n.astype(mdt).astype(F32), axis=1, keepdims=True)
        den = jnp.sum(s, axis=1, keepdims=True) + inter * qn
        hout = num / jnp.maximum(jnp.abs(den), jnp.exp(-m_t))
        o = a_ref[:, 3 * W_A + h * DV_A:3 * W_A + (h + 1) * DV_A]
        y = _head_norm(hout) * g_ref[:, h * DV_A:(h + 1) * DV_A] * jax.nn.sigmoid(o)
        y_ref[:, h * DV_A:(h + 1) * DV_A] = y.astype(y_ref.dtype)
        m_new = m_t[L - 1:L, :]
        F_last = F[L - 1:L, :]
        w_s = jnp.exp(F_last - F + ig - m_new)
        decay = jnp.exp(F_last + m_prev - m_new)
        kw = kf * w_s
        c_scr[h] = decay * C + _dot_tn(kw.astype(mdt), v)
        n_scr[h:h + 1, :] = decay * n + jnp.sum(kw, axis=0, keepdims=True)
        m_scr[h:h + 1, :] = jnp.broadcast_to(m_new, (1, LANES))

    @pl.when(c == pl.num_programs(1) - 1)
    def _():
        c1_ref[...] = c_scr[...]
        n1_ref[...] = n_scr[...]
        m1_ref[...] = m_scr[...]


def _mlstm(a, s, gnorm, c0, n0, m0b, nb, nc, odt):
    n = a.shape[0]
    L = CHUNK
    return pl.pallas_call(
        _mlstm_kernel,
        grid=(nb, nc),
        in_specs=[pl.BlockSpec((L, 4 * W_A), lambda b, c: (b * nc + c, 0)),
                  pl.BlockSpec((L, LANES), lambda b, c: (b * nc + c, 0)),
                  pl.BlockSpec((1, W_A), lambda b, c: (0, 0)),
                  pl.BlockSpec((None, H_A, DK_A, DV_A), lambda b, c: (b, 0, 0, 0)),
                  pl.BlockSpec((None, H_A, DK_A), lambda b, c: (b, 0, 0)),
                  pl.BlockSpec((None, H_A, LANES), lambda b, c: (b, 0, 0))],
        out_specs=[pl.BlockSpec((L, W_A), lambda b, c: (b * nc + c, 0)),
                   pl.BlockSpec((None, H_A, DK_A, DV_A), lambda b, c: (b, 0, 0, 0)),
                   pl.BlockSpec((None, H_A, DK_A), lambda b, c: (b, 0, 0)),
                   pl.BlockSpec((None, H_A, LANES), lambda b, c: (b, 0, 0))],
        out_shape=[jax.ShapeDtypeStruct((n, W_A), odt),
                   jax.ShapeDtypeStruct((nb, H_A, DK_A, DV_A), F32),
                   jax.ShapeDtypeStruct((nb, H_A, DK_A), F32),
                   jax.ShapeDtypeStruct((nb, H_A, LANES), F32)],
        scratch_shapes=[pltpu.VMEM((H_A, DK_A, DV_A), F32), pltpu.VMEM((H_A, DK_A), F32),
                        pltpu.VMEM((H_A, LANES), F32)],
        compiler_params=_cparams("parallel", "arbitrary"),
        name="mlstm",
    )(a, s, gnorm, c0, n0, m0b)


def _retention_kernel(d_ref, cos_ref, sin_ref, dm_ref, dec_ref, g_ref, s0_ref, y_ref, s1_ref, s_scr):
    c = pl.program_id(1)

    @pl.when(c == 0)
    def _():
        s_scr[...] = s0_ref[...]

    cos2 = cos_ref[...]
    sin2 = sin_ref[...]
    mdt = y_ref.dtype
    for h in range(H_D):
        qf = d_ref[:, h * DK_D:(h + 1) * DK_D]
        kf = d_ref[:, W_D + h * DK_D:W_D + (h + 1) * DK_D]
        q = qf * cos2 + pltpu.roll(qf, DK_D // 2, 1) * sin2
        k = (kf * cos2 + pltpu.roll(kf, DK_D // 2, 1) * sin2) * DK_D ** -0.5
        v = d_ref[:, 2 * W_D + h * DV_D:2 * W_D + (h + 1) * DV_D].astype(mdt)
        qdec = dec_ref[h, :, 0:1]
        kdec = dec_ref[h, :, 1:2]
        sdec = dec_ref[h, 0:1, 2:3]
        S = s_scr[h]
        qb = q.astype(mdt)
        att = _dot_nt(qb, k.astype(mdt)) * dm_ref[h]
        o = _dot(att.astype(mdt), v) + qdec * _dot(qb, S.astype(mdt))
        s_scr[h] = sdec * S + _dot_tn((k * kdec).astype(mdt), v)
        gt = d_ref[:, 3 * W_D + h * DV_D:3 * W_D + (h + 1) * DV_D]
        y = _head_norm(o) * g_ref[:, h * DV_D:(h + 1) * DV_D] * (gt * jax.nn.sigmoid(gt))
        y_ref[:, h * DV_D:(h + 1) * DV_D] = y.astype(y_ref.dtype)

    @pl.when(c == pl.num_programs(1) - 1)
    def _():
        s1_ref[...] = s_scr[...]


def _retention(d, cos2, sin2, dmat, dec, gnorm, s0, nb, nc, odt):
    n = d.shape[0]
    L = CHUNK
    return pl.pallas_call(
        _retention_kernel,
        grid=(nb, nc),
        in_specs=[pl.BlockSpec((L, 4 * W_D), lambda b, c: (b * nc + c, 0)),
                  pl.BlockSpec((L, DK_D), lambda b, c: (c, 0)),
                  pl.BlockSpec((L, DK_D), lambda b, c: (c, 0)),
                  pl.BlockSpec((H_D, L, L), lambda b, c: (0, 0, 0)),
                  pl.BlockSpec((H_D, L, LANES), lambda b, c: (0, 0, 0)),
                  pl.BlockSpec((1, W_D), lambda b, c: (0, 0)),
                  pl.BlockSpec((None, H_D, DK_D, DV_D), lambda b, c: (b, 0, 0, 0))],
        out_specs=[pl.BlockSpec((L, W_D), lambda b, c: (b * nc + c, 0)),
                   pl.BlockSpec((None, H_D, DK_D, DV_D), lambda b, c: (b, 0, 0, 0))],
        out_shape=[jax.ShapeDtypeStruct((n, W_D), odt),
                   jax.ShapeDtypeStruct((nb, H_D, DK_D, DV_D), F32)],
        scratch_shapes=[pltpu.VMEM((H_D, DK_D, DV_D), F32)],
        compiler_params=_cparams("parallel", "arbitrary"),
        name="retention",
    )(d, cos2, sin2, dmat, dec, gnorm, s0)


def _shift_rows(x, prev, j):
    tb = x.shape[0]
    xs = pltpu.roll(x, j, 0)
    pr = pltpu.roll(prev, j, 0)
    row = lax.broadcasted_iota(I32, pr.shape, 0)
    first = jnp.where(row < j, pr, xs[0:SUBLANES])
    if tb == SUBLANES:
        return first
    return jnp.concatenate([first, xs[SUBLANES:]], axis=0)


def _rglru_kernel(c_ref, cw_ref, cb_ref, wa_ref, ba_ref, wx_ref, bx_ref, lam_ref, conv0_ref, h0_ref,
                  y_ref, hl_ref, prev_scr, h_scr):
    t = pl.program_id(1)
    tb = c_ref.shape[0]

    @pl.when(t == 0)
    def _():
        prev_scr[...] = conv0_ref[...]
        h_scr[...] = h0_ref[...]

    x = c_ref[:, 0:W_C]
    gate = c_ref[:, W_C:2 * W_C]
    prev = prev_scr[...]
    xc = x * cw_ref[CONV_W - 1:CONV_W, :] + cb_ref[...]
    for j in range(1, CONV_W):
        xc = xc + _shift_rows(x, prev, j) * cw_ref[CONV_W - 1 - j:CONV_W - j, :]
    prev_scr[...] = x[tb - SUBLANES:tb]

    xb = xc.astype(wa_ref.dtype)
    r = jax.nn.sigmoid(_dot(xb, wa_ref[...]) + ba_ref[...])
    i = jax.nn.sigmoid(_dot(xb, wx_ref[...]) + bx_ref[...])
    log_a = -LRU_C * r * jax.nn.softplus(-lam_ref[...])
    A = jnp.exp(log_a)
    th = jnp.tanh(log_a)
    U = jnp.sqrt(-2.0 * th / (1.0 - th)) * (i * xc)
    row = lax.broadcasted_iota(I32, (tb, W_C), 0)
    d = 1
    while d < tb:
        keep = row >= d
        U = jnp.where(keep, U + A * pltpu.roll(U, d, 0), U)
        A = jnp.where(keep, A * pltpu.roll(A, d, 0), A)
        d *= 2
    h = U + A * h_scr[0:1, :]
    h_scr[...] = jnp.broadcast_to(h[tb - 1:tb, :], (SUBLANES, W_C))
    y_ref[...] = (h * jax.nn.gelu(gate)).astype(y_ref.dtype)

    @pl.when(t == pl.num_programs(1) - 1)
    def _():
        hl_ref[...] = h_scr[...]


def _rglru(c, cw, cb, wa, ba, wx, bx, lam, conv0, h0, nb, nt, tb, odt):
    n = c.shape[0]
    vec = pl.BlockSpec((1, W_C), lambda b, t: (0, 0))
    mat = pl.BlockSpec((W_C, W_C), lambda b, t: (0, 0))
    st = pl.BlockSpec((None, SUBLANES, W_C), lambda b, t: (b, 0, 0))
    return pl.pallas_call(
        _rglru_kernel,
        grid=(nb, nt),
        in_specs=[pl.BlockSpec((tb, 2 * W_C), lambda b, t: (b * nt + t, 0)),
                  pl.BlockSpec((CONV_W, W_C), lambda b, t: (0, 0)), vec, mat, vec, mat, vec, vec, st, st],
        out_specs=[pl.BlockSpec((tb, W_C), lambda b, t: (b * nt + t, 0)), st],
        out_shape=[jax.ShapeDtypeStruct((n, W_C), odt), jax.ShapeDtypeStruct((nb, SUBLANES, W_C), F32)],
        scratch_shapes=[pltpu.VMEM((SUBLANES, W_C), F32), pltpu.VMEM((SUBLANES, W_C), F32)],
        compiler_params=_cparams("parallel", "arbitrary"),
        name="rglru",
    )(c, cw, cb, wa, ba, wx, bx, lam, conv0, h0)


def _sort_key(score):
    score = jnp.where(score == 0.0, 0.0, score)
    bits = lax.bitcast_convert_type(score, I32)
    return bits ^ ((bits >> 31) & 0x7FFFFFFF)


def _kth_largest_key(count_ge, k, rows):
    kf = float(k)
    zero = jnp.zeros((rows, 1), I32)
    t0 = jnp.where(count_ge(zero) >= kf, zero, jnp.full((rows, 1), INT_MIN, I32))

    def body(bi, t):
        cand = t + jnp.left_shift(jnp.int32(1), 30 - bi)
        return jnp.where(count_ge(cand) >= kf, cand, t)

    return lax.fori_loop(0, 31, body, t0)


def _first_positions(count_eq_before, need, nbits, rows):
    def body(bi, x):
        cand = x + jnp.left_shift(jnp.int32(1), nbits - 1 - bi)
        return jnp.where(count_eq_before(cand) < need, cand, x)

    return lax.fori_loop(0, nbits, body, jnp.zeros((rows, 1), I32))


def _qi_heads(qi):
    lane = lax.broadcasted_iota(I32, (qi.shape[0], LANES), 1)
    low = lane < D_IDX
    out = []
    for h in range(H_IDX):
        pair = qi[:, (h // 2) * LANES:(h // 2 + 1) * LANES]
        if h % 2:
            pair = pltpu.roll(pair, D_IDX, 1)
        out.append(jnp.where(low, pair, 0.0).astype(BF16))
    return out


def _dsa_prompt_kernel(q_ref, s_ref, ki_ref, kb_ref, vb_ref, o_ref, key_scr, bias_scr, x_scr, lg_scr,
                       *, q_off, kt_size, topk):
    i = pl.program_id(1)
    QB = q_ref.shape[0]
    KT = kt_size
    nkt = key_scr.shape[0]
    lk = nkt * KT
    q0 = q_off + i * QB
    qih = _qi_heads(q_ref[:, W_B:2 * W_B])
    wi = s_ref[:, S_WI:S_WI + H_IDX]
    wcols = [wi[:, h:h + 1] for h in range(H_IDX)]
    rowpos = q0 + lax.broadcasted_iota(I32, (QB, KT), 0)
    colpos = lax.broadcasted_iota(I32, (QB, KT), 1)

    def score_tile(kt, carry):
        k0 = pl.multiple_of(kt * KT, KT)
        ki = ki_ref[pl.ds(k0, KT), :].astype(BF16)
        sc = jnp.zeros((QB, KT), F32)
        for h in range(H_IDX):
            sc = sc + wcols[h] * jnp.maximum(_dot_nt(qih[h], ki), 0.0)
        sc = jnp.where(colpos + k0 <= rowpos, sc, -jnp.inf)
        key_scr[kt] = _sort_key(sc)
        return carry

    lax.fori_loop(0, nkt, score_tile, 0, unroll=2)

    lanepos = lax.broadcasted_iota(I32, (QB, LANES), 1)

    def count(pred):
        def body(kt, acc):
            tile = key_scr[kt]
            k0 = kt * KT
            for j in range(KT // LANES):
                sl = slice(j * LANES, (j + 1) * LANES)
                acc = acc + jnp.where(pred(tile[:, sl], lanepos + (k0 + j * LANES)), 1.0, 0.0)
            return acc
        acc = lax.fori_loop(0, nkt, body, jnp.zeros((QB, LANES), F32))
        return jnp.sum(acc, axis=1, keepdims=True)

    thr = _kth_largest_key(lambda t: count(lambda key, pos: key >= t), topk, QB)
    cnt_ge = count(lambda key, pos: key >= thr)
    cnt_gt = count(lambda key, pos: key > thr)
    x_scr[...] = jnp.full(x_scr.shape, lk, I32)

    @pl.when(jnp.max(cnt_ge) > float(topk))
    def _():
        need = float(topk) - cnt_gt
        nbits = max(1, (lk - 1).bit_length())
        x = _first_positions(
            lambda c: count(lambda key, pos: jnp.logical_and(key == thr, pos < c)), need, nbits, QB)
        x_scr[...] = jnp.broadcast_to(x, x_scr.shape)

    xlim = x_scr[:, 0:1]

    def bias_tile(kt, carry):
        key = key_scr[kt]
        pos = colpos + kt * KT
        sel = jnp.logical_or(key > thr, jnp.logical_and(key == thr, pos <= xlim))
        bias_scr[kt] = jnp.where(jnp.logical_and(sel, pos <= rowpos), 0.0, NEG)
        return carry

    lax.fori_loop(0, nkt, bias_tile, 0)

    low = lanepos < DH_B
    ngrp = KT // LANES

    def group_fold(op, acc, x):
        for g in range(ngrp):
            acc = op(acc, x[:, g * LANES:(g + 1) * LANES])
        return acc

    for j in range(H_B // 2):
        qpair = q_ref[:, j * LANES:(j + 1) * LANES] * DH_B ** -0.5
        q0m = jnp.where(low, qpair, 0.0).astype(BF16)
        q1m = jnp.where(low, 0.0, qpair).astype(BF16)

        def pass1(kt, carry, q0m=q0m, q1m=q1m, j=j):
            mx0, mx1 = carry
            k0 = pl.multiple_of(kt * KT, KT)
            kk = kb_ref[pl.ds(k0, KT), j * LANES:(j + 1) * LANES]
            bias = bias_scr[kt]
            lg0 = _dot_nt(q0m, kk) + bias
            lg1 = _dot_nt(q1m, kk) + bias
            lg_scr[0, kt] = lg0
            lg_scr[1, kt] = lg1
            return group_fold(jnp.maximum, mx0, lg0), group_fold(jnp.maximum, mx1, lg1)

        neg = jnp.full((QB, LANES), NEG, F32)
        mx0, mx1 = lax.fori_loop(0, nkt, pass1, (neg, neg), unroll=2)
        m0 = jnp.max(mx0, axis=1, keepdims=True)
        m1 = jnp.max(mx1, axis=1, keepdims=True)

        def pass2(kt, carry, m0=m0, m1=m1, j=j):
            l0, l1, a0, a1 = carry
            k0 = pl.multiple_of(kt * KT, KT)
            vv = vb_ref[pl.ds(k0, KT), j * LANES:(j + 1) * LANES]
            p0 = jnp.exp(lg_scr[0, kt] - m0)
            p1 = jnp.exp(lg_scr[1, kt] - m1)
            a0 = a0 + _dot(p0.astype(BF16), vv)
            a1 = a1 + _dot(p1.astype(BF16), vv)
            return group_fold(jnp.add, l0, p0), group_fold(jnp.add, l1, p1), a0, a1

        zero = jnp.zeros((QB, LANES), F32)
        l0, l1, a0, a1 = lax.fori_loop(0, nkt, pass2, (zero, zero, zero, zero), unroll=2)
        out0 = a0 / jnp.sum(l0, axis=1, keepdims=True)
        out1 = a1 / jnp.sum(l1, axis=1, keepdims=True)
        o_ref[:, j * LANES:(j + 1) * LANES] = jnp.where(low, out0, out1).astype(o_ref.dtype)


def _dsa_prompt(bq, s, kb, vb, nb, t):
    n = bq.shape[0]
    QB = math.gcd(t, CHUNK)
    topk = min(TOPK_MAX, t // 4)
    nseg = 1
    for cand in (8, 4, 2):
        if t % cand == 0 and (t // cand) % 512 == 0:
            nseg = cand
            break
    seg = t // nseg
    KT = 512 if seg % 512 == 0 else seg
    s3 = s.reshape(nb, t, LANES)
    kb3 = kb.reshape(nb, t, W_B)
    vb3 = vb.reshape(nb, t, W_B)
    nqb = seg // QB
    outs = []
    for g in range(nseg):
        lk = (g + 1) * seg
        nkt = lk // KT
        row_blk = functools.partial(lambda b, i, g: (b * (t // QB) + g * nqb + i, 0), g=g)
        outs.append(pl.pallas_call(
            functools.partial(_dsa_prompt_kernel, q_off=g * seg, kt_size=KT, topk=topk),
            grid=(nb, nqb),
            in_specs=[pl.BlockSpec((QB, 2 * W_B), row_blk),
                      pl.BlockSpec((QB, LANES), row_blk),
                      pl.BlockSpec((None, lk, LANES), lambda b, i: (b, 0, 0)),
                      pl.BlockSpec((None, lk, W_B), lambda b, i: (b, 0, 0)),
                      pl.BlockSpec((None, lk, W_B), lambda b, i: (b, 0, 0))],
            out_specs=pl.BlockSpec((None, QB, W_B), lambda b, i: (b, i, 0)),
            out_shape=jax.ShapeDtypeStruct((nb, seg, W_B), BF16),
            scratch_shapes=[pltpu.VMEM((nkt, QB, KT), I32), pltpu.VMEM((nkt, QB, KT), F32),
                            pltpu.VMEM((QB, LANES), I32), pltpu.VMEM((2, nkt, QB, KT), F32)],
            compiler_params=_cparams("parallel", "arbitrary"),
            name=f"dsa_prompt_{g}",
        )(bq, s, s3, kb3, vb3))
    return jnp.concatenate(outs, axis=1).reshape(n, W_B)


SCORE_PAGES = 16
ATTN_PAGES = 8


def _dsa_sample_score_kernel(pt_ref, q_ref, s_ref, snew_ref, *rest):
    page_refs, o_ref = rest[:-1], rest[-1]
    p = pl.program_id(1)
    last = pl.num_programs(1) - 1
    T = q_ref.shape[0]
    qi = q_ref[:, W_B:2 * W_B]
    wi = s_ref[:, S_WI:S_WI + H_IDX]
    qst = jnp.concatenate([qi[:, h * D_IDX:(h + 1) * D_IDX] for h in range(H_IDX)], axis=0)
    wcol = jnp.concatenate([wi[:, h:h + 1] for h in range(H_IDX)], axis=0)

    def scores(dots):
        r = wcol * jnp.maximum(dots, 0.0)
        sc = r[0:T]
        for h in range(1, H_IDX):
            sc = sc + r[h * T:(h + 1) * T]
        return sc

    @pl.when(p < last)
    def _():
        for j, page_ref in enumerate(page_refs):
            o_ref[j] = scores(_dot(qst, page_ref[...]))

    @pl.when(p == last)
    def _():
        sc = scores(_dot_nt(qst, snew_ref[:, S_KI:S_KI + D_IDX]))
        r = lax.broadcasted_iota(I32, sc.shape, 0)
        c = lax.broadcasted_iota(I32, sc.shape, 1)
        o_ref[0] = jnp.where(c <= r, sc, -jnp.inf)
        for j in range(1, len(page_refs)):
            o_ref[j] = jnp.full(sc.shape, -jnp.inf, F32)


def _dsa_sample_attn_kernel(pt_ref, q_ref, sc_ref, knew_ref, vnew_ref, *rest, topk, npg, G):
    kpages, vpages, o_ref = rest[:G], rest[G:2 * G], rest[2 * G]
    thr_scr, x_scr, m_scr, l_scr, acc_scr = rest[2 * G + 1:]
    p = pl.program_id(1)
    last = pl.num_programs(1) - 1
    T = q_ref.shape[0]
    R = H_B * T
    ltot = sc_ref.shape[0] * PAGE_SIZE

    @pl.when(p == 0)
    def _():
        key = _sort_key(sc_ref[...])
        pos = (lax.broadcasted_iota(I32, key.shape, 0) * PAGE_SIZE
               + lax.broadcasted_iota(I32, key.shape, 2))

        def count(pred):
            per_lane = jnp.sum(jnp.where(pred(key, pos), 1.0, 0.0), axis=0)
            return jnp.sum(per_lane, axis=1, keepdims=True)

        thr = _kth_largest_key(lambda t: count(lambda k_, p_: k_ >= t), topk, T)
        cnt_ge = count(lambda k_, p_: k_ >= thr)
        thr_scr[...] = jnp.broadcast_to(thr, thr_scr.shape)
        x_scr[...] = jnp.full(x_scr.shape, ltot, I32)

        @pl.when(jnp.max(cnt_ge) > float(topk))
        def _():
            cnt_gt = count(lambda k_, p_: k_ > thr)
            nbits = max(1, (ltot - 1).bit_length())
            x = _first_positions(lambda c: count(lambda k_, p_: jnp.logical_and(k_ == thr, p_ < c)),
                                 float(topk) - cnt_gt, nbits, T)
            x_scr[...] = jnp.broadcast_to(x, x_scr.shape)

        m_scr[...] = jnp.full(m_scr.shape, NEG, F32)
        l_scr[...] = jnp.zeros_like(l_scr)
        acc_scr[...] = jnp.zeros_like(acc_scr)

    q = q_ref[:, 0:W_B] * DH_B ** -0.5
    qrep = jnp.concatenate([q] * H_B, axis=0)
    rr = lax.broadcasted_iota(I32, (R, W_B), 0) // T
    cc = lax.broadcasted_iota(I32, (R, W_B), 1) // DH_B
    diag = rr == cc
    qbd = jnp.where(diag, qrep, 0.0)
    thr = thr_scr[:, 0:1]
    xlim = x_scr[:, 0:1]

    def selected(page):
        key = _sort_key(sc_ref[page])
        pos = page * PAGE_SIZE + lax.broadcasted_iota(I32, key.shape, 1)
        return jnp.logical_or(key > thr, jnp.logical_and(key == thr, pos <= xlim))

    def masked(logits, valid):
        bias = jnp.where(valid, 0.0, NEG)
        return logits + jnp.concatenate([bias] * H_B, axis=0)

    def update(lgs, pv):
        mx = lgs[0]
        for lg in lgs[1:]:
            mx = jnp.maximum(mx, lg)
        m = m_scr[:, 0:1]
        m_new = jnp.maximum(m, jnp.max(mx, axis=1, keepdims=True))
        alpha = jnp.exp(m - m_new)
        ps = [jnp.exp(lg - m_new) for lg in lgs]
        psum = ps[0]
        for pr in ps[1:]:
            psum = psum + pr
        acc = pv(0, ps[0])
        for j in range(1, len(ps)):
            acc = acc + pv(j, ps[j])
        l_scr[...] = jnp.broadcast_to(alpha * l_scr[:, 0:1] + jnp.sum(psum, axis=1, keepdims=True), l_scr.shape)
        acc_scr[...] = alpha * acc_scr[...] + acc
        m_scr[...] = jnp.broadcast_to(m_new, m_scr.shape)

    q2 = _split2(qbd)

    @pl.when(p < last)
    def _():
        lgs = [masked(_dot3(q2, _split2(kpages[j][...]), _dot), selected(p * G + j)) for j in range(G)]
        update(lgs, lambda j, pr: _dot3(_split2(pr), _split2(vpages[j][...]), _dot_nt))

    @pl.when(p == last)
    def _():
        sel = selected(npg)
        r = lax.broadcasted_iota(I32, sel.shape, 0)
        c = lax.broadcasted_iota(I32, sel.shape, 1)
        lg = masked(_dot3(q2, _split2(knew_ref[...]), _dot_nt), jnp.logical_and(sel, c <= r))
        update([lg], lambda j, pr: _dot3(_split2(pr), _split2(vnew_ref[...]), _dot))
        out = jnp.where(diag, acc_scr[...] / l_scr[:, 0:1], 0.0)
        res = out[0:T]
        for h in range(1, H_B):
            res = res + out[h * T:(h + 1) * T]
        o_ref[...] = res.astype(o_ref.dtype)


def _dsa_sample(bq, s, k, v, pool_k, pool_v, pool_ki, page_table, nb, t):
    n_pool = pool_k.shape[1]
    pool_k = jnp.transpose(pool_k, (0, 1, 3, 4, 2)).reshape(1, n_pool, W_B, PAGE_SIZE)
    pool_v = jnp.transpose(pool_v, (0, 1, 3, 4, 2)).reshape(1, n_pool, W_B, PAGE_SIZE)
    pool_ki = jnp.transpose(pool_ki, (0, 1, 3, 2))
    npg = page_table.shape[1]
    past = npg * PAGE_SIZE
    topk = min(TOPK_MAX, (past + t) // 4)
    gs, ga = math.gcd(npg, SCORE_PAGES), math.gcd(npg, ATTN_PAGES)
    pad = PAGE_SIZE - t
    s_new = jnp.pad(s.reshape(nb, t, LANES), ((0, 0), (0, pad), (0, 0)))
    k_new = jnp.pad(k.reshape(nb, t, W_B), ((0, 0), (0, pad), (0, 0)))
    v_new = jnp.pad(v.reshape(nb, t, W_B), ((0, 0), (0, pad), (0, 0)))
    lastp = npg - 1
    ptot = npg + gs

    def page_spec(tail, j, group):
        zeros = (0,) * len(tail)
        return pl.BlockSpec((None, None) + tail,
                            lambda b, p, pt: (0, pt[b, jnp.minimum(p * group + j, lastp)]) + zeros)

    scores = pl.pallas_call(
        _dsa_sample_score_kernel,
        grid_spec=pltpu.PrefetchScalarGridSpec(
            num_scalar_prefetch=1,
            grid=(nb, npg // gs + 1),
            in_specs=[pl.BlockSpec((t, 2 * W_B), lambda b, p, pt: (b, 0)),
                      pl.BlockSpec((t, LANES), lambda b, p, pt: (b, 0)),
                      pl.BlockSpec((None, PAGE_SIZE, LANES), lambda b, p, pt: (b, 0, 0))]
            + [page_spec((D_IDX, PAGE_SIZE), j, gs) for j in range(gs)],
            out_specs=pl.BlockSpec((None, gs, t, PAGE_SIZE), lambda b, p, pt: (b, p, 0, 0))),
        out_shape=jax.ShapeDtypeStruct((nb, ptot, t, PAGE_SIZE), F32),
        compiler_params=_cparams("parallel", "arbitrary"),
        name="dsa_sample_scores",
    )(page_table, bq, s, s_new, *([pool_ki] * gs))

    kv_tail = (W_B, PAGE_SIZE)
    return pl.pallas_call(
        functools.partial(_dsa_sample_attn_kernel, topk=topk, npg=npg, G=ga),
        grid_spec=pltpu.PrefetchScalarGridSpec(
            num_scalar_prefetch=1,
            grid=(nb, npg // ga + 1),
            in_specs=[pl.BlockSpec((t, 2 * W_B), lambda b, p, pt: (b, 0)),
                      pl.BlockSpec((None, ptot, t, PAGE_SIZE), lambda b, p, pt: (b, 0, 0, 0)),
                      pl.BlockSpec((None, PAGE_SIZE, W_B), lambda b, p, pt: (b, 0, 0)),
                      pl.BlockSpec((None, PAGE_SIZE, W_B), lambda b, p, pt: (b, 0, 0))]
            + [page_spec(kv_tail, j, ga) for j in range(ga)]
            + [page_spec(kv_tail, j, ga) for j in range(ga)],
            out_specs=pl.BlockSpec((t, W_B), lambda b, p, pt: (b, 0)),
            scratch_shapes=[pltpu.VMEM((t, LANES), I32), pltpu.VMEM((t, LANES), I32),
                            pltpu.VMEM((H_B * t, LANES), F32), pltpu.VMEM((H_B * t, LANES), F32),
                            pltpu.VMEM((H_B * t, W_B), F32)]),
        out_shape=jax.ShapeDtypeStruct((nb * t, W_B), F32),
        compiler_params=_cparams("parallel", "arbitrary"),
        name="dsa_sample_attn",
    )(page_table, bq, scores, k_new, v_new, *([pool_k] * ga), *([pool_v] * ga))


def _moe(x, prm, layer):
    wg, wu, wd = prm["moe_wg"][layer], prm["moe_wu"][layer], prm["moe_wd"][layer]
    g, b = prm["ln_g"][layer, 1], prm["ln_b"][layer, 1]
    if wg.dtype == BF16:
        return _moe_ln_grouped(x, prm["router_wt"], prm["router_bt"], wg, wu, wd, g, b)
    return _moe_ln(x, prm["router_w"], prm["router_b"], wg, wu, wd, g, b)


def _pad_chunks(a, nb, t, fill=None):
    c = a.shape[1]
    a3 = a.reshape(nb, t, c)
    if fill is None:
        a3 = jnp.pad(a3, ((0, 0), (0, CHUNK - t), (0, 0)))
    else:
        a3 = jnp.concatenate([a3, jnp.broadcast_to(fill, (nb, CHUNK - t, c))], axis=1)
    return a3.reshape(nb * CHUNK, c)


def _retention_tables(t_true, pos0, t_pad):
    L = CHUNK
    lt = min(L, t_true)
    lg = jnp.log1p(-jnp.exp2(-5.0 - jnp.arange(H_D, dtype=F32)))
    j = jnp.arange(L, dtype=F32)
    causal = jnp.tril(jnp.ones((L, L), dtype=bool))
    dmat = jnp.exp(jnp.where(causal, (j[:, None] - j[None, :]) * lg[:, None, None], -jnp.inf))
    qdec = jnp.exp((j + 1.0) * lg[:, None])
    kdec = jnp.where(j < lt, jnp.exp((lt - 1.0 - j) * lg[:, None]), 0.0)
    sdec = jnp.broadcast_to(jnp.exp(lt * lg)[:, None], (H_D, L))
    dec = jnp.zeros((H_D, L, LANES), F32)
    dec = dec.at[:, :, 0].set(qdec).at[:, :, 1].set(kdec).at[:, :, 2].set(sdec)
    half = DK_D // 2
    freq = ROPE_BASE ** (-jnp.arange(half, dtype=F32) / half)
    pos = (pos0 + jnp.arange(t_pad)).astype(F32)
    ang = pos[:, None] * freq[None, :]
    cos, sin = jnp.cos(ang), jnp.sin(ang)
    return dmat, dec, jnp.concatenate([cos, cos], -1), jnp.concatenate([-sin, sin], -1)


def _forward(x3, st, prm):
    nb, t, d = x3.shape
    n = nb * t
    x = x3.reshape(n, d)
    short = t < CHUNK
    t_pad = CHUNK if short else t
    nc = t_pad // CHUNK
    new = {}
    odt = prm["w_in_e"].dtype

    a, bq, k, v, s, kb, vb = _inproj_e(x, prm["w_in_e"], prm["bias_e"])
    if st is None:
        c0 = jnp.zeros((nb, H_A, DK_A, DV_A), F32)
        n0 = jnp.zeros((nb, H_A, DK_A), F32)
        m0 = jnp.zeros((nb, H_A), F32)
    else:
        c0, n0, m0 = st["mlstm_C"][0], st["mlstm_n"][0], st["mlstm_m"][0]
    m0b = jnp.broadcast_to(m0[:, :, None], (nb, H_A, LANES))
    if short:
        lane = jnp.arange(LANES)
        fill = jnp.where((lane >= S_IG) & (lane < S_IG + H_A), NEG,
                         jnp.where((lane >= S_FG) & (lane < S_FG + H_A), 1e4, 0.0)).astype(F32)
        a_m, s_m = _pad_chunks(a, nb, t), _pad_chunks(s, nb, t, fill)
    else:
        a_m, s_m = a, s
    ya, c1, n1, m1 = _mlstm(a_m, s_m, prm["a_norm_g"], c0, n0, m0b, nb, nc, odt)
    if short:
        ya = ya.reshape(nb, CHUNK, W_A)[:, :t].reshape(n, W_A)
    if st is None:
        yb = _dsa_prompt(bq, s, kb, vb, nb, t)
    else:
        yb = _dsa_sample(bq, s, k, v, st["pool_k"], st["pool_v"], st["pool_ki"], st["page_table"], nb, t)
    new["mlstm_C"], new["mlstm_n"], new["mlstm_m"] = c1[None], n1[None], m1[None, :, :, 0]
    new["k"] = k.reshape(1, nb, t, H_B, DH_B)
    new["v"] = v.reshape(1, nb, t, H_B, DH_B)
    new["kidx"] = s[:, S_KI:S_KI + D_IDX].reshape(1, nb, t, D_IDX)
    x = _outproj_ln(ya, yb, prm["w_out_e"], x, prm["ln_g"][0, 0], prm["ln_b"][0, 0])
    x = _moe(x, prm, 0)

    c, dd = _inproj_o(x, prm["w_in_o"])
    if st is None:
        conv0 = jnp.zeros((nb, SUBLANES, W_C), F32)
        h0 = jnp.zeros((nb, SUBLANES, W_C), F32)
        s0 = jnp.zeros((nb, H_D, DK_D, DV_D), F32)
        pos0 = 0
    else:
        conv0 = jnp.pad(st["conv"][0], ((0, 0), (SUBLANES - (CONV_W - 1), 0), (0, 0)))
        h0 = jnp.broadcast_to(st["lru_h"][0][:, None, :], (nb, SUBLANES, W_C))
        s0 = st["ret_S"][0]
        pos0 = st["page_table"].shape[1] * PAGE_SIZE
    tb = _row_block(t, 256)
    yc, hl = _rglru(c, prm["c_conv_w"], prm["c_conv_b"], prm["c_wa"], prm["c_ba"], prm["c_wx"], prm["c_bx"],
                    prm["c_lambda"], conv0, h0, nb, t // tb, tb, odt)
    dmat, dec, cos2, sin2 = _retention_tables(t, pos0, t_pad)
    d_m = _pad_chunks(dd, nb, t) if short else dd
    yd, s1 = _retention(d_m, cos2, sin2, dmat, dec, prm["d_norm_g"], s0, nb, nc, odt)
    if short:
        yd = yd.reshape(nb, CHUNK, W_D)[:, :t].reshape(n, W_D)
    new["conv"] = c[:, 0:W_C].reshape(nb, t, W_C)[None, :, t - (CONV_W - 1):]
    new["lru_h"] = hl[None, :, 0]
    new["ret_S"] = s1[None]
    x = _outproj_ln(yc, yd, prm["w_out_o"], x, prm["ln_g"][1, 0], prm["ln_b"][1, 0])
    x = _moe(x, prm, 1)
    return x.reshape(nb, t, d), new


def _block_diag(w):
    nblk, blk, _ = w.shape
    eye = jnp.eye(nblk, dtype=w.dtype)
    return (eye[:, None, :, None] * w[:, :, None, :]).reshape(nblk * blk, nblk * blk)


def _prepare_params(w_in_e, b_if_e, a_norm_g, w_out_e, w_in_o, c_conv_w, c_conv_b, c_wa, c_ba, c_wx, c_bx,
                    c_lambda, d_norm_g, w_out_o, router_w, router_bias, moe_w_gate, moe_w_up, moe_w_down,
                    ln_g, ln_b, wdt):
    d = w_in_e.shape[1]
    o = [0]
    for width in (W_A, W_A, W_A, W_A, 2 * H_A, W_B, W_B, W_B, H_IDX * D_IDX, D_IDX, H_IDX):
        o.append(o[-1] + width)
    we = w_in_e[0]
    col = lambda i: we[:, o[i]:o[i + 1]]
    slab = jnp.zeros((d, LANES), F32)
    slab = slab.at[:, S_KI:S_KI + D_IDX].set(col(9)).at[:, S_IG:S_IG + 2 * H_A].set(col(4))
    slab = slab.at[:, S_WI:S_WI + H_IDX].set(col(10))
    w_e = jnp.concatenate([col(0), col(1), col(2), col(3), col(5), col(8), col(6), col(7), slab], axis=1)
    bias_e = jnp.zeros((1, LANES), F32).at[0, S_IG:S_IG + 2 * H_A].set(b_if_e[0])
    rw = jnp.zeros((d, LANES), F32).at[:, :N_EXPERTS].set(router_w)
    rb = jnp.zeros((1, LANES), F32).at[0, :N_EXPERTS].set(router_bias)
    row = lambda a: a.reshape(1, -1).astype(F32)
    return {
        "w_in_e": w_e.astype(wdt), "bias_e": bias_e, "a_norm_g": row(a_norm_g[0]),
        "w_out_e": w_out_e[0].astype(wdt), "w_in_o": w_in_o[0].astype(wdt),
        "c_conv_w": c_conv_w[0].astype(F32), "c_conv_b": row(c_conv_b[0]),
        "c_wa": _block_diag(c_wa[0]).astype(wdt), "c_ba": row(c_ba[0]),
        "c_wx": _block_diag(c_wx[0]).astype(wdt), "c_bx": row(c_bx[0]),
        "c_lambda": row(c_lambda[0]), "d_norm_g": row(d_norm_g[0]), "w_out_o": w_out_o[0].astype(wdt),
        "router_w": rw, "router_b": rb,
        "router_wt": rw.T, "router_bt": jnp.broadcast_to(rb.reshape(LANES, 1), (LANES, LANES)),
        "moe_wg": moe_w_gate.astype(wdt), "moe_wu": moe_w_up.astype(wdt), "moe_wd": moe_w_down.astype(wdt),
        "ln_g": ln_g.reshape(DEPTH, 2, 1, -1).astype(F32), "ln_b": ln_b.reshape(DEPTH, 2, 1, -1).astype(F32),
    }


def kernel(x_prompt, x_sample, state_mlstm_C, state_mlstm_n, state_mlstm_m, cache_k, cache_v, cache_kidx,
           page_table, state_conv, state_lru_h, state_ret_S, w_in_e, b_if_e, a_norm_g, w_out_e, w_in_o,
           c_conv_w, c_conv_b, c_wa, c_ba, c_wx, c_bx, c_lambda, d_norm_g, w_out_o, router_w, router_bias,
           moe_w_gate, moe_w_up, moe_w_down, ln_g, ln_b):
    weights = (w_in_e, b_if_e, a_norm_g, w_out_e, w_in_o, c_conv_w, c_conv_b, c_wa, c_ba, c_wx, c_bx, c_lambda,
               d_norm_g, w_out_o, router_w, router_bias, moe_w_gate, moe_w_up, moe_w_down, ln_g, ln_b)
    prm_prompt = _prepare_params(*weights, BF16)
    prm_sample = _prepare_params(*weights, F32)
    st = {"mlstm_C": state_mlstm_C, "mlstm_n": state_mlstm_n, "mlstm_m": state_mlstm_m,
          "pool_k": cache_k, "pool_v": cache_v, "pool_ki": cache_kidx, "page_table": page_table,
          "conv": state_conv, "lru_h": state_lru_h, "ret_S": state_ret_S}
    y_p, nsp = _forward(x_prompt, None, prm_prompt)
    y_s, nss = _forward(x_sample, st, prm_sample)
    names = ("mlstm_C", "mlstm_n", "mlstm_m", "k", "v", "kidx", "conv", "lru_h", "ret_S")
    return (y_p, y_s) + tuple(nsp[k] for k in names) + tuple(nss[k] for k in names)
```

```python
import functools
import math

import jax
import jax.numpy as jnp
from jax import lax
from jax.experimental import pallas as pl
from jax.experimental.pallas import tpu as pltpu

F32 = jnp.float32
BF16 = jnp.bfloat16
I32 = jnp.int32

DEPTH = 2
PAGE_SIZE = 128
H_A, DK_A, DV_A = 4, 128, 128
W_A = H_A * DV_A
H_B, DH_B = 8, 64
W_B = H_B * DH_B
H_IDX, D_IDX = 8, 64
TOPK_MAX = 256
W_C, N_BLK_C, CONV_W, LRU_C = 512, 8, 4, 8.0
BLK_C = W_C // N_BLK_C
H_D, DK_D, DV_D = 4, 128, 128
W_D = H_D * DV_D
ROPE_BASE = 10000.0
N_EXPERTS, N_GROUPS, TOP_K_EXP, D_FF_EXP = 16, 4, 2, 512
EXP_PER_GROUP = N_EXPERTS // N_GROUPS
ALPHA = (2 * DEPTH) ** 0.25
LN_EPS = 1e-5
HN_EPS = 1e-6

CHUNK = 128
DSA_QB = 256
LANES = 128
SUBLANES = 8
NEG = -1e30
INT_MIN = -2 ** 31
VMEM_LIMIT = 56 * 1024 * 1024

S_KI = 0
S_IG = 64
S_FG = 68
S_WI = 72


def _cparams(*sem):
    return pltpu.CompilerParams(dimension_semantics=sem, vmem_limit_bytes=VMEM_LIMIT)


def _precision(a, b):
    return lax.Precision.HIGHEST if a.dtype == F32 and b.dtype == F32 else None


def _dot(a, b):
    return jnp.dot(a, b, preferred_element_type=F32, precision=_precision(a, b))


def _dot_nt(a, b):
    return lax.dot_general(a, b, (((1,), (1,)), ((), ())), preferred_element_type=F32, precision=_precision(a, b))


def _dot_tn(a, b):
    return lax.dot_general(a, b, (((0,), (0,)), ((), ())), preferred_element_type=F32, precision=_precision(a, b))


def _split2(a):
    hi = a.astype(BF16)
    return hi, (a - hi.astype(F32)).astype(BF16)


def _dot3(a2, b2, dot):
    return dot(a2[0], b2[0]) + (dot(a2[0], b2[1]) + dot(a2[1], b2[0]))


def _row_block(n, target):
    t = min(n, target)
    while n % t:
        t //= 2
    return t


def _layer_norm(z, g, b):
    mu = jnp.mean(z, -1, keepdims=True)
    zc = z - mu
    var = jnp.mean(zc * zc, -1, keepdims=True)
    return zc * lax.rsqrt(var + LN_EPS) * g + b


def _head_norm(h):
    mu = jnp.mean(h, -1, keepdims=True)
    hc = h - mu
    var = jnp.mean(hc * hc, -1, keepdims=True)
    return hc * lax.rsqrt(var + HN_EPS)


def _inproj_e_kernel(x_ref, w_ref, bias_ref, a_ref, bq_ref, k_ref, v_ref, s_ref, kb_ref, vb_ref):
    x = x_ref[...].astype(w_ref.dtype)

    def mm(lo, hi):
        return _dot(x, w_ref[:, lo:hi])

    a_ref[:, 0:W_A] = mm(0, W_A)
    a_ref[:, W_A:2 * W_A] = mm(W_A, 2 * W_A) * DK_A ** -0.5
    a_ref[:, 2 * W_A:4 * W_A] = mm(2 * W_A, 4 * W_A)
    o = 4 * W_A
    bq_ref[...] = mm(o, o + 2 * W_B)
    o += 2 * W_B
    k = mm(o, o + W_B)
    k_ref[...] = k
    kb_ref[...] = k.astype(kb_ref.dtype)
    o += W_B
    v = mm(o, o + W_B)
    v_ref[...] = v
    vb_ref[...] = v.astype(vb_ref.dtype)
    o += W_B
    s_ref[...] = mm(o, o + LANES) + bias_ref[...]


def _inproj_e(x, w, bias):
    n, d = x.shape
    tm = _row_block(n, 256)
    wcols = w.shape[1]
    outs = [(4 * W_A, F32), (2 * W_B, F32), (W_B, F32), (W_B, F32), (LANES, F32), (W_B, w.dtype), (W_B, w.dtype)]
    return pl.pallas_call(
        _inproj_e_kernel,
        grid=(n // tm,),
        in_specs=[pl.BlockSpec((tm, d), lambda i: (i, 0)),
                  pl.BlockSpec((d, wcols), lambda i: (0, 0)),
                  pl.BlockSpec((1, LANES), lambda i: (0, 0))],
        out_specs=[pl.BlockSpec((tm, c), lambda i: (i, 0)) for c, _ in outs],
        out_shape=[jax.ShapeDtypeStruct((n, c), dt) for c, dt in outs],
        compiler_params=_cparams("parallel"),
        name="inproj_even",
    )(x, w, bias)


def _inproj_o_kernel(x_ref, w_ref, c_ref, d_ref):
    x = x_ref[...].astype(w_ref.dtype)
    c_ref[...] = _dot(x, w_ref[:, 0:2 * W_C])
    d_ref[...] = _dot(x, w_ref[:, 2 * W_C:2 * W_C + 4 * W_D])


def _inproj_o(x, w):
    n, d = x.shape
    tm = _row_block(n, 256)
    return pl.pallas_call(
        _inproj_o_kernel,
        grid=(n // tm,),
        in_specs=[pl.BlockSpec((tm, d), lambda i: (i, 0)),
                  pl.BlockSpec(w.shape, lambda i: (0, 0))],
        out_specs=[pl.BlockSpec((tm, 2 * W_C), lambda i: (i, 0)),
                   pl.BlockSpec((tm, 4 * W_D), lambda i: (i, 0))],
        out_shape=[jax.ShapeDtypeStruct((n, 2 * W_C), F32), jax.ShapeDtypeStruct((n, 4 * W_D), F32)],
        compiler_params=_cparams("parallel"),
        name="inproj_odd",
    )(x, w)


def _outproj_ln_kernel(y1_ref, y2_ref, w_ref, x_ref, g_ref, b_ref, o_ref):
    half = y1_ref.shape[1]
    y = _dot(y1_ref[...], w_ref[0:half, :]) + _dot(y2_ref[...], w_ref[half:2 * half, :])
    o_ref[...] = _layer_norm(ALPHA * x_ref[...] + y, g_ref[...], b_ref[...])


def _outproj_ln(y1, y2, w, x, g, b):
    n, d = x.shape
    tm = _row_block(n, 512)
    half = y1.shape[1]
    return pl.pallas_call(
        _outproj_ln_kernel,
        grid=(n // tm,),
        in_specs=[pl.BlockSpec((tm, half), lambda i: (i, 0)),
                  pl.BlockSpec((tm, half), lambda i: (i, 0)),
                  pl.BlockSpec(w.shape, lambda i: (0, 0)),
                  pl.BlockSpec((tm, d), lambda i: (i, 0)),
                  pl.BlockSpec((1, d), lambda i: (0, 0)),
                  pl.BlockSpec((1, d), lambda i: (0, 0))],
        out_specs=pl.BlockSpec((tm, d), lambda i: (i, 0)),
        out_shape=jax.ShapeDtypeStruct((n, d), F32),
        compiler_params=_cparams("parallel"),
        name="outproj_ln",
    )(y1, y2, w, x, g, b)


def _route(logits, bias):
    lane = lax.broadcasted_iota(I32, logits.shape, 1)
    valid = lane < N_EXPERTS
    pos = lane % EXP_PER_GROUP
    grp = (lane // EXP_PER_GROUP).astype(F32)
    s = jax.nn.sigmoid(logits)
    sel = jnp.where(valid, s + bias, NEG)
    rank = jnp.zeros(logits.shape, F32)
    for d in range(1, EXP_PER_GROUP):
        lo = pltpu.roll(sel, d, 1)
        hi = pltpu.roll(sel, LANES - d, 1)
        rank = rank + jnp.where(jnp.logical_and(pos >= d, lo >= sel), 1.0, 0.0)
        rank = rank + jnp.where(jnp.logical_and(pos + d < EXP_PER_GROUP, hi > sel), 1.0, 0.0)
    top2 = jnp.logical_and(rank < TOP_K_EXP, valid)
    contrib = jnp.where(top2, sel, 0.0)
    gs = contrib
    for d in range(1, EXP_PER_GROUP):
        lo = pltpu.roll(contrib, d, 1)
        hi = pltpu.roll(contrib, LANES - d, 1)
        gs = gs + jnp.where(pos >= d, lo, 0.0) + jnp.where(pos + d < EXP_PER_GROUP, hi, 0.0)
    gs = jnp.where(valid, gs, NEG)
    gmax = jnp.max(gs, axis=1, keepdims=True)
    best = jnp.min(jnp.where(gs == gmax, grp, 1e9), axis=1, keepdims=True)
    chosen = jnp.logical_and(top2, grp == best)
    s_sel = jnp.where(chosen, s, 0.0)
    return s_sel / jnp.sum(s_sel, axis=1, keepdims=True)


def _moe_kernel(x_ref, rw_ref, rb_ref, wg_ref, wu_ref, wd_ref, g_ref, b_ref, o_ref,
                xb_scr, comb_scr, acc_scr, *, sub):
    e = pl.program_id(1)
    tm = x_ref.shape[0]

    @pl.when(e == 0)
    def _():
        xb_scr[...] = x_ref[...].astype(xb_scr.dtype)
        logits = jnp.dot(x_ref[...], rw_ref[...], precision=lax.Precision.HIGHEST, preferred_element_type=F32)
        comb_scr[...] = _route(logits, rb_ref[...])
        acc_scr[...] = jnp.zeros_like(acc_scr)

    def rows(r, carry):
        r0 = pl.multiple_of(r * sub, sub)
        xb = xb_scr[pl.ds(r0, sub), :]
        gate = _dot(xb, wg_ref[...])
        h = gate * jax.nn.sigmoid(gate) * _dot(xb, wu_ref[...])
        y = _dot(h.astype(wd_ref.dtype), wd_ref[...])
        comb = comb_scr[pl.ds(r0, sub), :]
        lane = lax.broadcasted_iota(I32, comb.shape, 1)
        c_e = jnp.sum(jnp.where(lane == e, comb, 0.0), axis=1, keepdims=True)
        acc_scr[pl.ds(r0, sub), :] += c_e * y
        return carry

    lax.fori_loop(0, tm // sub, rows, 0)

    @pl.when(e == N_EXPERTS - 1)
    def _():
        o_ref[...] = _layer_norm(ALPHA * x_ref[...] + acc_scr[...], g_ref[...], b_ref[...])


def _moe_ln(x, rw, rb, wg, wu, wd, g, b):
    n, d = x.shape
    tm = _row_block(n, 1024)
    sub = _row_block(tm, 256)
    f = wg.shape[2]
    return pl.pallas_call(
        functools.partial(_moe_kernel, sub=sub),
        grid=(n // tm, N_EXPERTS),
        in_specs=[pl.BlockSpec((tm, d), lambda i, e: (i, 0)),
                  pl.BlockSpec((d, LANES), lambda i, e: (0, 0)),
                  pl.BlockSpec((1, LANES), lambda i, e: (0, 0)),
                  pl.BlockSpec((None, d, f), lambda i, e: (e, 0, 0)),
                  pl.BlockSpec((None, d, f), lambda i, e: (e, 0, 0)),
                  pl.BlockSpec((None, f, d), lambda i, e: (e, 0, 0)),
                  pl.BlockSpec((1, d), lambda i, e: (0, 0)),
                  pl.BlockSpec((1, d), lambda i, e: (0, 0))],
        out_specs=pl.BlockSpec((tm, d), lambda i, e: (i, 0)),
        out_shape=jax.ShapeDtypeStruct((n, d), F32),
        scratch_shapes=[pltpu.VMEM((tm, d), wg.dtype), pltpu.VMEM((tm, LANES), F32), pltpu.VMEM((tm, d), F32)],
        compiler_params=_cparams("parallel", "arbitrary"),
        name="moe_ln",
    )(x, rw, rb, wg, wu, wd, g, b)


def _route_t(logits, bias):
    s = jax.nn.sigmoid(logits)
    sel = s + bias
    top2, gscore = [], []
    for g in range(N_GROUPS):
        v = [sel[g * EXP_PER_GROUP + i:g * EXP_PER_GROUP + i + 1] for i in range(EXP_PER_GROUP)]
        flags = []
        for i in range(EXP_PER_GROUP):
            rank = jnp.zeros(v[i].shape, F32)
            for j in range(EXP_PER_GROUP):
                if j < i:
                    rank = rank + jnp.where(v[j] >= v[i], 1.0, 0.0)
                elif j > i:
                    rank = rank + jnp.where(v[j] > v[i], 1.0, 0.0)
            flags.append(rank < TOP_K_EXP)
        gs = jnp.where(flags[0], v[0], 0.0)
        for i in range(1, EXP_PER_GROUP):
            gs = gs + jnp.where(flags[i], v[i], 0.0)
        top2.append(flags)
        gscore.append(gs)
    gmax = gscore[0]
    for g in range(1, N_GROUPS):
        gmax = jnp.maximum(gmax, gscore[g])
    chosen, taken = [], None
    for g in range(N_GROUPS):
        c = gscore[g] == gmax
        if taken is not None:
            c = jnp.logical_and(c, jnp.logical_not(taken))
        taken = c if taken is None else jnp.logical_or(taken, c)
        chosen.append(c)
    gates = []
    for g in range(N_GROUPS):
        for i in range(EXP_PER_GROUP):
            e = g * EXP_PER_GROUP + i
            gates.append(jnp.where(jnp.logical_and(chosen[g], top2[g][i]), s[e:e + 1], 0.0))
    den = gates[0]
    for gt in gates[1:]:
        den = den + gt
    return [gt / den for gt in gates], chosen


MOE_BLOCK = 1024
MOE_TILE = 256
MOE_TAIL = 64

R_CHOSEN = EXP_PER_GROUP
R_RANK = EXP_PER_GROUP + 1


def _moe_group_kernel(x_ref, rwt_ref, rbt_ref, tri_ref, wg_ref, wu_ref, wd_ref, g_ref, b_ref, o_ref,
                      xb_scr, info_scr, acc_scr):
    g = pl.program_id(1)
    tb = x_ref.shape[0]
    mdt = xb_scr.dtype

    @pl.when(g == 0)
    def _():
        xb_scr[...] = x_ref[...].astype(mdt)
        logits = lax.dot_general(rwt_ref[...], x_ref[...], (((1,), (1,)), ((), ())),
                                 precision=lax.Precision.HIGHEST, preferred_element_type=F32)
        comb, chosen = _route_t(logits, rbt_ref[0:N_EXPERTS, 0:1])
        flags = [jnp.where(c, 1.0, 0.0) for c in chosen]
        pad = jnp.zeros((SUBLANES - N_GROUPS, tb), F32)
        prefix = _dot(jnp.concatenate(flags + [pad], axis=0).astype(BF16), tri_ref[...])
        fill = jnp.zeros((SUBLANES - EXP_PER_GROUP - 2, tb), F32)
        for gg in range(N_GROUPS):
            rows = comb[gg * EXP_PER_GROUP:(gg + 1) * EXP_PER_GROUP] + [flags[gg], prefix[gg:gg + 1] - 1.0, fill]
            info_scr[gg] = jnp.concatenate(rows, axis=0)
        acc_scr[...] = jnp.zeros_like(acc_scr)

    info = info_scr[g]
    chosen_row = info[R_CHOSEN:R_CHOSEN + 1] > 0.5
    rank_row = info[R_RANK:R_RANK + 1]
    count = jnp.max(jnp.where(chosen_row, rank_row + 1.0, 0.0)).astype(I32)

    def tile(first, rows):
        rid = (lax.broadcasted_iota(I32, (rows, tb), 0) + first).astype(F32)
        onehot = jnp.where(jnp.logical_and(rank_row == rid, chosen_row), 1.0, 0.0)
        pb = onehot.astype(mdt)
        xg = _dot(pb, xb_scr[...]).astype(mdt)
        out = jnp.zeros((rows, x_ref.shape[1]), F32)
        for i in range(EXP_PER_GROUP):
            gate = jnp.sum(onehot * info[i:i + 1], axis=1, keepdims=True)
            gt = _dot(xg, wg_ref[i])
            h = gt * jax.nn.sigmoid(gt) * _dot(xg, wu_ref[i])
            out = out + gate * _dot(h.astype(mdt), wd_ref[i])
        hi = out.astype(mdt)
        lo = (out - hi.astype(F32)).astype(mdt)
        acc_scr[...] += _dot_tn(pb, hi) + _dot_tn(pb, lo)

    def full_tile(s, carry):
        tile(s * MOE_TILE, MOE_TILE)
        return carry

    nfull = count // MOE_TILE
    rem = count - nfull * MOE_TILE
    lax.fori_loop(0, nfull, full_tile, 0)

    @pl.when(rem > MOE_TAIL)
    def _():
        tile(nfull * MOE_TILE, MOE_TILE)

    @pl.when(jnp.logical_and(rem > 0, rem <= MOE_TAIL))
    def _():
        tile(nfull * MOE_TILE, MOE_TAIL)

    @pl.when(g == N_GROUPS - 1)
    def _():
        o_ref[...] = _layer_norm(ALPHA * x_ref[...] + acc_scr[...], g_ref[...], b_ref[...])


def _moe_ln_grouped(x, rwt, rbt, wg, wu, wd, g, b):
    n, d = x.shape
    tb = _row_block(n, MOE_BLOCK)
    f = wg.shape[2]
    tri = jnp.triu(jnp.ones((tb, tb), BF16))
    once = pl.Buffered(1)
    return pl.pallas_call(
        _moe_group_kernel,
        grid=(n // tb, N_GROUPS),
        in_specs=[pl.BlockSpec((tb, d), lambda i, q: (i, 0), pipeline_mode=once),
                  pl.BlockSpec((N_EXPERTS, d), lambda i, q: (0, 0), pipeline_mode=once),
                  pl.BlockSpec((LANES, LANES), lambda i, q: (0, 0), pipeline_mode=once),
                  pl.BlockSpec((tb, tb), lambda i, q: (0, 0), pipeline_mode=once),
                  pl.BlockSpec((EXP_PER_GROUP, d, f), lambda i, q: (q, 0, 0)),
                  pl.BlockSpec((EXP_PER_GROUP, d, f), lambda i, q: (q, 0, 0)),
                  pl.BlockSpec((EXP_PER_GROUP, f, d), lambda i, q: (q, 0, 0)),
                  pl.BlockSpec((1, d), lambda i, q: (0, 0)),
                  pl.BlockSpec((1, d), lambda i, q: (0, 0))],
        out_specs=pl.BlockSpec((tb, d), lambda i, q: (i, 0), pipeline_mode=once),
        out_shape=jax.ShapeDtypeStruct((n, d), F32),
        scratch_shapes=[pltpu.VMEM((tb, d), wg.dtype), pltpu.VMEM((N_GROUPS, SUBLANES, tb), F32),
                        pltpu.VMEM((tb, d), F32)],
        compiler_params=_cparams("parallel", "arbitrary"),
        name="moe_group_ln",
    )(x, rwt, rbt, tri, wg, wu, wd, g, b)


def _mlstm_kernel(a_ref, s_ref, g_ref, c0_ref, n0_ref, m0_ref, y_ref, c1_ref, n1_ref, m1_ref,
                  c_scr, n_scr, m_scr):
    c = pl.program_id(1)
    L = a_ref.shape[0]
    mdt = y_ref.dtype

    @pl.when(c == 0)
    def _():
        c_scr[...] = c0_ref[...]
        n_scr[...] = n0_ref[...]
        m_scr[...] = m0_ref[...]

    S = s_ref[...]
    lane = lax.broadcasted_iota(I32, S.shape, 1)
    is_f = jnp.logical_and(lane >= S_FG, lane < S_FG + H_A)
    lf = jnp.where(is_f, jax.nn.log_sigmoid(S), 0.0)
    row = lax.broadcasted_iota(I32, (L, L), 0)
    col = lax.broadcasted_iota(I32, (L, L), 1)
    causal = row >= col
    Fs = jnp.dot(causal.astype(F32), lf, precision=lax.Precision.HIGHEST, preferred_element_type=F32)
    Fa = pltpu.roll(Fs, LANES - (S_FG - S_IG), 1)
    AT = jnp.transpose(S - Fa)

    for h in range(H_A):
        ig = S[:, S_IG + h:S_IG + h + 1]
        F = Fa[:, S_IG + h:S_IG + h + 1]
        a_row = AT[S_IG + h:S_IG + h + 1, :]
        m_prev = m_scr[h:h + 1, 0:1]
        cm = jnp.max(jnp.where(causal, a_row, NEG), axis=1, keepdims=True)
        m_t = F + jnp.maximum(m_prev, cm)
        dmat = jnp.exp(jnp.where(causal, (F - m_t) + a_row, NEG))
        inter = jnp.exp(F + m_prev - m_t)
        q = a_ref[:, h * DK_A:(h + 1) * DK_A].astype(mdt)
        kf = a_ref[:, W_A + h * DK_A:W_A + (h + 1) * DK_A]
        k = kf.astype(mdt)
        v = a_ref[:, 2 * W_A + h * DV_A:2 * W_A + (h + 1) * DV_A].astype(mdt)
        C = c_scr[h]
        n = n_scr[h:h + 1, :]
        s = _dot_nt(q, k) * dmat
        num = _dot(s.astype(mdt), v) + inter * _dot(q, C.astype(mdt))
        qn = jnp.sum(q.astype(F32) * n.astype(mdt).astype(F32), axis=1, keepdims=True)
        den = jnp.sum(s, axis=1, keepdims=True) + inter * qn
        hout = num / jnp.maximum(jnp.abs(den), jnp.exp(-m_t))
        o = a_ref[:, 3 * W_A + h * DV_A:3 * W_A + (h + 1) * DV_A]
        y = _head_norm(hout) * g_ref[:, h * DV_A:(h + 1) * DV_A] * jax.nn.sigmoid(o)
        y_ref[:, h * DV_A:(h + 1) * DV_A] = y.astype(y_ref.dtype)
        m_new = m_t[L - 1:L, :]
        F_last = F[L - 1:L, :]
        w_s = jnp.exp(F_last - F + ig - m_new)
        decay = jnp.exp(F_last + m_prev - m_new)
        kw = kf * w_s
        c_scr[h] = decay * C + _dot_tn(kw.astype(mdt), v)
        n_scr[h:h + 1, :] = decay * n + jnp.sum(kw, axis=0, keepdims=True)
        m_scr[h:h + 1, :] = jnp.broadcast_to(m_new, (1, LANES))

    @pl.when(c == pl.num_programs(1) - 1)
    def _():
        c1_ref[...] = c_scr[...]
        n1_ref[...] = n_scr[...]
        m1_ref[...] = m_scr[...]


def _mlstm(a, s, gnorm, c0, n0, m0b, nb, nc, odt):
    n = a.shape[0]
    L = CHUNK
    return pl.pallas_call(
        _mlstm_kernel,
        grid=(nb, nc),
        in_specs=[pl.BlockSpec((L, 4 * W_A), lambda b, c: (b * nc + c, 0)),
                  pl.BlockSpec((L, LANES), lambda b, c: (b * nc + c, 0)),
                  pl.BlockSpec((1, W_A), lambda b, c: (0, 0)),
                  pl.BlockSpec((None, H_A, DK_A, DV_A), lambda b, c: (b, 0, 0, 0)),
                  pl.BlockSpec((None, H_A, DK_A), lambda b, c: (b, 0, 0)),
                  pl.BlockSpec((None, H_A, LANES), lambda b, c: (b, 0, 0))],
        out_specs=[pl.BlockSpec((L, W_A), lambda b, c: (b * nc + c, 0)),
                   pl.BlockSpec((None, H_A, DK_A, DV_A), lambda b, c: (b, 0, 0, 0)),
                   pl.BlockSpec((None, H_A, DK_A), lambda b, c: (b, 0, 0)),
                   pl.BlockSpec((None, H_A, LANES), lambda b, c: (b, 0, 0))],
        out_shape=[jax.ShapeDtypeStruct((n, W_A), odt),
                   jax.ShapeDtypeStruct((nb, H_A, DK_A, DV_A), F32),
                   jax.ShapeDtypeStruct((nb, H_A, DK_A), F32),
                   jax.ShapeDtypeStruct((nb, H_A, LANES), F32)],
        scratch_shapes=[pltpu.VMEM((H_A, DK_A, DV_A), F32), pltpu.VMEM((H_A, DK_A), F32),
                        pltpu.VMEM((H_A, LANES), F32)],
        compiler_params=_cparams("parallel", "arbitrary"),
        name="mlstm",
    )(a, s, gnorm, c0, n0, m0b)


def _retention_kernel(d_ref, cos_ref, sin_ref, dm_ref, dec_ref, g_ref, s0_ref, y_ref, s1_ref, s_scr):
    c = pl.program_id(1)

    @pl.when(c == 0)
    def _():
        s_scr[...] = s0_ref[...]

    cos2 = cos_ref[...]
    sin2 = sin_ref[...]
    mdt = y_ref.dtype
    for h in range(H_D):
        qf = d_ref[:, h * DK_D:(h + 1) * DK_D]
        kf = d_ref[:, W_D + h * DK_D:W_D + (h + 1) * DK_D]
        q = qf * cos2 + pltpu.roll(qf, DK_D // 2, 1) * sin2
        k = (kf * cos2 + pltpu.roll(kf, DK_D // 2, 1) * sin2) * DK_D ** -0.5
        v = d_ref[:, 2 * W_D + h * DV_D:2 * W_D + (h + 1) * DV_D].astype(mdt)
        qdec = dec_ref[h, :, 0:1]
        kdec = dec_ref[h, :, 1:2]
        sdec = dec_ref[h, 0:1, 2:3]
        S = s_scr[h]
        qb = q.astype(mdt)
        att = _dot_nt(qb, k.astype(mdt)) * dm_ref[h]
        o = _dot(att.astype(mdt), v) + qdec * _dot(qb, S.astype(mdt))
        s_scr[h] = sdec * S + _dot_tn((k * kdec).astype(mdt), v)
        gt = d_ref[:, 3 * W_D + h * DV_D:3 * W_D + (h + 1) * DV_D]
        y = _head_norm(o) * g_ref[:, h * DV_D:(h + 1) * DV_D] * (gt * jax.nn.sigmoid(gt))
        y_ref[:, h * DV_D:(h + 1) * DV_D] = y.astype(y_ref.dtype)

    @pl.when(c == pl.num_programs(1) - 1)
    def _():
        s1_ref[...] = s_scr[...]


def _retention(d, cos2, sin2, dmat, dec, gnorm, s0, nb, nc, odt):
    n = d.shape[0]
    L = CHUNK
    return pl.pallas_call(
        _retention_kernel,
        grid=(nb, nc),
        in_specs=[pl.BlockSpec((L, 4 * W_D), lambda b, c: (b * nc + c, 0)),
                  pl.BlockSpec((L, DK_D), lambda b, c: (c, 0)),
                  pl.BlockSpec((L, DK_D), lambda b, c: (c, 0)),
                  pl.BlockSpec((H_D, L, L), lambda b, c: (0, 0, 0)),
                  pl.BlockSpec((H_D, L, LANES), lambda b, c: (0, 0, 0)),
                  pl.BlockSpec((1, W_D), lambda b, c: (0, 0)),
                  pl.BlockSpec((None, H_D, DK_D, DV_D), lambda b, c: (b, 0, 0, 0))],
        out_specs=[pl.BlockSpec((L, W_D), lambda b, c: (b * nc + c, 0)),
                   pl.BlockSpec((None, H_D, DK_D, DV_D), lambda b, c: (b, 0, 0, 0))],
        out_shape=[jax.ShapeDtypeStruct((n, W_D), odt),
                   jax.ShapeDtypeStruct((nb, H_D, DK_D, DV_D), F32)],
        scratch_shapes=[pltpu.VMEM((H_D, DK_D, DV_D), F32)],
        compiler_params=_cparams("parallel", "arbitrary"),
        name="retention",
    )(d, cos2, sin2, dmat, dec, gnorm, s0)


def _shift_rows(x, prev, j):
    tb = x.shape[0]
    xs = pltpu.roll(x, j, 0)
    pr = pltpu.roll(prev, j, 0)
    row = lax.broadcasted_iota(I32, pr.shape, 0)
    first = jnp.where(row < j, pr, xs[0:SUBLANES])
    if tb == SUBLANES:
        return first
    return jnp.concatenate([first, xs[SUBLANES:]], axis=0)


def _rglru_kernel(c_ref, cw_ref, cb_ref, wa_ref, ba_ref, wx_ref, bx_ref, lam_ref, conv0_ref, h0_ref,
                  y_ref, hl_ref, prev_scr, h_scr):
    t = pl.program_id(1)
    tb = c_ref.shape[0]

    @pl.when(t == 0)
    def _():
        prev_scr[...] = conv0_ref[...]
        h_scr[...] = h0_ref[...]

    x = c_ref[:, 0:W_C]
    gate = c_ref[:, W_C:2 * W_C]
    prev = prev_scr[...]
    xc = x * cw_ref[CONV_W - 1:CONV_W, :] + cb_ref[...]
    for j in range(1, CONV_W):
        xc = xc + _shift_rows(x, prev, j) * cw_ref[CONV_W - 1 - j:CONV_W - j, :]
    prev_scr[...] = x[tb - SUBLANES:tb]

    xb = xc.astype(wa_ref.dtype)
    r = jax.nn.sigmoid(_dot(xb, wa_ref[...]) + ba_ref[...])
    i = jax.nn.sigmoid(_dot(xb, wx_ref[...]) + bx_ref[...])
    log_a = -LRU_C * r * jax.nn.softplus(-lam_ref[...])
    A = jnp.exp(log_a)
    th = jnp.tanh(log_a)
    U = jnp.sqrt(-2.0 * th / (1.0 - th)) * (i * xc)
    row = lax.broadcasted_iota(I32, (tb, W_C), 0)
    d = 1
    while d < tb:
        keep = row >= d
        U = jnp.where(keep, U + A * pltpu.roll(U, d, 0), U)
        A = jnp.where(keep, A * pltpu.roll(A, d, 0), A)
        d *= 2
    h = U + A * h_scr[0:1, :]
    h_scr[...] = jnp.broadcast_to(h[tb - 1:tb, :], (SUBLANES, W_C))
    y_ref[...] = (h * jax.nn.gelu(gate)).astype(y_ref.dtype)

    @pl.when(t == pl.num_programs(1) - 1)
    def _():
        hl_ref[...] = h_scr[...]


def _rglru(c, cw, cb, wa, ba, wx, bx, lam, conv0, h0, nb, nt, tb, odt):
    n = c.shape[0]
    vec = pl.BlockSpec((1, W_C), lambda b, t: (0, 0))
    mat = pl.BlockSpec((W_C, W_C), lambda b, t: (0, 0))
    st = pl.BlockSpec((None, SUBLANES, W_C), lambda b, t: (b, 0, 0))
    return pl.pallas_call(
        _rglru_kernel,
        grid=(nb, nt),
        in_specs=[pl.BlockSpec((tb, 2 * W_C), lambda b, t: (b * nt + t, 0)),
                  pl.BlockSpec((CONV_W, W_C), lambda b, t: (0, 0)), vec, mat, vec, mat, vec, vec, st, st],
        out_specs=[pl.BlockSpec((tb, W_C), lambda b, t: (b * nt + t, 0)), st],
        out_shape=[jax.ShapeDtypeStruct((n, W_C), odt), jax.ShapeDtypeStruct((nb, SUBLANES, W_C), F32)],
        scratch_shapes=[pltpu.VMEM((SUBLANES, W_C), F32), pltpu.VMEM((SUBLANES, W_C), F32)],
        compiler_params=_cparams("parallel", "arbitrary"),
        name="rglru",
    )(c, cw, cb, wa, ba, wx, bx, lam, conv0, h0)


def _sort_key(score):
    score = jnp.where(score == 0.0, 0.0, score)
    bits = lax.bitcast_convert_type(score, I32)
    return bits ^ ((bits >> 31) & 0x7FFFFFFF)


def _kth_largest_key(count_ge, k, rows):
    kf = float(k)
    zero = jnp.zeros((rows, 1), I32)
    t0 = jnp.where(count_ge(zero) >= kf, zero, jnp.full((rows, 1), INT_MIN, I32))

    def body(bi, t):
        cand = t + jnp.left_shift(jnp.int32(1), 30 - bi)
        return jnp.where(count_ge(cand) >= kf, cand, t)

    return lax.fori_loop(0, 31, body, t0)


def _first_positions(count_eq_before, need, nbits, rows):
    def body(bi, x):
        cand = x + jnp.left_shift(jnp.int32(1), nbits - 1 - bi)
        return jnp.where(count_eq_before(cand) < need, cand, x)

    return lax.fori_loop(0, nbits, body, jnp.zeros((rows, 1), I32))


def _qi_heads(qi):
    lane = lax.broadcasted_iota(I32, (qi.shape[0], LANES), 1)
    low = lane < D_IDX
    out = []
    for h in range(H_IDX):
        pair = qi[:, (h // 2) * LANES:(h // 2 + 1) * LANES]
        if h % 2:
            pair = pltpu.roll(pair, D_IDX, 1)
        out.append(jnp.where(low, pair, 0.0).astype(BF16))
    return out


def _dsa_prompt_kernel(q_ref, s_ref, ki_ref, kb_ref, vb_ref, o_ref, key_scr, bias_scr, x_scr, thr_scr, lg_scr,
                       *, q_off, kt_size, topk):
    i = pl.program_id(1)
    QB = q_ref.shape[0]
    KT = kt_size
    nkt = key_scr.shape[0]
    lk = nkt * KT
    q0 = q_off + i * QB
    qih = _qi_heads(q_ref[:, W_B:2 * W_B])
    wi = s_ref[:, S_WI:S_WI + H_IDX]
    wcols = [wi[:, h:h + 1] for h in range(H_IDX)]
    rowpos = q0 + lax.broadcasted_iota(I32, (QB, KT), 0)
    colpos = lax.broadcasted_iota(I32, (QB, KT), 1)

    def score_tile(kt, carry):
        k0 = pl.multiple_of(kt * KT, KT)
        ki = ki_ref[pl.ds(k0, KT), :].astype(BF16)
        sc = jnp.zeros((QB, KT), F32)
        for h in range(H_IDX):
            sc = sc + wcols[h] * jnp.maximum(_dot_nt(qih[h], ki), 0.0)
        sc = jnp.where(colpos + k0 <= rowpos, sc, -jnp.inf)
        key_scr[kt] = _sort_key(sc)
        return carry

    lax.fori_loop(0, nkt, score_tile, 0, unroll=2)

    lanepos = lax.broadcasted_iota(I32, (QB, LANES), 1)
    x_scr[...] = jnp.full(x_scr.shape, lk, I32)

    SR = min(QB, CHUNK)
    lanepos_s = lax.broadcasted_iota(I32, (SR, LANES), 1)
    for r0 in range(0, QB, SR):
        part = slice(r0, r0 + SR)

        def count(pred, part=part):
            def body(kt, acc):
                tile = key_scr[kt, part, :]
                k0 = kt * KT
                for j in range(KT // LANES):
                    sl = slice(j * LANES, (j + 1) * LANES)
                    acc = acc + jnp.where(pred(tile[:, sl], lanepos_s + (k0 + j * LANES)), 1.0, 0.0)
                return acc
            acc = lax.fori_loop(0, nkt, body, jnp.zeros((SR, LANES), F32))
            return jnp.sum(acc, axis=1, keepdims=True)

        thr_p = _kth_largest_key(lambda t: count(lambda key, pos: key >= t), topk, SR)
        thr_scr[part, :] = jnp.broadcast_to(thr_p, (SR, LANES))
        cnt_ge = count(lambda key, pos: key >= thr_p)

        @pl.when(jnp.max(cnt_ge) > float(topk))
        def _(count=count, thr_p=thr_p, part=part):
            need = float(topk) - count(lambda key, pos: key > thr_p)
            nbits = max(1, (lk - 1).bit_length())
            x = _first_positions(
                lambda c: count(lambda key, pos: jnp.logical_and(key == thr_p, pos < c)), need, nbits, SR)
            x_scr[part, :] = jnp.broadcast_to(x, (SR, LANES))

    thr = thr_scr[:, 0:1]
    xlim = x_scr[:, 0:1]

    def bias_tile(kt, carry):
        key = key_scr[kt]
        pos = colpos + kt * KT
        sel = jnp.logical_or(key > thr, jnp.logical_and(key == thr, pos <= xlim))
        bias_scr[kt] = jnp.where(jnp.logical_and(sel, pos <= rowpos), 0.0, NEG)
        return carry

    lax.fori_loop(0, nkt, bias_tile, 0)

    low = lanepos < DH_B
    ngrp = KT // LANES

    def group_fold(op, acc, x):
        for g in range(ngrp):
            acc = op(acc, x[:, g * LANES:(g + 1) * LANES])
        return acc

    for j in range(H_B // 2):
        qpair = q_ref[:, j * LANES:(j + 1) * LANES] * DH_B ** -0.5
        q0m = jnp.where(low, qpair, 0.0).astype(BF16)
        q1m = jnp.where(low, 0.0, qpair).astype(BF16)

        def pass1(kt, carry, q0m=q0m, q1m=q1m, j=j):
            mx0, mx1 = carry
            k0 = pl.multiple_of(kt * KT, KT)
            kk = kb_ref[pl.ds(k0, KT), j * LANES:(j + 1) * LANES]
            bias = bias_scr[kt]
            lg0 = _dot_nt(q0m, kk) + bias
            lg1 = _dot_nt(q1m, kk) + bias
            lg_scr[0, kt] = lg0
            lg_scr[1, kt] = lg1
            return group_fold(jnp.maximum, mx0, lg0), group_fold(jnp.maximum, mx1, lg1)

        neg = jnp.full((QB, LANES), NEG, F32)
        mx0, mx1 = lax.fori_loop(0, nkt, pass1, (neg, neg), unroll=2)
        m0 = jnp.max(mx0, axis=1, keepdims=True)
        m1 = jnp.max(mx1, axis=1, keepdims=True)

        def pass2(kt, carry, m0=m0, m1=m1, j=j):
            l0, l1, a0, a1 = carry
            k0 = pl.multiple_of(kt * KT, KT)
            vv = vb_ref[pl.ds(k0, KT), j * LANES:(j + 1) * LANES]
            p0 = jnp.exp(lg_scr[0, kt] - m0)
            p1 = jnp.exp(lg_scr[1, kt] - m1)
            a0 = a0 + _dot(p0.astype(BF16), vv)
            a1 = a1 + _dot(p1.astype(BF16), vv)
            return group_fold(jnp.add, l0, p0), group_fold(jnp.add, l1, p1), a0, a1

        zero = jnp.zeros((QB, LANES), F32)
        l0, l1, a0, a1 = lax.fori_loop(0, nkt, pass2, (zero, zero, zero, zero), unroll=2)
        out0 = a0 / jnp.sum(l0, axis=1, keepdims=True)
        out1 = a1 / jnp.sum(l1, axis=1, keepdims=True)
        o_ref[:, j * LANES:(j + 1) * LANES] = jnp.where(low, out0, out1).astype(o_ref.dtype)


def _dsa_prompt(bq, s, kb, vb, nb, t):
    n = bq.shape[0]
    QB = math.gcd(t, DSA_QB)
    topk = min(TOPK_MAX, t // 4)
    nseg = 1
    for cand in (8, 4, 2):
        if t % cand == 0 and (t // cand) % 512 == 0:
            nseg = cand
            break
    seg = t // nseg
    KT = 512 if seg % 512 == 0 else seg
    s3 = s.reshape(nb, t, LANES)
    kb3 = kb.reshape(nb, t, W_B)
    vb3 = vb.reshape(nb, t, W_B)
    nqb = seg // QB
    outs = []
    for g in range(nseg):
        lk = (g + 1) * seg
        nkt = lk // KT
        row_blk = functools.partial(lambda b, i, g: (b * (t // QB) + g * nqb + i, 0), g=g)
        outs.append(pl.pallas_call(
            functools.partial(_dsa_prompt_kernel, q_off=g * seg, kt_size=KT, topk=topk),
            grid=(nb, nqb),
            in_specs=[pl.BlockSpec((QB, 2 * W_B), row_blk),
                      pl.BlockSpec((QB, LANES), row_blk),
                      pl.BlockSpec((None, lk, LANES), lambda b, i: (b, 0, 0)),
                      pl.BlockSpec((None, lk, W_B), lambda b, i: (b, 0, 0)),
                      pl.BlockSpec((None, lk, W_B), lambda b, i: (b, 0, 0))],
            out_specs=pl.BlockSpec((None, QB, W_B), lambda b, i: (b, i, 0)),
            out_shape=jax.ShapeDtypeStruct((nb, seg, W_B), BF16),
            scratch_shapes=[pltpu.VMEM((nkt, QB, KT), I32), pltpu.VMEM((nkt, QB, KT), F32),
                            pltpu.VMEM((QB, LANES), I32), pltpu.VMEM((QB, LANES), I32),
                            pltpu.VMEM((2, nkt, QB, KT), F32)],
            compiler_params=_cparams("parallel", "arbitrary"),
            name=f"dsa_prompt_{g}",
        )(bq, s, s3, kb3, vb3))
    return jnp.concatenate(outs, axis=1).reshape(n, W_B)


SCORE_PAGES = 16
ATTN_PAGES = 8


def _dsa_sample_score_kernel(pt_ref, q_ref, s_ref, snew_ref, *rest):
    page_refs, o_ref = rest[:-1], rest[-1]
    p = pl.program_id(1)
    last = pl.num_programs(1) - 1
    T = q_ref.shape[0]
    qi = q_ref[:, W_B:2 * W_B]
    wi = s_ref[:, S_WI:S_WI + H_IDX]
    qst = jnp.concatenate([qi[:, h * D_IDX:(h + 1) * D_IDX] for h in range(H_IDX)], axis=0)
    wcol = jnp.concatenate([wi[:, h:h + 1] for h in range(H_IDX)], axis=0)

    def scores(dots):
        r = wcol * jnp.maximum(dots, 0.0)
        sc = r[0:T]
        for h in range(1, H_IDX):
            sc = sc + r[h * T:(h + 1) * T]
        return sc

    @pl.when(p < last)
    def _():
        for j, page_ref in enumerate(page_refs):
            o_ref[j] = scores(_dot(qst, page_ref[...]))

    @pl.when(p == last)
    def _():
        sc = scores(_dot_nt(qst, snew_ref[:, S_KI:S_KI + D_IDX]))
        r = lax.broadcasted_iota(I32, sc.shape, 0)
        c = lax.broadcasted_iota(I32, sc.shape, 1)
        o_ref[0] = jnp.where(c <= r, sc, -jnp.inf)
        for j in range(1, len(page_refs)):
            o_ref[j] = jnp.full(sc.shape, -jnp.inf, F32)


def _dsa_sample_attn_kernel(pt_ref, q_ref, sc_ref, knew_ref, vnew_ref, *rest, topk, npg, G):
    kpages, vpages, o_ref = rest[:G], rest[G:2 * G], rest[2 * G]
    thr_scr, x_scr, m_scr, l_scr, acc_scr = rest[2 * G + 1:]
    p = pl.program_id(1)
    last = pl.num_programs(1) - 1
    T = q_ref.shape[0]
    R = H_B * T
    ltot = sc_ref.shape[0] * PAGE_SIZE

    @pl.when(p == 0)
    def _():
        key = _sort_key(sc_ref[...])
        pos = (lax.broadcasted_iota(I32, key.shape, 0) * PAGE_SIZE
               + lax.broadcasted_iota(I32, key.shape, 2))

        def count(pred):
            per_lane = jnp.sum(jnp.where(pred(key, pos), 1.0, 0.0), axis=0)
            return jnp.sum(per_lane, axis=1, keepdims=True)

        thr = _kth_largest_key(lambda t: count(lambda k_, p_: k_ >= t), topk, T)
        cnt_ge = count(lambda k_, p_: k_ >= thr)
        thr_scr[...] = jnp.broadcast_to(thr, thr_scr.shape)
        x_scr[...] = jnp.full(x_scr.shape, ltot, I32)

        @pl.when(jnp.max(cnt_ge) > float(topk))
        def _():
            cnt_gt = count(lambda k_, p_: k_ > thr)
            nbits = max(1, (ltot - 1).bit_length())
            x = _first_positions(lambda c: count(lambda k_, p_: jnp.logical_and(k_ == thr, p_ < c)),
                                 float(topk) - cnt_gt, nbits, T)
            x_scr[...] = jnp.broadcast_to(x, x_scr.shape)

        m_scr[...] = jnp.full(m_scr.shape, NEG, F32)
        l_scr[...] = jnp.zeros_like(l_scr)
        acc_scr[...] = jnp.zeros_like(acc_scr)

    q = q_ref[:, 0:W_B] * DH_B ** -0.5
    qrep = jnp.concatenate([q] * H_B, axis=0)
    rr = lax.broadcasted_iota(I32, (R, W_B), 0) // T
    cc = lax.broadcasted_iota(I32, (R, W_B), 1) // DH_B
    diag = rr == cc
    qbd = jnp.where(diag, qrep, 0.0)
    thr = thr_scr[:, 0:1]
    xlim = x_scr[:, 0:1]

    def selected(page):
        key = _sort_key(sc_ref[page])
        pos = page * PAGE_SIZE + lax.broadcasted_iota(I32, key.shape, 1)
        return jnp.logical_or(key > thr, jnp.logical_and(key == thr, pos <= xlim))

    def masked(logits, valid):
        bias = jnp.where(valid, 0.0, NEG)
        return logits + jnp.concatenate([bias] * H_B, axis=0)

    def update(lgs, pv):
        mx = lgs[0]
        for lg in lgs[1:]:
            mx = jnp.maximum(mx, lg)
        m = m_scr[:, 0:1]
        m_new = jnp.maximum(m, jnp.max(mx, axis=1, keepdims=True))
        alpha = jnp.exp(m - m_new)
        ps = [jnp.exp(lg - m_new) for lg in lgs]
        psum = ps[0]
        for pr in ps[1:]:
            psum = psum + pr
        acc = pv(0, ps[0])
        for j in range(1, len(ps)):
            acc = acc + pv(j, ps[j])
        l_scr[...] = jnp.broadcast_to(alpha * l_scr[:, 0:1] + jnp.sum(psum, axis=1, keepdims=True), l_scr.shape)
        acc_scr[...] = alpha * acc_scr[...] + acc
        m_scr[...] = jnp.broadcast_to(m_new, m_scr.shape)

    q2 = _split2(qbd)

    @pl.when(p < last)
    def _():
        lgs = [masked(_dot3(q2, _split2(kpages[j][...]), _dot), selected(p * G + j)) for j in range(G)]
        update(lgs, lambda j, pr: _dot3(_split2(pr), _split2(vpages[j][...]), _dot_nt))

    @pl.when(p == last)
    def _():
        sel = selected(npg)
        r = lax.broadcasted_iota(I32, sel.shape, 0)
        c = lax.broadcasted_iota(I32, sel.shape, 1)
        lg = masked(_dot3(q2, _split2(knew_ref[...]), _dot_nt), jnp.logical_and(sel, c <= r))
        update([lg], lambda j, pr: _dot3(_split2(pr), _split2(vnew_ref[...]), _dot))
        out = jnp.where(diag, acc_scr[...] / l_scr[:, 0:1], 0.0)
        res = out[0:T]
        for h in range(1, H_B):
            res = res + out[h * T:(h + 1) * T]
        o_ref[...] = res.astype(o_ref.dtype)


def _dsa_sample(bq, s, k, v, pool_k, pool_v, pool_ki, page_table, nb, t):
    n_pool = pool_k.shape[1]
    pool_k = jnp.transpose(pool_k, (0, 1, 3, 4, 2)).reshape(1, n_pool, W_B, PAGE_SIZE)
    pool_v = jnp.transpose(pool_v, (0, 1, 3, 4, 2)).reshape(1, n_pool, W_B, PAGE_SIZE)
    pool_ki = jnp.transpose(pool_ki, (0, 1, 3, 2))
    npg = page_table.shape[1]
    past = npg * PAGE_SIZE
    topk = min(TOPK_MAX, (past + t) // 4)
    gs, ga = math.gcd(npg, SCORE_PAGES), math.gcd(npg, ATTN_PAGES)
    pad = PAGE_SIZE - t
    s_new = jnp.pad(s.reshape(nb, t, LANES), ((0, 0), (0, pad), (0, 0)))
    k_new = jnp.pad(k.reshape(nb, t, W_B), ((0, 0), (0, pad), (0, 0)))
    v_new = jnp.pad(v.reshape(nb, t, W_B), ((0, 0), (0, pad), (0, 0)))
    lastp = npg - 1
    ptot = npg + gs

    def page_spec(tail, j, group):
        zeros = (0,) * len(tail)
        return pl.BlockSpec((None, None) + tail,
                            lambda b, p, pt: (0, pt[b, jnp.minimum(p * group + j, lastp)]) + zeros)

    scores = pl.pallas_call(
        _dsa_sample_score_kernel,
        grid_spec=pltpu.PrefetchScalarGridSpec(
            num_scalar_prefetch=1,
            grid=(nb, npg // gs + 1),
            in_specs=[pl.BlockSpec((t, 2 * W_B), lambda b, p, pt: (b, 0)),
                      pl.BlockSpec((t, LANES), lambda b, p, pt: (b, 0)),
                      pl.BlockSpec((None, PAGE_SIZE, LANES), lambda b, p, pt: (b, 0, 0))]
            + [page_spec((D_IDX, PAGE_SIZE), j, gs) for j in range(gs)],
            out_specs=pl.BlockSpec((None, gs, t, PAGE_SIZE), lambda b, p, pt: (b, p, 0, 0))),
        out_shape=jax.ShapeDtypeStruct((nb, ptot, t, PAGE_SIZE), F32),
        compiler_params=_cparams("parallel", "arbitrary"),
        name="dsa_sample_scores",
    )(page_table, bq, s, s_new, *([pool_ki] * gs))

    kv_tail = (W_B, PAGE_SIZE)
    return pl.pallas_call(
        functools.partial(_dsa_sample_attn_kernel, topk=topk, npg=npg, G=ga),
        grid_spec=pltpu.PrefetchScalarGridSpec(
            num_scalar_prefetch=1,
            grid=(nb, npg // ga + 1),
            in_specs=[pl.BlockSpec((t, 2 * W_B), lambda b, p, pt: (b, 0)),
                      pl.BlockSpec((None, ptot, t, PAGE_SIZE), lambda b, p, pt: (b, 0, 0, 0)),
                      pl.BlockSpec((None, PAGE_SIZE, W_B), lambda b, p, pt: (b, 0, 0)),
                      pl.BlockSpec((None, PAGE_SIZE, W_B), lambda b, p, pt: (b, 0, 0))]
            + [page_spec(kv_tail, j, ga) for j in range(ga)]
            + [page_spec(kv_tail, j, ga) for j in range(ga)],
            out_specs=pl.BlockSpec((t, W_B), lambda b, p, pt: (b, 0)),
            scratch_shapes=[pltpu.VMEM((t, LANES), I32), pltpu.VMEM((t, LANES), I32),
                            pltpu.VMEM((H_B * t, LANES), F32), pltpu.VMEM((H_B * t, LANES), F32),
                            pltpu.VMEM((H_B * t, W_B), F32)]),
        out_shape=jax.ShapeDtypeStruct((nb * t, W_B), F32),
        compiler_params=_cparams("parallel", "arbitrary"),
        name="dsa_sample_attn",
    )(page_table, bq, scores, k_new, v_new, *([pool_k] * ga), *([pool_v] * ga))


def _moe(x, prm, layer):
    wg, wu, wd = prm["moe_wg"][layer], prm["moe_wu"][layer], prm["moe_wd"][layer]
    g, b = prm["ln_g"][layer, 1], prm["ln_b"][layer, 1]
    if wg.dtype == BF16:
        return _moe_ln_grouped(x, prm["router_wt"], prm["router_bt"], wg, wu, wd, g, b)
    return _moe_ln(x, prm["router_w"], prm["router_b"], wg, wu, wd, g, b)


def _pad_chunks(a, nb, t, fill=None):
    c = a.shape[1]
    a3 = a.reshape(nb, t, c)
    if fill is None:
        a3 = jnp.pad(a3, ((0, 0), (0, CHUNK - t), (0, 0)))
    else:
        a3 = jnp.concatenate([a3, jnp.broadcast_to(fill, (nb, CHUNK - t, c))], axis=1)
    return a3.reshape(nb * CHUNK, c)


def _retention_tables(t_true, pos0, t_pad):
    L = CHUNK
    lt = min(L, t_true)
    lg = jnp.log1p(-jnp.exp2(-5.0 - jnp.arange(H_D, dtype=F32)))
    j = jnp.arange(L, dtype=F32)
    causal = jnp.tril(jnp.ones((L, L), dtype=bool))
    dmat = jnp.exp(jnp.where(causal, (j[:, None] - j[None, :]) * lg[:, None, None], -jnp.inf))
    qdec = jnp.exp((j + 1.0) * lg[:, None])
    kdec = jnp.where(j < lt, jnp.exp((lt - 1.0 - j) * lg[:, None]), 0.0)
    sdec = jnp.broadcast_to(jnp.exp(lt * lg)[:, None], (H_D, L))
    dec = jnp.zeros((H_D, L, LANES), F32)
    dec = dec.at[:, :, 0].set(qdec).at[:, :, 1].set(kdec).at[:, :, 2].set(sdec)
    half = DK_D // 2
    freq = ROPE_BASE ** (-jnp.arange(half, dtype=F32) / half)
    pos = (pos0 + jnp.arange(t_pad)).astype(F32)
    ang = pos[:, None] * freq[None, :]
    cos, sin = jnp.cos(ang), jnp.sin(ang)
    return dmat, dec, jnp.concatenate([cos, cos], -1), jnp.concatenate([-sin, sin], -1)


def _forward(x3, st, prm):
    nb, t, d = x3.shape
    n = nb * t
    x = x3.reshape(n, d)
    short = t < CHUNK
    t_pad = CHUNK if short else t
    nc = t_pad // CHUNK
    new = {}
    odt = prm["w_in_e"].dtype

    a, bq, k, v, s, kb, vb = _inproj_e(x, prm["w_in_e"], prm["bias_e"])
    if st is None:
        c0 = jnp.zeros((nb, H_A, DK_A, DV_A), F32)
        n0 = jnp.zeros((nb, H_A, DK_A), F32)
        m0 = jnp.zeros((nb, H_A), F32)
    else:
        c0, n0, m0 = st["mlstm_C"][0], st["mlstm_n"][0], st["mlstm_m"][0]
    m0b = jnp.broadcast_to(m0[:, :, None], (nb, H_A, LANES))
    if short:
        lane = jnp.arange(LANES)
        fill = jnp.where((lane >= S_IG) & (lane < S_IG + H_A), NEG,
                         jnp.where((lane >= S_FG) & (lane < S_FG + H_A), 1e4, 0.0)).astype(F32)
        a_m, s_m = _pad_chunks(a, nb, t), _pad_chunks(s, nb, t, fill)
    else:
        a_m, s_m = a, s
    ya, c1, n1, m1 = _mlstm(a_m, s_m, prm["a_norm_g"], c0, n0, m0b, nb, nc, odt)
    if short:
        ya = ya.reshape(nb, CHUNK, W_A)[:, :t].reshape(n, W_A)
    if st is None:
        yb = _dsa_prompt(bq, s, kb, vb, nb, t)
    else:
        yb = _dsa_sample(bq, s, k, v, st["pool_k"], st["pool_v"], st["pool_ki"], st["page_table"], nb, t)
    new["mlstm_C"], new["mlstm_n"], new["mlstm_m"] = c1[None], n1[None], m1[None, :, :, 0]
    new["k"] = k.reshape(1, nb, t, H_B, DH_B)
    new["v"] = v.reshape(1, nb, t, H_B, DH_B)
    new["kidx"] = s[:, S_KI:S_KI + D_IDX].reshape(1, nb, t, D_IDX)
    x = _outproj_ln(ya, yb, prm["w_out_e"], x, prm["ln_g"][0, 0], prm["ln_b"][0, 0])
    x = _moe(x, prm, 0)

    c, dd = _inproj_o(x, prm["w_in_o"])
    if st is None:
        conv0 = jnp.zeros((nb, SUBLANES, W_C), F32)
        h0 = jnp.zeros((nb, SUBLANES, W_C), F32)
        s0 = jnp.zeros((nb, H_D, DK_D, DV_D), F32)
        pos0 = 0
    else:
        conv0 = jnp.pad(st["conv"][0], ((0, 0), (SUBLANES - (CONV_W - 1), 0), (0, 0)))
        h0 = jnp.broadcast_to(st["lru_h"][0][:, None, :], (nb, SUBLANES, W_C))
        s0 = st["ret_S"][0]
        pos0 = st["page_table"].shape[1] * PAGE_SIZE
    tb = _row_block(t, 256)
    yc, hl = _rglru(c, prm["c_conv_w"], prm["c_conv_b"], prm["c_wa"], prm["c_ba"], prm["c_wx"], prm["c_bx"],
                    prm["c_lambda"], conv0, h0, nb, t // tb, tb, odt)
    dmat, dec, cos2, sin2 = _retention_tables(t, pos0, t_pad)
    d_m = _pad_chunks(dd, nb, t) if short else dd
    yd, s1 = _retention(d_m, cos2, sin2, dmat, dec, prm["d_norm_g"], s0, nb, nc, odt)
    if short:
        yd = yd.reshape(nb, CHUNK, W_D)[:, :t].reshape(n, W_D)
    new["conv"] = c[:, 0:W_C].reshape(nb, t, W_C)[None, :, t - (CONV_W - 1):]
    new["lru_h"] = hl[None, :, 0]
    new["ret_S"] = s1[None]
    x = _outproj_ln(yc, yd, prm["w_out_o"], x, prm["ln_g"][1, 0], prm["ln_b"][1, 0])
    x = _moe(x, prm, 1)
    return x.reshape(nb, t, d), new


def _block_diag(w):
    nblk, blk, _ = w.shape
    eye = jnp.eye(nblk, dtype=w.dtype)
    return (eye[:, None, :, None] * w[:, :, None, :]).reshape(nblk * blk, nblk * blk)


def _prepare_params(w_in_e, b_if_e, a_norm_g, w_out_e, w_in_o, c_conv_w, c_conv_b, c_wa, c_ba, c_wx, c_bx,
                    c_lambda, d_norm_g, w_out_o, router_w, router_bias, moe_w_gate, moe_w_up, moe_w_down,
                    ln_g, ln_b, wdt):
    d = w_in_e.shape[1]
    o = [0]
    for width in (W_A, W_A, W_A, W_A, 2 * H_A, W_B, W_B, W_B, H_IDX * D_IDX, D_IDX, H_IDX):
        o.append(o[-1] + width)
    we = w_in_e[0]
    col = lambda i: we[:, o[i]:o[i + 1]]
    slab = jnp.zeros((d, LANES), F32)
    slab = slab.at[:, S_KI:S_KI + D_IDX].set(col(9)).at[:, S_IG:S_IG + 2 * H_A].set(col(4))
    slab = slab.at[:, S_WI:S_WI + H_IDX].set(col(10))
    w_e = jnp.concatenate([col(0), col(1), col(2), col(3), col(5), col(8), col(6), col(7), slab], axis=1)
    bias_e = jnp.zeros((1, LANES), F32).at[0, S_IG:S_IG + 2 * H_A].set(b_if_e[0])
    rw = jnp.zeros((d, LANES), F32).at[:, :N_EXPERTS].set(router_w)
    rb = jnp.zeros((1, LANES), F32).at[0, :N_EXPERTS].set(router_bias)
    row = lambda a: a.reshape(1, -1).astype(F32)
    return {
        "w_in_e": w_e.astype(wdt), "bias_e": bias_e, "a_norm_g": row(a_norm_g[0]),
        "w_out_e": w_out_e[0].astype(wdt), "w_in_o": w_in_o[0].astype(wdt),
        "c_conv_w": c_conv_w[0].astype(F32), "c_conv_b": row(c_conv_b[0]),
        "c_wa": _block_diag(c_wa[0]).astype(wdt), "c_ba": row(c_ba[0]),
        "c_wx": _block_diag(c_wx[0]).astype(wdt), "c_bx": row(c_bx[0]),
        "c_lambda": row(c_lambda[0]), "d_norm_g": row(d_norm_g[0]), "w_out_o": w_out_o[0].astype(wdt),
        "router_w": rw, "router_b": rb,
        "router_wt": router_w.T.astype(F32), "router_bt": jnp.broadcast_to(rb.reshape(LANES, 1), (LANES, LANES)),
        "moe_wg": moe_w_gate.astype(wdt), "moe_wu": moe_w_up.astype(wdt), "moe_wd": moe_w_down.astype(wdt),
        "ln_g": ln_g.reshape(DEPTH, 2, 1, -1).astype(F32), "ln_b": ln_b.reshape(DEPTH, 2, 1, -1).astype(F32),
    }


def kernel(x_prompt, x_sample, state_mlstm_C, state_mlstm_n, state_mlstm_m, cache_k, cache_v, cache_kidx,
           page_table, state_conv, state_lru_h, state_ret_S, w_in_e, b_if_e, a_norm_g, w_out_e, w_in_o,
           c_conv_w, c_conv_b, c_wa, c_ba, c_wx, c_bx, c_lambda, d_norm_g, w_out_o, router_w, router_bias,
           moe_w_gate, moe_w_up, moe_w_down, ln_g, ln_b):
    weights = (w_in_e, b_if_e, a_norm_g, w_out_e, w_in_o, c_conv_w, c_conv_b, c_wa, c_ba, c_wx, c_bx, c_lambda,
               d_norm_g, w_out_o, router_w, router_bias, moe_w_gate, moe_w_up, moe_w_down, ln_g, ln_b)
    prm_prompt = _prepare_params(*weights, BF16)
    prm_sample = _prepare_params(*weights, F32)
    st = {"mlstm_C": state_mlstm_C, "mlstm_n": state_mlstm_n, "mlstm_m": state_mlstm_m,
          "pool_k": cache_k, "pool_v": cache_v, "pool_ki": cache_kidx, "page_table": page_table,
          "conv": state_conv, "lru_h": state_lru_h, "ret_S": state_ret_S}
    y_p, nsp = _forward(x_prompt, None, prm_prompt)
    y_s, nss = _forward(x_sample, st, prm_sample)
    names = ("mlstm_C", "mlstm_n", "mlstm_m", "k", "v", "kidx", "conv", "lru_h", "ret_S")
    return (y_p, y_s) + tuple(nsp[k] for k in names) + tuple(nss[k] for k in names)
```

```python
import functools
import math

import jax
import jax.numpy as jnp
from jax import lax
from jax.experimental import pallas as pl
from jax.experimental.pallas import tpu as pltpu

F32 = jnp.float32
BF16 = jnp.bfloat16
I32 = jnp.int32

DEPTH = 2
PAGE_SIZE = 128
H_A, DK_A, DV_A = 4, 128, 128
W_A = H_A * DV_A
H_B, DH_B = 8, 64
W_B = H_B * DH_B
H_IDX, D_IDX = 8, 64
TOPK_MAX = 256
W_C, N_BLK_C, CONV_W, LRU_C = 512, 8, 4, 8.0
BLK_C = W_C // N_BLK_C
H_D, DK_D, DV_D = 4, 128, 128
W_D = H_D * DV_D
ROPE_BASE = 10000.0
N_EXPERTS, N_GROUPS, TOP_K_EXP, D_FF_EXP = 16, 4, 2, 512
EXP_PER_GROUP = N_EXPERTS // N_GROUPS
ALPHA = (2 * DEPTH) ** 0.25
LN_EPS = 1e-5
HN_EPS = 1e-6

CHUNK = 128
DSA_QB = 256
LANES = 128
SUBLANES = 8
NEG = -1e30
INT_MIN = -2 ** 31
VMEM_LIMIT = 56 * 1024 * 1024

S_KI = 0
S_IG = 64
S_FG = 68
S_WI = 72


def _cparams(*sem):
    return pltpu.CompilerParams(dimension_semantics=sem, vmem_limit_bytes=VMEM_LIMIT)


def _precision(a, b):
    return lax.Precision.HIGHEST if a.dtype == F32 and b.dtype == F32 else None


def _dot(a, b):
    return jnp.dot(a, b, preferred_element_type=F32, precision=_precision(a, b))


def _dot_nt(a, b):
    return lax.dot_general(a, b, (((1,), (1,)), ((), ())), preferred_element_type=F32, precision=_precision(a, b))


def _dot_tn(a, b):
    return lax.dot_general(a, b, (((0,), (0,)), ((), ())), preferred_element_type=F32, precision=_precision(a, b))


def _split2(a):
    hi = a.astype(BF16)
    return hi, (a - hi.astype(F32)).astype(BF16)


def _dot3(a2, b2, dot):
    m = a2[0].shape[0]
    both = dot(jnp.concatenate([a2[0], a2[1]], axis=0), b2[0])
    return both[0:m] + (both[m:2 * m] + dot(a2[0], b2[1]))


def _row_block(n, target):
    t = min(n, target)
    while n % t:
        t //= 2
    return t


def _layer_norm(z, g, b):
    mu = jnp.mean(z, -1, keepdims=True)
    zc = z - mu
    var = jnp.mean(zc * zc, -1, keepdims=True)
    return zc * lax.rsqrt(var + LN_EPS) * g + b


def _head_norm(h):
    mu = jnp.mean(h, -1, keepdims=True)
    hc = h - mu
    var = jnp.mean(hc * hc, -1, keepdims=True)
    return hc * lax.rsqrt(var + HN_EPS)


def _inproj_e_kernel(x_ref, w_ref, bias_ref, a_ref, bq_ref, k_ref, v_ref, s_ref, kb_ref, vb_ref):
    x = x_ref[...].astype(w_ref.dtype)

    def mm(lo, hi):
        return _dot(x, w_ref[:, lo:hi])

    a_ref[:, 0:W_A] = mm(0, W_A)
    a_ref[:, W_A:2 * W_A] = mm(W_A, 2 * W_A) * DK_A ** -0.5
    a_ref[:, 2 * W_A:4 * W_A] = mm(2 * W_A, 4 * W_A)
    o = 4 * W_A
    bq_ref[...] = mm(o, o + 2 * W_B)
    o += 2 * W_B
    k = mm(o, o + W_B)
    k_ref[...] = k
    kb_ref[...] = k.astype(kb_ref.dtype)
    o += W_B
    v = mm(o, o + W_B)
    v_ref[...] = v
    vb_ref[...] = v.astype(vb_ref.dtype)
    o += W_B
    s_ref[...] = mm(o, o + LANES) + bias_ref[...]


def _inproj_e(x, w, bias):
    n, d = x.shape
    tm = _row_block(n, 256)
    wcols = w.shape[1]
    outs = [(4 * W_A, F32), (2 * W_B, F32), (W_B, F32), (W_B, F32), (LANES, F32), (W_B, w.dtype), (W_B, w.dtype)]
    return pl.pallas_call(
        _inproj_e_kernel,
        grid=(n // tm,),
        in_specs=[pl.BlockSpec((tm, d), lambda i: (i, 0)),
                  pl.BlockSpec((d, wcols), lambda i: (0, 0)),
                  pl.BlockSpec((1, LANES), lambda i: (0, 0))],
        out_specs=[pl.BlockSpec((tm, c), lambda i: (i, 0)) for c, _ in outs],
        out_shape=[jax.ShapeDtypeStruct((n, c), dt) for c, dt in outs],
        compiler_params=_cparams("parallel"),
        name="inproj_even",
    )(x, w, bias)


def _inproj_o_kernel(x_ref, w_ref, c_ref, d_ref):
    x = x_ref[...].astype(w_ref.dtype)
    c_ref[...] = _dot(x, w_ref[:, 0:2 * W_C])
    d_ref[...] = _dot(x, w_ref[:, 2 * W_C:2 * W_C + 4 * W_D])


def _inproj_o(x, w):
    n, d = x.shape
    tm = _row_block(n, 256)
    return pl.pallas_call(
        _inproj_o_kernel,
        grid=(n // tm,),
        in_specs=[pl.BlockSpec((tm, d), lambda i: (i, 0)),
                  pl.BlockSpec(w.shape, lambda i: (0, 0))],
        out_specs=[pl.BlockSpec((tm, 2 * W_C), lambda i: (i, 0)),
                   pl.BlockSpec((tm, 4 * W_D), lambda i: (i, 0))],
        out_shape=[jax.ShapeDtypeStruct((n, 2 * W_C), F32), jax.ShapeDtypeStruct((n, 4 * W_D), F32)],
        compiler_params=_cparams("parallel"),
        name="inproj_odd",
    )(x, w)


def _outproj_ln_kernel(y1_ref, y2_ref, w_ref, x_ref, g_ref, b_ref, o_ref):
    half = y1_ref.shape[1]
    y = _dot(y1_ref[...], w_ref[0:half, :]) + _dot(y2_ref[...], w_ref[half:2 * half, :])
    o_ref[...] = _layer_norm(ALPHA * x_ref[...] + y, g_ref[...], b_ref[...])


def _outproj_ln(y1, y2, w, x, g, b):
    n, d = x.shape
    tm = _row_block(n, 512)
    half = y1.shape[1]
    return pl.pallas_call(
        _outproj_ln_kernel,
        grid=(n // tm,),
        in_specs=[pl.BlockSpec((tm, half), lambda i: (i, 0)),
                  pl.BlockSpec((tm, half), lambda i: (i, 0)),
                  pl.BlockSpec(w.shape, lambda i: (0, 0)),
                  pl.BlockSpec((tm, d), lambda i: (i, 0)),
                  pl.BlockSpec((1, d), lambda i: (0, 0)),
                  pl.BlockSpec((1, d), lambda i: (0, 0))],
        out_specs=pl.BlockSpec((tm, d), lambda i: (i, 0)),
        out_shape=jax.ShapeDtypeStruct((n, d), F32),
        compiler_params=_cparams("parallel"),
        name="outproj_ln",
    )(y1, y2, w, x, g, b)


def _route(logits, bias):
    lane = lax.broadcasted_iota(I32, logits.shape, 1)
    valid = lane < N_EXPERTS
    pos = lane % EXP_PER_GROUP
    grp = (lane // EXP_PER_GROUP).astype(F32)
    s = jax.nn.sigmoid(logits)
    sel = jnp.where(valid, s + bias, NEG)
    rank = jnp.zeros(logits.shape, F32)
    for d in range(1, EXP_PER_GROUP):
        lo = pltpu.roll(sel, d, 1)
        hi = pltpu.roll(sel, LANES - d, 1)
        rank = rank + jnp.where(jnp.logical_and(pos >= d, lo >= sel), 1.0, 0.0)
        rank = rank + jnp.where(jnp.logical_and(pos + d < EXP_PER_GROUP, hi > sel), 1.0, 0.0)
    top2 = jnp.logical_and(rank < TOP_K_EXP, valid)
    contrib = jnp.where(top2, sel, 0.0)
    gs = contrib
    for d in range(1, EXP_PER_GROUP):
        lo = pltpu.roll(contrib, d, 1)
        hi = pltpu.roll(contrib, LANES - d, 1)
        gs = gs + jnp.where(pos >= d, lo, 0.0) + jnp.where(pos + d < EXP_PER_GROUP, hi, 0.0)
    gs = jnp.where(valid, gs, NEG)
    gmax = jnp.max(gs, axis=1, keepdims=True)
    best = jnp.min(jnp.where(gs == gmax, grp, 1e9), axis=1, keepdims=True)
    chosen = jnp.logical_and(top2, grp == best)
    s_sel = jnp.where(chosen, s, 0.0)
    return s_sel / jnp.sum(s_sel, axis=1, keepdims=True)


def _moe_kernel(x_ref, rw_ref, rb_ref, wg_ref, wu_ref, wd_ref, g_ref, b_ref, o_ref,
                xb_scr, comb_scr, acc_scr, *, sub):
    e = pl.program_id(1)
    tm = x_ref.shape[0]

    @pl.when(e == 0)
    def _():
        xb_scr[...] = x_ref[...].astype(xb_scr.dtype)
        logits = jnp.dot(x_ref[...], rw_ref[...], precision=lax.Precision.HIGHEST, preferred_element_type=F32)
        comb_scr[...] = _route(logits, rb_ref[...])
        acc_scr[...] = jnp.zeros_like(acc_scr)

    def rows(r, carry):
        r0 = pl.multiple_of(r * sub, sub)
        xb = xb_scr[pl.ds(r0, sub), :]
        gate = _dot(xb, wg_ref[...])
        h = gate * jax.nn.sigmoid(gate) * _dot(xb, wu_ref[...])
        y = _dot(h.astype(wd_ref.dtype), wd_ref[...])
        comb = comb_scr[pl.ds(r0, sub), :]
        lane = lax.broadcasted_iota(I32, comb.shape, 1)
        c_e = jnp.sum(jnp.where(lane == e, comb, 0.0), axis=1, keepdims=True)
        acc_scr[pl.ds(r0, sub), :] += c_e * y
        return carry

    lax.fori_loop(0, tm // sub, rows, 0)

    @pl.when(e == N_EXPERTS - 1)
    def _():
        o_ref[...] = _layer_norm(ALPHA * x_ref[...] + acc_scr[...], g_ref[...], b_ref[...])


def _moe_ln(x, rw, rb, wg, wu, wd, g, b):
    n, d = x.shape
    tm = _row_block(n, 1024)
    sub = _row_block(tm, 256)
    f = wg.shape[2]
    return pl.pallas_call(
        functools.partial(_moe_kernel, sub=sub),
        grid=(n // tm, N_EXPERTS),
        in_specs=[pl.BlockSpec((tm, d), lambda i, e: (i, 0)),
                  pl.BlockSpec((d, LANES), lambda i, e: (0, 0)),
                  pl.BlockSpec((1, LANES), lambda i, e: (0, 0)),
                  pl.BlockSpec((None, d, f), lambda i, e: (e, 0, 0)),
                  pl.BlockSpec((None, d, f), lambda i, e: (e, 0, 0)),
                  pl.BlockSpec((None, f, d), lambda i, e: (e, 0, 0)),
                  pl.BlockSpec((1, d), lambda i, e: (0, 0)),
                  pl.BlockSpec((1, d), lambda i, e: (0, 0))],
        out_specs=pl.BlockSpec((tm, d), lambda i, e: (i, 0)),
        out_shape=jax.ShapeDtypeStruct((n, d), F32),
        scratch_shapes=[pltpu.VMEM((tm, d), wg.dtype), pltpu.VMEM((tm, LANES), F32), pltpu.VMEM((tm, d), F32)],
        compiler_params=_cparams("parallel", "arbitrary"),
        name="moe_ln",
    )(x, rw, rb, wg, wu, wd, g, b)


def _route_t(logits, bias):
    s = jax.nn.sigmoid(logits)
    sel = s + bias
    top2, gscore = [], []
    for g in range(N_GROUPS):
        v = [sel[g * EXP_PER_GROUP + i:g * EXP_PER_GROUP + i + 1] for i in range(EXP_PER_GROUP)]
        flags = []
        for i in range(EXP_PER_GROUP):
            rank = jnp.zeros(v[i].shape, F32)
            for j in range(EXP_PER_GROUP):
                if j < i:
                    rank = rank + jnp.where(v[j] >= v[i], 1.0, 0.0)
                elif j > i:
                    rank = rank + jnp.where(v[j] > v[i], 1.0, 0.0)
            flags.append(rank < TOP_K_EXP)
        gs = jnp.where(flags[0], v[0], 0.0)
        for i in range(1, EXP_PER_GROUP):
            gs = gs + jnp.where(flags[i], v[i], 0.0)
        top2.append(flags)
        gscore.append(gs)
    gmax = gscore[0]
    for g in range(1, N_GROUPS):
        gmax = jnp.maximum(gmax, gscore[g])
    chosen, taken = [], None
    for g in range(N_GROUPS):
        c = gscore[g] == gmax
        if taken is not None:
            c = jnp.logical_and(c, jnp.logical_not(taken))
        taken = c if taken is None else jnp.logical_or(taken, c)
        chosen.append(c)
    gates = []
    for g in range(N_GROUPS):
        for i in range(EXP_PER_GROUP):
            e = g * EXP_PER_GROUP + i
            gates.append(jnp.where(jnp.logical_and(chosen[g], top2[g][i]), s[e:e + 1], 0.0))
    den = gates[0]
    for gt in gates[1:]:
        den = den + gt
    return [gt / den for gt in gates], chosen


MOE_BLOCK = 1024
MOE_TILE = 256
MOE_TAIL = 64

R_CHOSEN = EXP_PER_GROUP
R_RANK = EXP_PER_GROUP + 1


def _moe_group_kernel(x_ref, rwt_ref, rbt_ref, tri_ref, wg_ref, wu_ref, wd_ref, g_ref, b_ref, o_ref,
                      xb_scr, info_scr, acc_scr):
    g = pl.program_id(1)
    tb = x_ref.shape[0]
    mdt = xb_scr.dtype

    @pl.when(g == 0)
    def _():
        xb_scr[...] = x_ref[...].astype(mdt)
        logits = lax.dot_general(rwt_ref[...], x_ref[...], (((1,), (1,)), ((), ())),
                                 precision=lax.Precision.HIGHEST, preferred_element_type=F32)
        comb, chosen = _route_t(logits, rbt_ref[0:N_EXPERTS, 0:1])
        flags = [jnp.where(c, 1.0, 0.0) for c in chosen]
        pad = jnp.zeros((SUBLANES - N_GROUPS, tb), F32)
        prefix = _dot(jnp.concatenate(flags + [pad], axis=0).astype(BF16), tri_ref[...])
        fill = jnp.zeros((SUBLANES - EXP_PER_GROUP - 2, tb), F32)
        for gg in range(N_GROUPS):
            rows = comb[gg * EXP_PER_GROUP:(gg + 1) * EXP_PER_GROUP] + [flags[gg], prefix[gg:gg + 1] - 1.0, fill]
            info_scr[gg] = jnp.concatenate(rows, axis=0)
        acc_scr[...] = jnp.zeros_like(acc_scr)

    info = info_scr[g]
    chosen_row = info[R_CHOSEN:R_CHOSEN + 1] > 0.5
    rank_row = info[R_RANK:R_RANK + 1]
    count = jnp.max(jnp.where(chosen_row, rank_row + 1.0, 0.0)).astype(I32)

    def tile(first, rows):
        rid = (lax.broadcasted_iota(I32, (rows, tb), 0) + first).astype(F32)
        onehot = jnp.where(jnp.logical_and(rank_row == rid, chosen_row), 1.0, 0.0)
        pb = onehot.astype(mdt)
        xg = _dot(pb, xb_scr[...]).astype(mdt)
        out = jnp.zeros((rows, x_ref.shape[1]), F32)
        for i in range(EXP_PER_GROUP):
            gate = jnp.sum(onehot * info[i:i + 1], axis=1, keepdims=True)
            gt = _dot(xg, wg_ref[i])
            h = gt * jax.nn.sigmoid(gt) * _dot(xg, wu_ref[i])
            out = out + gate * _dot(h.astype(mdt), wd_ref[i])
        acc_scr[...] += _dot_tn(pb, out.astype(mdt))

    def full_tile(s, carry):
        tile(s * MOE_TILE, MOE_TILE)
        return carry

    nfull = count // MOE_TILE
    rem = count - nfull * MOE_TILE
    lax.fori_loop(0, nfull, full_tile, 0)

    @pl.when(rem > MOE_TAIL)
    def _():
        tile(nfull * MOE_TILE, MOE_TILE)

    @pl.when(jnp.logical_and(rem > 0, rem <= MOE_TAIL))
    def _():
        tile(nfull * MOE_TILE, MOE_TAIL)

    @pl.when(g == N_GROUPS - 1)
    def _():
        o_ref[...] = _layer_norm(ALPHA * x_ref[...] + acc_scr[...], g_ref[...], b_ref[...])


def _moe_ln_grouped(x, rwt, rbt, wg, wu, wd, g, b):
    n, d = x.shape
    tb = _row_block(n, MOE_BLOCK)
    f = wg.shape[2]
    tri = jnp.triu(jnp.ones((tb, tb), BF16))
    once = pl.Buffered(1)
    return pl.pallas_call(
        _moe_group_kernel,
        grid=(n // tb, N_GROUPS),
        in_specs=[pl.BlockSpec((tb, d), lambda i, q: (i, 0), pipeline_mode=once),
                  pl.BlockSpec((N_EXPERTS, d), lambda i, q: (0, 0), pipeline_mode=once),
                  pl.BlockSpec((LANES, LANES), lambda i, q: (0, 0), pipeline_mode=once),
                  pl.BlockSpec((tb, tb), lambda i, q: (0, 0), pipeline_mode=once),
                  pl.BlockSpec((EXP_PER_GROUP, d, f), lambda i, q: (q, 0, 0)),
                  pl.BlockSpec((EXP_PER_GROUP, d, f), lambda i, q: (q, 0, 0)),
                  pl.BlockSpec((EXP_PER_GROUP, f, d), lambda i, q: (q, 0, 0)),
                  pl.BlockSpec((1, d), lambda i, q: (0, 0)),
                  pl.BlockSpec((1, d), lambda i, q: (0, 0))],
        out_specs=pl.BlockSpec((tb, d), lambda i, q: (i, 0), pipeline_mode=once),
        out_shape=jax.ShapeDtypeStruct((n, d), F32),
        scratch_shapes=[pltpu.VMEM((tb, d), wg.dtype), pltpu.VMEM((N_GROUPS, SUBLANES, tb), F32),
                        pltpu.VMEM((tb, d), F32)],
        compiler_params=_cparams("parallel", "arbitrary"),
        name="moe_group_ln",
    )(x, rwt, rbt, tri, wg, wu, wd, g, b)


def _mlstm_kernel(a_ref, s_ref, g_ref, c0_ref, n0_ref, m0_ref, y_ref, c1_ref, n1_ref, m1_ref,
                  c_scr, n_scr, m_scr):
    c = pl.program_id(1)
    L = a_ref.shape[0]
    mdt = y_ref.dtype

    @pl.when(c == 0)
    def _():
        c_scr[...] = c0_ref[...]
        n_scr[...] = n0_ref[...]
        m_scr[...] = m0_ref[...]

    S = s_ref[...]
    lane = lax.broadcasted_iota(I32, S.shape, 1)
    is_f = jnp.logical_and(lane >= S_FG, lane < S_FG + H_A)
    lf = jnp.where(is_f, jax.nn.log_sigmoid(S), 0.0)
    row = lax.broadcasted_iota(I32, (L, L), 0)
    col = lax.broadcasted_iota(I32, (L, L), 1)
    causal = row >= col
    Fs = jnp.dot(causal.astype(F32), lf, precision=lax.Precision.HIGHEST, preferred_element_type=F32)
    Fa = pltpu.roll(Fs, LANES - (S_FG - S_IG), 1)
    AT = jnp.transpose(S - Fa)

    for h in range(H_A):
        ig = S[:, S_IG + h:S_IG + h + 1]
        F = Fa[:, S_IG + h:S_IG + h + 1]
        a_row = AT[S_IG + h:S_IG + h + 1, :]
        m_prev = m_scr[h:h + 1, 0:1]
        cm = jnp.max(jnp.where(causal, a_row, NEG), axis=1, keepdims=True)
        m_t = F + jnp.maximum(m_prev, cm)
        dmat = jnp.exp(jnp.where(causal, (F - m_t) + a_row, NEG))
        inter = jnp.exp(F + m_prev - m_t)
        q = a_ref[:, h * DK_A:(h + 1) * DK_A].astype(mdt)
        kf = a_ref[:, W_A + h * DK_A:W_A + (h + 1) * DK_A]
        k = kf.astype(mdt)
        v = a_ref[:, 2 * W_A + h * DV_A:2 * W_A + (h + 1) * DV_A].astype(mdt)
        C = c_scr[h]
        n = n_scr[h:h + 1, :]
        s = _dot_nt(q, k) * dmat
        num = _dot(s.astype(mdt), v) + inter * _dot(q, C.astype(mdt))
        qn = jnp.sum(q.astype(F32) * n.astype(mdt).astype(F32), axis=1, keepdims=True)
        den = jnp.sum(s, axis=1, keepdims=True) + inter * qn
        hout = num / jnp.maximum(jnp.abs(den), jnp.exp(-m_t))
        o = a_ref[:, 3 * W_A + h * DV_A:3 * W_A + (h + 1) * DV_A]
        y = _head_norm(hout) * g_ref[:, h * DV_A:(h + 1) * DV_A] * jax.nn.sigmoid(o)
        y_ref[:, h * DV_A:(h + 1) * DV_A] = y.astype(y_ref.dtype)
        m_new = m_t[L - 1:L, :]
        F_last = F[L - 1:L, :]
        w_s = jnp.exp(F_last - F + ig - m_new)
        decay = jnp.exp(F_last + m_prev - m_new)
        kw = kf * w_s
        c_scr[h] = decay * C + _dot_tn(kw.astype(mdt), v)
        n_scr[h:h + 1, :] = decay * n + jnp.sum(kw, axis=0, keepdims=True)
        m_scr[h:h + 1, :] = jnp.broadcast_to(m_new, (1, LANES))

    @pl.when(c == pl.num_programs(1) - 1)
    def _():
        c1_ref[...] = c_scr[...]
        n1_ref[...] = n_scr[...]
        m1_ref[...] = m_scr[...]


def _mlstm(a, s, gnorm, c0, n0, m0b, nb, nc, odt):
    n = a.shape[0]
    L = CHUNK
    return pl.pallas_call(
        _mlstm_kernel,
        grid=(nb, nc),
        in_specs=[pl.BlockSpec((L, 4 * W_A), lambda b, c: (b * nc + c, 0)),
                  pl.BlockSpec((L, LANES), lambda b, c: (b * nc + c, 0)),
                  pl.BlockSpec((1, W_A), lambda b, c: (0, 0)),
                  pl.BlockSpec((None, H_A, DK_A, DV_A), lambda b, c: (b, 0, 0, 0)),
                  pl.BlockSpec((None, H_A, DK_A), lambda b, c: (b, 0, 0)),
                  pl.BlockSpec((None, H_A, LANES), lambda b, c: (b, 0, 0))],
        out_specs=[pl.BlockSpec((L, W_A), lambda b, c: (b * nc + c, 0)),
                   pl.BlockSpec((None, H_A, DK_A, DV_A), lambda b, c: (b, 0, 0, 0)),
                   pl.BlockSpec((None, H_A, DK_A), lambda b, c: (b, 0, 0)),
                   pl.BlockSpec((None, H_A, LANES), lambda b, c: (b, 0, 0))],
        out_shape=[jax.ShapeDtypeStruct((n, W_A), odt),
                   jax.ShapeDtypeStruct((nb, H_A, DK_A, DV_A), F32),
                   jax.ShapeDtypeStruct((nb, H_A, DK_A), F32),
                   jax.ShapeDtypeStruct((nb, H_A, LANES), F32)],
        scratch_shapes=[pltpu.VMEM((H_A, DK_A, DV_A), F32), pltpu.VMEM((H_A, DK_A), F32),
                        pltpu.VMEM((H_A, LANES), F32)],
        compiler_params=_cparams("parallel", "arbitrary"),
        name="mlstm",
    )(a, s, gnorm, c0, n0, m0b)


def _retention_kernel(d_ref, cos_ref, sin_ref, dm_ref, dec_ref, g_ref, s0_ref, y_ref, s1_ref, s_scr):
    c = pl.program_id(1)

    @pl.when(c == 0)
    def _():
        s_scr[...] = s0_ref[...]

    cos2 = cos_ref[...]
    sin2 = sin_ref[...]
    mdt = y_ref.dtype
    for h in range(H_D):
        qf = d_ref[:, h * DK_D:(h + 1) * DK_D]
        kf = d_ref[:, W_D + h * DK_D:W_D + (h + 1) * DK_D]
        q = qf * cos2 + pltpu.roll(qf, DK_D // 2, 1) * sin2
        k = (kf * cos2 + pltpu.roll(kf, DK_D // 2, 1) * sin2) * DK_D ** -0.5
        v = d_ref[:, 2 * W_D + h * DV_D:2 * W_D + (h + 1) * DV_D].astype(mdt)
        qdec = dec_ref[h, :, 0:1]
        kdec = dec_ref[h, :, 1:2]
        sdec = dec_ref[h, 0:1, 2:3]
        S = s_scr[h]
        qb = q.astype(mdt)
        att = _dot_nt(qb, k.astype(mdt)) * dm_ref[h]
        o = _dot(att.astype(mdt), v) + qdec * _dot(qb, S.astype(mdt))
        s_scr[h] = sdec * S + _dot_tn((k * kdec).astype(mdt), v)
        gt = d_ref[:, 3 * W_D + h * DV_D:3 * W_D + (h + 1) * DV_D]
        y = _head_norm(o) * g_ref[:, h * DV_D:(h + 1) * DV_D] * (gt * jax.nn.sigmoid(gt))
        y_ref[:, h * DV_D:(h + 1) * DV_D] = y.astype(y_ref.dtype)

    @pl.when(c == pl.num_programs(1) - 1)
    def _():
        s1_ref[...] = s_scr[...]


def _retention(d, cos2, sin2, dmat, dec, gnorm, s0, nb, nc, odt):
    n = d.shape[0]
    L = CHUNK
    return pl.pallas_call(
        _retention_kernel,
        grid=(nb, nc),
        in_specs=[pl.BlockSpec((L, 4 * W_D), lambda b, c: (b * nc + c, 0)),
                  pl.BlockSpec((L, DK_D), lambda b, c: (c, 0)),
                  pl.BlockSpec((L, DK_D), lambda b, c: (c, 0)),
                  pl.BlockSpec((H_D, L, L), lambda b, c: (0, 0, 0)),
                  pl.BlockSpec((H_D, L, LANES), lambda b, c: (0, 0, 0)),
                  pl.BlockSpec((1, W_D), lambda b, c: (0, 0)),
                  pl.BlockSpec((None, H_D, DK_D, DV_D), lambda b, c: (b, 0, 0, 0))],
        out_specs=[pl.BlockSpec((L, W_D), lambda b, c: (b * nc + c, 0)),
                   pl.BlockSpec((None, H_D, DK_D, DV_D), lambda b, c: (b, 0, 0, 0))],
        out_shape=[jax.ShapeDtypeStruct((n, W_D), odt),
                   jax.ShapeDtypeStruct((nb, H_D, DK_D, DV_D), F32)],
        scratch_shapes=[pltpu.VMEM((H_D, DK_D, DV_D), F32)],
        compiler_params=_cparams("parallel", "arbitrary"),
        name="retention",
    )(d, cos2, sin2, dmat, dec, gnorm, s0)


def _shift_rows(x, prev, j):
    tb = x.shape[0]
    xs = pltpu.roll(x, j, 0)
    pr = pltpu.roll(prev, j, 0)
    row = lax.broadcasted_iota(I32, pr.shape, 0)
    first = jnp.where(row < j, pr, xs[0:SUBLANES])
    if tb == SUBLANES:
        return first
    return jnp.concatenate([first, xs[SUBLANES:]], axis=0)


def _rglru_kernel(c_ref, cw_ref, cb_ref, wa_ref, ba_ref, wx_ref, bx_ref, lam_ref, conv0_ref, h0_ref,
                  y_ref, hl_ref, prev_scr, h_scr):
    t = pl.program_id(1)
    tb = c_ref.shape[0]

    @pl.when(t == 0)
    def _():
        prev_scr[...] = conv0_ref[...]
        h_scr[...] = h0_ref[...]

    x = c_ref[:, 0:W_C]
    gate = c_ref[:, W_C:2 * W_C]
    prev = prev_scr[...]
    xc = x * cw_ref[CONV_W - 1:CONV_W, :] + cb_ref[...]
    for j in range(1, CONV_W):
        xc = xc + _shift_rows(x, prev, j) * cw_ref[CONV_W - 1 - j:CONV_W - j, :]
    prev_scr[...] = x[tb - SUBLANES:tb]

    xb = xc.astype(wa_ref.dtype)
    r = jax.nn.sigmoid(_dot(xb, wa_ref[...]) + ba_ref[...])
    i = jax.nn.sigmoid(_dot(xb, wx_ref[...]) + bx_ref[...])
    log_a = -LRU_C * r * jax.nn.softplus(-lam_ref[...])
    A = jnp.exp(log_a)
    th = jnp.tanh(log_a)
    U = jnp.sqrt(-2.0 * th / (1.0 - th)) * (i * xc)
    row = lax.broadcasted_iota(I32, (tb, W_C), 0)
    d = 1
    while d < tb:
        keep = row >= d
        U = jnp.where(keep, U + A * pltpu.roll(U, d, 0), U)
        A = jnp.where(keep, A * pltpu.roll(A, d, 0), A)
        d *= 2
    h = U + A * h_scr[0:1, :]
    h_scr[...] = jnp.broadcast_to(h[tb - 1:tb, :], (SUBLANES, W_C))
    y_ref[...] = (h * jax.nn.gelu(gate)).astype(y_ref.dtype)

    @pl.when(t == pl.num_programs(1) - 1)
    def _():
        hl_ref[...] = h_scr[...]


def _rglru(c, cw, cb, wa, ba, wx, bx, lam, conv0, h0, nb, nt, tb, odt):
    n = c.shape[0]
    vec = pl.BlockSpec((1, W_C), lambda b, t: (0, 0))
    mat = pl.BlockSpec((W_C, W_C), lambda b, t: (0, 0))
    st = pl.BlockSpec((None, SUBLANES, W_C), lambda b, t: (b, 0, 0))
    return pl.pallas_call(
        _rglru_kernel,
        grid=(nb, nt),
        in_specs=[pl.BlockSpec((tb, 2 * W_C), lambda b, t: (b * nt + t, 0)),
                  pl.BlockSpec((CONV_W, W_C), lambda b, t: (0, 0)), vec, mat, vec, mat, vec, vec, st, st],
        out_specs=[pl.BlockSpec((tb, W_C), lambda b, t: (b * nt + t, 0)), st],
        out_shape=[jax.ShapeDtypeStruct((n, W_C), odt), jax.ShapeDtypeStruct((nb, SUBLANES, W_C), F32)],
        scratch_shapes=[pltpu.VMEM((SUBLANES, W_C), F32), pltpu.VMEM((SUBLANES, W_C), F32)],
        compiler_params=_cparams("parallel", "arbitrary"),
        name="rglru",
    )(c, cw, cb, wa, ba, wx, bx, lam, conv0, h0)


def _sort_key(score):
    score = jnp.where(score == 0.0, 0.0, score)
    bits = lax.bitcast_convert_type(score, I32)
    return bits ^ ((bits >> 31) & 0x7FFFFFFF)


def _kth_largest_key(count_ge, k, rows):
    kf = float(k)
    zero = jnp.zeros((rows, 1), I32)
    t0 = jnp.where(count_ge(zero) >= kf, zero, jnp.full((rows, 1), INT_MIN, I32))

    def body(bi, t):
        cand = t + jnp.left_shift(jnp.int32(1), 30 - bi)
        return jnp.where(count_ge(cand) >= kf, cand, t)

    return lax.fori_loop(0, 31, body, t0)


def _first_positions(count_eq_before, need, nbits, rows):
    def body(bi, x):
        cand = x + jnp.left_shift(jnp.int32(1), nbits - 1 - bi)
        return jnp.where(count_eq_before(cand) < need, cand, x)

    return lax.fori_loop(0, nbits, body, jnp.zeros((rows, 1), I32))


def _qi_heads(qi):
    lane = lax.broadcasted_iota(I32, (qi.shape[0], LANES), 1)
    low = lane < D_IDX
    out = []
    for h in range(H_IDX):
        pair = qi[:, (h // 2) * LANES:(h // 2 + 1) * LANES]
        if h % 2:
            pair = pltpu.roll(pair, D_IDX, 1)
        out.append(jnp.where(low, pair, 0.0).astype(BF16))
    return out


def _dsa_prompt_kernel(q_ref, s_ref, ki_ref, kb_ref, vb_ref, o_ref, key_scr, bias_scr, x_scr, thr_scr, lg_scr,
                       *, q_off, kt_size, topk):
    i = pl.program_id(1)
    QB = q_ref.shape[0]
    KT = kt_size
    nkt = key_scr.shape[0]
    lk = nkt * KT
    q0 = q_off + i * QB
    qih = _qi_heads(q_ref[:, W_B:2 * W_B])
    wi = s_ref[:, S_WI:S_WI + H_IDX]
    wcols = [wi[:, h:h + 1] for h in range(H_IDX)]
    rowpos = q0 + lax.broadcasted_iota(I32, (QB, KT), 0)
    colpos = lax.broadcasted_iota(I32, (QB, KT), 1)

    def score_tile(kt, carry):
        k0 = pl.multiple_of(kt * KT, KT)
        ki = ki_ref[pl.ds(k0, KT), :].astype(BF16)
        sc = jnp.zeros((QB, KT), F32)
        for h in range(H_IDX):
            sc = sc + wcols[h] * jnp.maximum(_dot_nt(qih[h], ki), 0.0)
        sc = jnp.where(colpos + k0 <= rowpos, sc, -jnp.inf)
        key_scr[kt] = _sort_key(sc)
        return carry

    lax.fori_loop(0, nkt, score_tile, 0, unroll=2)

    lanepos = lax.broadcasted_iota(I32, (QB, LANES), 1)
    x_scr[...] = jnp.full(x_scr.shape, lk, I32)

    SR = min(QB, CHUNK)
    lanepos_s = lax.broadcasted_iota(I32, (SR, LANES), 1)
    for r0 in range(0, QB, SR):
        part = slice(r0, r0 + SR)

        def count(pred, part=part):
            def body(kt, acc):
                tile = key_scr[kt, part, :]
                k0 = kt * KT
                for j in range(KT // LANES):
                    sl = slice(j * LANES, (j + 1) * LANES)
                    acc = acc + jnp.where(pred(tile[:, sl], lanepos_s + (k0 + j * LANES)), 1.0, 0.0)
                return acc
            acc = lax.fori_loop(0, nkt, body, jnp.zeros((SR, LANES), F32))
            return jnp.sum(acc, axis=1, keepdims=True)

        thr_p = _kth_largest_key(lambda t: count(lambda key, pos: key >= t), topk, SR)
        thr_scr[part, :] = jnp.broadcast_to(thr_p, (SR, LANES))
        cnt_ge = count(lambda key, pos: key >= thr_p)

        @pl.when(jnp.max(cnt_ge) > float(topk))
        def _(count=count, thr_p=thr_p, part=part):
            need = float(topk) - count(lambda key, pos: key > thr_p)
            nbits = max(1, (lk - 1).bit_length())
            x = _first_positions(
                lambda c: count(lambda key, pos: jnp.logical_and(key == thr_p, pos < c)), need, nbits, SR)
            x_scr[part, :] = jnp.broadcast_to(x, (SR, LANES))

    thr = thr_scr[:, 0:1]
    xlim = x_scr[:, 0:1]

    def bias_tile(kt, carry):
        key = key_scr[kt]
        pos = colpos + kt * KT
        sel = jnp.logical_or(key > thr, jnp.logical_and(key == thr, pos <= xlim))
        bias_scr[kt] = jnp.where(jnp.logical_and(sel, pos <= rowpos), 0.0, NEG)
        return carry

    lax.fori_loop(0, nkt, bias_tile, 0)

    low = lanepos < DH_B
    ngrp = KT // LANES

    def group_fold(op, acc, x):
        for g in range(ngrp):
            acc = op(acc, x[:, g * LANES:(g + 1) * LANES])
        return acc

    for j in range(H_B // 2):
        qpair = q_ref[:, j * LANES:(j + 1) * LANES] * DH_B ** -0.5
        q0m = jnp.where(low, qpair, 0.0).astype(BF16)
        q1m = jnp.where(low, 0.0, qpair).astype(BF16)

        def pass1(kt, carry, q0m=q0m, q1m=q1m, j=j):
            mx0, mx1 = carry
            k0 = pl.multiple_of(kt * KT, KT)
            kk = kb_ref[pl.ds(k0, KT), j * LANES:(j + 1) * LANES]
            bias = bias_scr[kt]
            lg0 = _dot_nt(q0m, kk) + bias
            lg1 = _dot_nt(q1m, kk) + bias
            lg_scr[0, kt] = lg0
            lg_scr[1, kt] = lg1
            return group_fold(jnp.maximum, mx0, lg0), group_fold(jnp.maximum, mx1, lg1)

        neg = jnp.full((QB, LANES), NEG, F32)
        mx0, mx1 = lax.fori_loop(0, nkt, pass1, (neg, neg), unroll=2)
        m0 = jnp.max(mx0, axis=1, keepdims=True)
        m1 = jnp.max(mx1, axis=1, keepdims=True)

        def pass2(kt, carry, m0=m0, m1=m1, j=j):
            l0, l1, a0, a1 = carry
            k0 = pl.multiple_of(kt * KT, KT)
            vv = vb_ref[pl.ds(k0, KT), j * LANES:(j + 1) * LANES]
            p0 = jnp.exp(lg_scr[0, kt] - m0)
            p1 = jnp.exp(lg_scr[1, kt] - m1)
            a0 = a0 + _dot(p0.astype(BF16), vv)
            a1 = a1 + _dot(p1.astype(BF16), vv)
            return group_fold(jnp.add, l0, p0), group_fold(jnp.add, l1, p1), a0, a1

        zero = jnp.zeros((QB, LANES), F32)
        l0, l1, a0, a1 = lax.fori_loop(0, nkt, pass2, (zero, zero, zero, zero), unroll=2)
        out0 = a0 / jnp.sum(l0, axis=1, keepdims=True)
        out1 = a1 / jnp.sum(l1, axis=1, keepdims=True)
        o_ref[:, j * LANES:(j + 1) * LANES] = jnp.where(low, out0, out1).astype(o_ref.dtype)


def _dsa_prompt(bq, s, kb, vb, nb, t):
    n = bq.shape[0]
    QB = math.gcd(t, DSA_QB)
    topk = min(TOPK_MAX, t // 4)
    nseg = 1
    for cand in (8, 4, 2):
        if t % cand == 0 and (t // cand) % 512 == 0:
            nseg = cand
            break
    seg = t // nseg
    KT = 512 if seg % 512 == 0 else seg
    s3 = s.reshape(nb, t, LANES)
    kb3 = kb.reshape(nb, t, W_B)
    vb3 = vb.reshape(nb, t, W_B)
    nqb = seg // QB
    outs = []
    for g in range(nseg):
        lk = (g + 1) * seg
        nkt = lk // KT
        row_blk = functools.partial(lambda b, i, g: (b * (t // QB) + g * nqb + i, 0), g=g)
        outs.append(pl.pallas_call(
            functools.partial(_dsa_prompt_kernel, q_off=g * seg, kt_size=KT, topk=topk),
            grid=(nb, nqb),
            in_specs=[pl.BlockSpec((QB, 2 * W_B), row_blk),
                      pl.BlockSpec((QB, LANES), row_blk),
                      pl.BlockSpec((None, lk, LANES), lambda b, i: (b, 0, 0)),
                      pl.BlockSpec((None, lk, W_B), lambda b, i: (b, 0, 0)),
                      pl.BlockSpec((None, lk, W_B), lambda b, i: (b, 0, 0))],
            out_specs=pl.BlockSpec((None, QB, W_B), lambda b, i: (b, i, 0)),
            out_shape=jax.ShapeDtypeStruct((nb, seg, W_B), BF16),
            scratch_shapes=[pltpu.VMEM((nkt, QB, KT), I32), pltpu.VMEM((nkt, QB, KT), F32),
                            pltpu.VMEM((QB, LANES), I32), pltpu.VMEM((QB, LANES), I32),
                            pltpu.VMEM((2, nkt, QB, KT), F32)],
            compiler_params=_cparams("parallel", "arbitrary"),
            name=f"dsa_prompt_{g}",
        )(bq, s, s3, kb3, vb3))
    return jnp.concatenate(outs, axis=1).reshape(n, W_B)


SCORE_PAGES = 16
ATTN_PAGES = 16


def _dsa_sample_score_kernel(pt_ref, q_ref, s_ref, snew_ref, *rest):
    page_refs, o_ref = rest[:-1], rest[-1]
    p = pl.program_id(1)
    last = pl.num_programs(1) - 1
    T = q_ref.shape[0]
    qi = q_ref[:, W_B:2 * W_B]
    wi = s_ref[:, S_WI:S_WI + H_IDX]
    qst = jnp.concatenate([qi[:, h * D_IDX:(h + 1) * D_IDX] for h in range(H_IDX)], axis=0)
    wcol = jnp.concatenate([wi[:, h:h + 1] for h in range(H_IDX)], axis=0)

    def scores(dots):
        r = wcol * jnp.maximum(dots, 0.0)
        sc = r[0:T]
        for h in range(1, H_IDX):
            sc = sc + r[h * T:(h + 1) * T]
        return sc

    @pl.when(p < last)
    def _():
        for j, page_ref in enumerate(page_refs):
            o_ref[j] = scores(_dot(qst, page_ref[...]))

    @pl.when(p == last)
    def _():
        sc = scores(_dot_nt(qst, snew_ref[:, S_KI:S_KI + D_IDX]))
        r = lax.broadcasted_iota(I32, sc.shape, 0)
        c = lax.broadcasted_iota(I32, sc.shape, 1)
        o_ref[0] = jnp.where(c <= r, sc, -jnp.inf)
        for j in range(1, len(page_refs)):
            o_ref[j] = jnp.full(sc.shape, -jnp.inf, F32)


def _dsa_sample_attn_kernel(pt_ref, q_ref, sc_ref, knew_ref, vnew_ref, *rest, topk, npg, G):
    kpages, vpages, o_ref = rest[:G], rest[G:2 * G], rest[2 * G]
    thr_scr, x_scr, m_scr, l_scr, acc_scr = rest[2 * G + 1:]
    p = pl.program_id(1)
    last = pl.num_programs(1) - 1
    T = q_ref.shape[0]
    R = H_B * T
    ltot = sc_ref.shape[0] * PAGE_SIZE

    @pl.when(p == 0)
    def _():
        key = _sort_key(sc_ref[...])
        pos = (lax.broadcasted_iota(I32, key.shape, 0) * PAGE_SIZE
               + lax.broadcasted_iota(I32, key.shape, 2))

        def count(pred):
            per_lane = jnp.sum(jnp.where(pred(key, pos), 1.0, 0.0), axis=0)
            return jnp.sum(per_lane, axis=1, keepdims=True)

        thr = _kth_largest_key(lambda t: count(lambda k_, p_: k_ >= t), topk, T)
        cnt_ge = count(lambda k_, p_: k_ >= thr)
        thr_scr[...] = jnp.broadcast_to(thr, thr_scr.shape)
        x_scr[...] = jnp.full(x_scr.shape, ltot, I32)

        @pl.when(jnp.max(cnt_ge) > float(topk))
        def _():
            cnt_gt = count(lambda k_, p_: k_ > thr)
            nbits = max(1, (ltot - 1).bit_length())
            x = _first_positions(lambda c: count(lambda k_, p_: jnp.logical_and(k_ == thr, p_ < c)),
                                 float(topk) - cnt_gt, nbits, T)
            x_scr[...] = jnp.broadcast_to(x, x_scr.shape)

        m_scr[...] = jnp.full(m_scr.shape, NEG, F32)
        l_scr[...] = jnp.zeros_like(l_scr)
        acc_scr[...] = jnp.zeros_like(acc_scr)

    q = q_ref[:, 0:W_B] * DH_B ** -0.5
    qrep = jnp.concatenate([q] * H_B, axis=0)
    rr = lax.broadcasted_iota(I32, (R, W_B), 0) // T
    cc = lax.broadcasted_iota(I32, (R, W_B), 1) // DH_B
    diag = rr == cc
    qbd = jnp.where(diag, qrep, 0.0)
    thr = thr_scr[:, 0:1]
    xlim = x_scr[:, 0:1]

    def selected(page):
        key = _sort_key(sc_ref[page])
        pos = page * PAGE_SIZE + lax.broadcasted_iota(I32, key.shape, 1)
        return jnp.logical_or(key > thr, jnp.logical_and(key == thr, pos <= xlim))

    def masked(logits, valid):
        bias = jnp.where(valid, 0.0, NEG)
        return logits + jnp.concatenate([bias] * H_B, axis=0)

    def update(lgs, pv):
        mx = lgs[0]
        for lg in lgs[1:]:
            mx = jnp.maximum(mx, lg)
        m = m_scr[:, 0:1]
        m_new = jnp.maximum(m, jnp.max(mx, axis=1, keepdims=True))
        alpha = jnp.exp(m - m_new)
        ps = [jnp.exp(lg - m_new) for lg in lgs]
        psum = ps[0]
        for pr in ps[1:]:
            psum = psum + pr
        acc = pv(0, ps[0])
        for j in range(1, len(ps)):
            acc = acc + pv(j, ps[j])
        l_scr[...] = jnp.broadcast_to(alpha * l_scr[:, 0:1] + jnp.sum(psum, axis=1, keepdims=True), l_scr.shape)
        acc_scr[...] = alpha * acc_scr[...] + acc
        m_scr[...] = jnp.broadcast_to(m_new, m_scr.shape)

    q2 = _split2(qbd)

    @pl.when(p < last)
    def _():
        lgs = [masked(_dot3(q2, _split2(kpages[j][...]), _dot), selected(p * G + j)) for j in range(G)]
        update(lgs, lambda j, pr: _dot3(_split2(pr), _split2(vpages[j][...]), _dot_nt))

    @pl.when(p == last)
    def _():
        sel = selected(npg)
        r = lax.broadcasted_iota(I32, sel.shape, 0)
        c = lax.broadcasted_iota(I32, sel.shape, 1)
        lg = masked(_dot3(q2, _split2(knew_ref[...]), _dot_nt), jnp.logical_and(sel, c <= r))
        update([lg], lambda j, pr: _dot3(_split2(pr), _split2(vnew_ref[...]), _dot))
        out = jnp.where(diag, acc_scr[...] / l_scr[:, 0:1], 0.0)
        res = out[0:T]
        for h in range(1, H_B):
            res = res + out[h * T:(h + 1) * T]
        o_ref[...] = res.astype(o_ref.dtype)


def _dsa_sample(bq, s, k, v, pool_k, pool_v, pool_ki, page_table, nb, t):
    n_pool = pool_k.shape[1]
    pool_k = jnp.transpose(pool_k, (0, 1, 3, 4, 2)).reshape(1, n_pool, W_B, PAGE_SIZE)
    pool_v = jnp.transpose(pool_v, (0, 1, 3, 4, 2)).reshape(1, n_pool, W_B, PAGE_SIZE)
    pool_ki = jnp.transpose(pool_ki, (0, 1, 3, 2))
    npg = page_table.shape[1]
    past = npg * PAGE_SIZE
    topk = min(TOPK_MAX, (past + t) // 4)
    gs, ga = math.gcd(npg, SCORE_PAGES), math.gcd(npg, ATTN_PAGES)
    pad = PAGE_SIZE - t
    s_new = jnp.pad(s.reshape(nb, t, LANES), ((0, 0), (0, pad), (0, 0)))
    k_new = jnp.pad(k.reshape(nb, t, W_B), ((0, 0), (0, pad), (0, 0)))
    v_new = jnp.pad(v.reshape(nb, t, W_B), ((0, 0), (0, pad), (0, 0)))
    lastp = npg - 1
    ptot = npg + gs

    def page_spec(tail, j, group):
        zeros = (0,) * len(tail)
        return pl.BlockSpec((None, None) + tail,
                            lambda b, p, pt: (0, pt[b, jnp.minimum(p * group + j, lastp)]) + zeros)

    scores = pl.pallas_call(
        _dsa_sample_score_kernel,
        grid_spec=pltpu.PrefetchScalarGridSpec(
            num_scalar_prefetch=1,
            grid=(nb, npg // gs + 1),
            in_specs=[pl.BlockSpec((t, 2 * W_B), lambda b, p, pt: (b, 0)),
                      pl.BlockSpec((t, LANES), lambda b, p, pt: (b, 0)),
                      pl.BlockSpec((None, PAGE_SIZE, LANES), lambda b, p, pt: (b, 0, 0))]
            + [page_spec((D_IDX, PAGE_SIZE), j, gs) for j in range(gs)],
            out_specs=pl.BlockSpec((None, gs, t, PAGE_SIZE), lambda b, p, pt: (b, p, 0, 0))),
        out_shape=jax.ShapeDtypeStruct((nb, ptot, t, PAGE_SIZE), F32),
        compiler_params=_cparams("parallel", "arbitrary"),
        name="dsa_sample_scores",
    )(page_table, bq, s, s_new, *([pool_ki] * gs))

    kv_tail = (W_B, PAGE_SIZE)
    return pl.pallas_call(
        functools.partial(_dsa_sample_attn_kernel, topk=topk, npg=npg, G=ga),
        grid_spec=pltpu.PrefetchScalarGridSpec(
            num_scalar_prefetch=1,
            grid=(nb, npg // ga + 1),
            in_specs=[pl.BlockSpec((t, 2 * W_B), lambda b, p, pt: (b, 0)),
                      pl.BlockSpec((None, ptot, t, PAGE_SIZE), lambda b, p, pt: (b, 0, 0, 0)),
                      pl.BlockSpec((None, PAGE_SIZE, W_B), lambda b, p, pt: (b, 0, 0)),
                      pl.BlockSpec((None, PAGE_SIZE, W_B), lambda b, p, pt: (b, 0, 0))]
            + [page_spec(kv_tail, j, ga) for j in range(ga)]
            + [page_spec(kv_tail, j, ga) for j in range(ga)],
            out_specs=pl.BlockSpec((t, W_B), lambda b, p, pt: (b, 0)),
            scratch_shapes=[pltpu.VMEM((t, LANES), I32), pltpu.VMEM((t, LANES), I32),
                            pltpu.VMEM((H_B * t, LANES), F32), pltpu.VMEM((H_B * t, LANES), F32),
                            pltpu.VMEM((H_B * t, W_B), F32)]),
        out_shape=jax.ShapeDtypeStruct((nb * t, W_B), F32),
        compiler_params=_cparams("parallel", "arbitrary"),
        name="dsa_sample_attn",
    )(page_table, bq, scores, k_new, v_new, *([pool_k] * ga), *([pool_v] * ga))


def _moe(x, prm, layer):
    wg, wu, wd = prm["moe_wg"][layer], prm["moe_wu"][layer], prm["moe_wd"][layer]
    g, b = prm["ln_g"][layer, 1], prm["ln_b"][layer, 1]
    if wg.dtype == BF16:
        return _moe_ln_grouped(x, prm["router_wt"], prm["router_bt"], wg, wu, wd, g, b)
    return _moe_ln(x, prm["router_w"], prm["router_b"], wg, wu, wd, g, b)


def _pad_chunks(a, nb, t, fill=None):
    c = a.shape[1]
    a3 = a.reshape(nb, t, c)
    if fill is None:
        a3 = jnp.pad(a3, ((0, 0), (0, CHUNK - t), (0, 0)))
    else:
        a3 = jnp.concatenate([a3, jnp.broadcast_to(fill, (nb, CHUNK - t, c))], axis=1)
    return a3.reshape(nb * CHUNK, c)


def _retention_tables(t_true, pos0, t_pad):
    L = CHUNK
    lt = min(L, t_true)
    lg = jnp.log1p(-jnp.exp2(-5.0 - jnp.arange(H_D, dtype=F32)))
    j = jnp.arange(L, dtype=F32)
    causal = jnp.tril(jnp.ones((L, L), dtype=bool))
    dmat = jnp.exp(jnp.where(causal, (j[:, None] - j[None, :]) * lg[:, None, None], -jnp.inf))
    qdec = jnp.exp((j + 1.0) * lg[:, None])
    kdec = jnp.where(j < lt, jnp.exp((lt - 1.0 - j) * lg[:, None]), 0.0)
    sdec = jnp.broadcast_to(jnp.exp(lt * lg)[:, None], (H_D, L))
    dec = jnp.zeros((H_D, L, LANES), F32)
    dec = dec.at[:, :, 0].set(qdec).at[:, :, 1].set(kdec).at[:, :, 2].set(sdec)
    half = DK_D // 2
    freq = ROPE_BASE ** (-jnp.arange(half, dtype=F32) / half)
    pos = (pos0 + jnp.arange(t_pad)).astype(F32)
    ang = pos[:, None] * freq[None, :]
    cos, sin = jnp.cos(ang), jnp.sin(ang)
    return dmat, dec, jnp.concatenate([cos, cos], -1), jnp.concatenate([-sin, sin], -1)


def _forward(x3, st, prm):
    nb, t, d = x3.shape
    n = nb * t
    x = x3.reshape(n, d)
    short = t < CHUNK
    t_pad = CHUNK if short else t
    nc = t_pad // CHUNK
    new = {}
    odt = prm["w_in_e"].dtype

    a, bq, k, v, s, kb, vb = _inproj_e(x, prm["w_in_e"], prm["bias_e"])
    if st is None:
        c0 = jnp.zeros((nb, H_A, DK_A, DV_A), F32)
        n0 = jnp.zeros((nb, H_A, DK_A), F32)
        m0 = jnp.zeros((nb, H_A), F32)
    else:
        c0, n0, m0 = st["mlstm_C"][0], st["mlstm_n"][0], st["mlstm_m"][0]
    m0b = jnp.broadcast_to(m0[:, :, None], (nb, H_A, LANES))
    if short:
        lane = jnp.arange(LANES)
        fill = jnp.where((lane >= S_IG) & (lane < S_IG + H_A), NEG,
                         jnp.where((lane >= S_FG) & (lane < S_FG + H_A), 1e4, 0.0)).astype(F32)
        a_m, s_m = _pad_chunks(a, nb, t), _pad_chunks(s, nb, t, fill)
    else:
        a_m, s_m = a, s
    ya, c1, n1, m1 = _mlstm(a_m, s_m, prm["a_norm_g"], c0, n0, m0b, nb, nc, odt)
    if short:
        ya = ya.reshape(nb, CHUNK, W_A)[:, :t].reshape(n, W_A)
    if st is None:
        yb = _dsa_prompt(bq, s, kb, vb, nb, t)
    else:
        yb = _dsa_sample(bq, s, k, v, st["pool_k"], st["pool_v"], st["pool_ki"], st["page_table"], nb, t)
    new["mlstm_C"], new["mlstm_n"], new["mlstm_m"] = c1[None], n1[None], m1[None, :, :, 0]
    new["k"] = k.reshape(1, nb, t, H_B, DH_B)
    new["v"] = v.reshape(1, nb, t, H_B, DH_B)
    new["kidx"] = s[:, S_KI:S_KI + D_IDX].reshape(1, nb, t, D_IDX)
    x = _outproj_ln(ya, yb, prm["w_out_e"], x, prm["ln_g"][0, 0], prm["ln_b"][0, 0])
    x = _moe(x, prm, 0)

    c, dd = _inproj_o(x, prm["w_in_o"])
    if st is None:
        conv0 = jnp.zeros((nb, SUBLANES, W_C), F32)
        h0 = jnp.zeros((nb, SUBLANES, W_C), F32)
        s0 = jnp.zeros((nb, H_D, DK_D, DV_D), F32)
        pos0 = 0
    else:
        conv0 = jnp.pad(st["conv"][0], ((0, 0), (SUBLANES - (CONV_W - 1), 0), (0, 0)))
        h0 = jnp.broadcast_to(st["lru_h"][0][:, None, :], (nb, SUBLANES, W_C))
        s0 = st["ret_S"][0]
        pos0 = st["page_table"].shape[1] * PAGE_SIZE
    tb = _row_block(t, 256)
    yc, hl = _rglru(c, prm["c_conv_w"], prm["c_conv_b"], prm["c_wa"], prm["c_ba"], prm["c_wx"], prm["c_bx"],
                    prm["c_lambda"], conv0, h0, nb, t // tb, tb, odt)
    dmat, dec, cos2, sin2 = _retention_tables(t, pos0, t_pad)
    d_m = _pad_chunks(dd, nb, t) if short else dd
    yd, s1 = _retention(d_m, cos2, sin2, dmat, dec, prm["d_norm_g"], s0, nb, nc, odt)
    if short:
        yd = yd.reshape(nb, CHUNK, W_D)[:, :t].reshape(n, W_D)
    new["conv"] = c[:, 0:W_C].reshape(nb, t, W_C)[None, :, t - (CONV_W - 1):]
    new["lru_h"] = hl[None, :, 0]
    new["ret_S"] = s1[None]
    x = _outproj_ln(yc, yd, prm["w_out_o"], x, prm["ln_g"][1, 0], prm["ln_b"][1, 0])
    x = _moe(x, prm, 1)
    return x.reshape(nb, t, d), new


def _block_diag(w):
    nblk, blk, _ = w.shape
    eye = jnp.eye(nblk, dtype=w.dtype)
    return (eye[:, None, :, None] * w[:, :, None, :]).reshape(nblk * blk, nblk * blk)


def _prepare_params(w_in_e, b_if_e, a_norm_g, w_out_e, w_in_o, c_conv_w, c_conv_b, c_wa, c_ba, c_wx, c_bx,
                    c_lambda, d_norm_g, w_out_o, router_w, router_bias, moe_w_gate, moe_w_up, moe_w_down,
                    ln_g, ln_b, wdt):
    d = w_in_e.shape[1]
    o = [0]
    for width in (W_A, W_A, W_A, W_A, 2 * H_A, W_B, W_B, W_B, H_IDX * D_IDX, D_IDX, H_IDX):
        o.append(o[-1] + width)
    we = w_in_e[0]
    col = lambda i: we[:, o[i]:o[i + 1]]
    slab = jnp.zeros((d, LANES), F32)
    slab = slab.at[:, S_KI:S_KI + D_IDX].set(col(9)).at[:, S_IG:S_IG + 2 * H_A].set(col(4))
    slab = slab.at[:, S_WI:S_WI + H_IDX].set(col(10))
    w_e = jnp.concatenate([col(0), col(1), col(2), col(3), col(5), col(8), col(6), col(7), slab], axis=1)
    bias_e = jnp.zeros((1, LANES), F32).at[0, S_IG:S_IG + 2 * H_A].set(b_if_e[0])
    rw = jnp.zeros((d, LANES), F32).at[:, :N_EXPERTS].set(router_w)
    rb = jnp.zeros((1, LANES), F32).at[0, :N_EXPERTS].set(router_bias)
    row = lambda a: a.reshape(1, -1).astype(F32)
    return {
        "w_in_e": w_e.astype(wdt), "bias_e": bias_e, "a_norm_g": row(a_norm_g[0]),
        "w_out_e": w_out_e[0].astype(wdt), "w_in_o": w_in_o[0].astype(wdt),
        "c_conv_w": c_conv_w[0].astype(F32), "c_conv_b": row(c_conv_b[0]),
        "c_wa": _block_diag(c_wa[0]).astype(wdt), "c_ba": row(c_ba[0]),
        "c_wx": _block_diag(c_wx[0]).astype(wdt), "c_bx": row(c_bx[0]),
        "c_lambda": row(c_lambda[0]), "d_norm_g": row(d_norm_g[0]), "w_out_o": w_out_o[0].astype(wdt),
        "router_w": rw, "router_b": rb,
        "router_wt": router_w.T.astype(F32), "router_bt": jnp.broadcast_to(rb.reshape(LANES, 1), (LANES, LANES)),
        "moe_wg": moe_w_gate.astype(wdt), "moe_wu": moe_w_up.astype(wdt), "moe_wd": moe_w_down.astype(wdt),
        "ln_g": ln_g.reshape(DEPTH, 2, 1, -1).astype(F32), "ln_b": ln_b.reshape(DEPTH, 2, 1, -1).astype(F32),
    }


def kernel(x_prompt, x_sample, state_mlstm_C, state_mlstm_n, state_mlstm_m, cache_k, cache_v, cache_kidx,
           page_table, state_conv, state_lru_h, state_ret_S, w_in_e, b_if_e, a_norm_g, w_out_e, w_in_o,
           c_conv_w, c_conv_b, c_wa, c_ba, c_wx, c_bx, c_lambda, d_norm_g, w_out_o, router_w, router_bias,
           moe_w_gate, moe_w_up, moe_w_down, ln_g, ln_b):
    weights = (w_in_e, b_if_e, a_norm_g, w_out_e, w_in_o, c_conv_w, c_conv_b, c_wa, c_ba, c_wx, c_bx, c_lambda,
               d_norm_g, w_out_o, router_w, router_bias, moe_w_gate, moe_w_up, moe_w_down, ln_g, ln_b)
    prm_prompt = _prepare_params(*weights, BF16)
    prm_sample = _prepare_params(*weights, F32)
    st = {"mlstm_C": state_mlstm_C, "mlstm_n": state_mlstm_n, "mlstm_m": state_mlstm_m,
          "pool_k": cache_k, "pool_v": cache_v, "pool_ki": cache_kidx, "page_table": page_table,
          "conv": state_conv, "lru_h": state_lru_h, "ret_S": state_ret_S}
    y_p, nsp = _forward(x_prompt, None, prm_prompt)
    y_s, nss = _forward(x_sample, st, prm_sample)
    names = ("mlstm_C", "mlstm_n", "mlstm_m", "k", "v", "kidx", "conv", "lru_h", "ret_S")
    return (y_p, y_s) + tuple(nsp[k] for k in names) + tuple(nss[k] for k in names)
```

```python
import functools
import math

import jax
import jax.numpy as jnp
from jax import lax
from jax.experimental import pallas as pl
from jax.experimental.pallas import tpu as pltpu

F32 = jnp.float32
BF16 = jnp.bfloat16
I32 = jnp.int32

DEPTH = 2
PAGE_SIZE = 128
H_A, DK_A, DV_A = 4, 128, 128
W_A = H_A * DV_A
H_B, DH_B = 8, 64
W_B = H_B * DH_B
H_IDX, D_IDX = 8, 64
TOPK_MAX = 256
W_C, N_BLK_C, CONV_W, LRU_C = 512, 8, 4, 8.0
BLK_C = W_C // N_BLK_C
H_D, DK_D, DV_D = 4, 128, 128
W_D = H_D * DV_D
ROPE_BASE = 10000.0
N_EXPERTS, N_GROUPS, TOP_K_EXP, D_FF_EXP = 16, 4, 2, 512
EXP_PER_GROUP = N_EXPERTS // N_GROUPS
ALPHA = (2 * DEPTH) ** 0.25
LN_EPS = 1e-5
HN_EPS = 1e-6

CHUNK = 128
DSA_QB = 256
LANES = 128
SUBLANES = 8
NEG = -1e30
INT_MIN = -2 ** 31
VMEM_LIMIT = 56 * 1024 * 1024

S_KI = 0
S_IG = 64
S_FG = 68
S_WI = 72


def _cparams(*sem):
    return pltpu.CompilerParams(dimension_semantics=sem, vmem_limit_bytes=VMEM_LIMIT)


def _precision(a, b):
    return lax.Precision.HIGHEST if a.dtype == F32 and b.dtype == F32 else None


def _dot(a, b):
    return jnp.dot(a, b, preferred_element_type=F32, precision=_precision(a, b))


def _dot_nt(a, b):
    return lax.dot_general(a, b, (((1,), (1,)), ((), ())), preferred_element_type=F32, precision=_precision(a, b))


def _dot_tn(a, b):
    return lax.dot_general(a, b, (((0,), (0,)), ((), ())), preferred_element_type=F32, precision=_precision(a, b))


def _split2(a):
    hi = a.astype(BF16)
    return hi, (a - hi.astype(F32)).astype(BF16)


def _dot3(a2, b2, dot):
    m = a2[0].shape[0]
    both = dot(jnp.concatenate([a2[0], a2[1]], axis=0), b2[0])
    return both[0:m] + (both[m:2 * m] + dot(a2[0], b2[1]))


def _row_block(n, target):
    t = min(n, target)
    while n % t:
        t //= 2
    return t


def _layer_norm(z, g, b):
    mu = jnp.mean(z, -1, keepdims=True)
    zc = z - mu
    var = jnp.mean(zc * zc, -1, keepdims=True)
    return zc * lax.rsqrt(var + LN_EPS) * g + b


def _head_norm(h):
    mu = jnp.mean(h, -1, keepdims=True)
    hc = h - mu
    var = jnp.mean(hc * hc, -1, keepdims=True)
    return hc * lax.rsqrt(var + HN_EPS)


def _inproj_e_kernel(x_ref, w_ref, bias_ref, a_ref, bq_ref, k_ref, v_ref, s_ref, kb_ref, vb_ref):
    x = x_ref[...].astype(w_ref.dtype)

    def mm(lo, hi):
        return _dot(x, w_ref[:, lo:hi])

    a_ref[:, 0:W_A] = mm(0, W_A)
    a_ref[:, W_A:2 * W_A] = mm(W_A, 2 * W_A) * DK_A ** -0.5
    a_ref[:, 2 * W_A:4 * W_A] = mm(2 * W_A, 4 * W_A)
    o = 4 * W_A
    bq_ref[...] = mm(o, o + 2 * W_B)
    o += 2 * W_B
    k = mm(o, o + W_B)
    k_ref[...] = k
    kb_ref[...] = k.astype(kb_ref.dtype)
    o += W_B
    v = mm(o, o + W_B)
    v_ref[...] = v
    vb_ref[...] = v.astype(vb_ref.dtype)
    o += W_B
    s_ref[...] = mm(o, o + LANES) + bias_ref[...]


def _inproj_e(x, w, bias):
    n, d = x.shape
    tm = _row_block(n, 256)
    wcols = w.shape[1]
    outs = [(4 * W_A, F32), (2 * W_B, F32), (W_B, F32), (W_B, F32), (LANES, F32), (W_B, w.dtype), (W_B, w.dtype)]
    return pl.pallas_call(
        _inproj_e_kernel,
        grid=(n // tm,),
        in_specs=[pl.BlockSpec((tm, d), lambda i: (i, 0)),
                  pl.BlockSpec((d, wcols), lambda i: (0, 0)),
                  pl.BlockSpec((1, LANES), lambda i: (0, 0))],
        out_specs=[pl.BlockSpec((tm, c), lambda i: (i, 0)) for c, _ in outs],
        out_shape=[jax.ShapeDtypeStruct((n, c), dt) for c, dt in outs],
        compiler_params=_cparams("parallel"),
        name="inproj_even",
    )(x, w, bias)


def _inproj_o_kernel(x_ref, w_ref, c_ref, d_ref):
    x = x_ref[...].astype(w_ref.dtype)
    c_ref[...] = _dot(x, w_ref[:, 0:2 * W_C])
    d_ref[...] = _dot(x, w_ref[:, 2 * W_C:2 * W_C + 4 * W_D])


def _inproj_o(x, w):
    n, d = x.shape
    tm = _row_block(n, 256)
    return pl.pallas_call(
        _inproj_o_kernel,
        grid=(n // tm,),
        in_specs=[pl.BlockSpec((tm, d), lambda i: (i, 0)),
                  pl.BlockSpec(w.shape, lambda i: (0, 0))],
        out_specs=[pl.BlockSpec((tm, 2 * W_C), lambda i: (i, 0)),
                   pl.BlockSpec((tm, 4 * W_D), lambda i: (i, 0))],
        out_shape=[jax.ShapeDtypeStruct((n, 2 * W_C), F32), jax.ShapeDtypeStruct((n, 4 * W_D), F32)],
        compiler_params=_cparams("parallel"),
        name="inproj_odd",
    )(x, w)


def _outproj_ln_kernel(y1_ref, y2_ref, w_ref, x_ref, g_ref, b_ref, o_ref):
    half = y1_ref.shape[1]
    y = _dot(y1_ref[...], w_ref[0:half, :]) + _dot(y2_ref[...], w_ref[half:2 * half, :])
    o_ref[...] = _layer_norm(ALPHA * x_ref[...] + y, g_ref[...], b_ref[...])


def _outproj_ln(y1, y2, w, x, g, b):
    n, d = x.shape
    tm = _row_block(n, 512)
    half = y1.shape[1]
    return pl.pallas_call(
        _outproj_ln_kernel,
        grid=(n // tm,),
        in_specs=[pl.BlockSpec((tm, half), lambda i: (i, 0)),
                  pl.BlockSpec((tm, half), lambda i: (i, 0)),
                  pl.BlockSpec(w.shape, lambda i: (0, 0)),
                  pl.BlockSpec((tm, d), lambda i: (i, 0)),
                  pl.BlockSpec((1, d), lambda i: (0, 0)),
                  pl.BlockSpec((1, d), lambda i: (0, 0))],
        out_specs=pl.BlockSpec((tm, d), lambda i: (i, 0)),
        out_shape=jax.ShapeDtypeStruct((n, d), F32),
        compiler_params=_cparams("parallel"),
        name="outproj_ln",
    )(y1, y2, w, x, g, b)


def _route(logits, bias):
    lane = lax.broadcasted_iota(I32, logits.shape, 1)
    valid = lane < N_EXPERTS
    pos = lane % EXP_PER_GROUP
    grp = (lane // EXP_PER_GROUP).astype(F32)
    s = jax.nn.sigmoid(logits)
    sel = jnp.where(valid, s + bias, NEG)
    rank = jnp.zeros(logits.shape, F32)
    for d in range(1, EXP_PER_GROUP):
        lo = pltpu.roll(sel, d, 1)
        hi = pltpu.roll(sel, LANES - d, 1)
        rank = rank + jnp.where(jnp.logical_and(pos >= d, lo >= sel), 1.0, 0.0)
        rank = rank + jnp.where(jnp.logical_and(pos + d < EXP_PER_GROUP, hi > sel), 1.0, 0.0)
    top2 = jnp.logical_and(rank < TOP_K_EXP, valid)
    contrib = jnp.where(top2, sel, 0.0)
    gs = contrib
    for d in range(1, EXP_PER_GROUP):
        lo = pltpu.roll(contrib, d, 1)
        hi = pltpu.roll(contrib, LANES - d, 1)
        gs = gs + jnp.where(pos >= d, lo, 0.0) + jnp.where(pos + d < EXP_PER_GROUP, hi, 0.0)
    gs = jnp.where(valid, gs, NEG)
    gmax = jnp.max(gs, axis=1, keepdims=True)
    best = jnp.min(jnp.where(gs == gmax, grp, 1e9), axis=1, keepdims=True)
    chosen = jnp.logical_and(top2, grp == best)
    s_sel = jnp.where(chosen, s, 0.0)
    return s_sel / jnp.sum(s_sel, axis=1, keepdims=True)


def _moe_kernel(x_ref, rw_ref, rb_ref, wg_ref, wu_ref, wd_ref, g_ref, b_ref, o_ref,
                xb_scr, comb_scr, acc_scr, *, sub):
    e = pl.program_id(1)
    tm = x_ref.shape[0]

    @pl.when(e == 0)
    def _():
        xb_scr[...] = x_ref[...].astype(xb_scr.dtype)
        logits = jnp.dot(x_ref[...], rw_ref[...], precision=lax.Precision.HIGHEST, preferred_element_type=F32)
        comb_scr[...] = _route(logits, rb_ref[...])
        acc_scr[...] = jnp.zeros_like(acc_scr)

    def rows(r, carry):
        r0 = pl.multiple_of(r * sub, sub)
        xb = xb_scr[pl.ds(r0, sub), :]
        gate = _dot(xb, wg_ref[...])
        h = gate * jax.nn.sigmoid(gate) * _dot(xb, wu_ref[...])
        y = _dot(h.astype(wd_ref.dtype), wd_ref[...])
        comb = comb_scr[pl.ds(r0, sub), :]
        lane = lax.broadcasted_iota(I32, comb.shape, 1)
        c_e = jnp.sum(jnp.where(lane == e, comb, 0.0), axis=1, keepdims=True)
        acc_scr[pl.ds(r0, sub), :] += c_e * y
        return carry

    lax.fori_loop(0, tm // sub, rows, 0)

    @pl.when(e == N_EXPERTS - 1)
    def _():
        o_ref[...] = _layer_norm(ALPHA * x_ref[...] + acc_scr[...], g_ref[...], b_ref[...])


def _moe_ln(x, rw, rb, wg, wu, wd, g, b):
    n, d = x.shape
    tm = _row_block(n, 1024)
    sub = _row_block(tm, 256)
    f = wg.shape[2]
    return pl.pallas_call(
        functools.partial(_moe_kernel, sub=sub),
        grid=(n // tm, N_EXPERTS),
        in_specs=[pl.BlockSpec((tm, d), lambda i, e: (i, 0)),
                  pl.BlockSpec((d, LANES), lambda i, e: (0, 0)),
                  pl.BlockSpec((1, LANES), lambda i, e: (0, 0)),
                  pl.BlockSpec((None, d, f), lambda i, e: (e, 0, 0)),
                  pl.BlockSpec((None, d, f), lambda i, e: (e, 0, 0)),
                  pl.BlockSpec((None, f, d), lambda i, e: (e, 0, 0)),
                  pl.BlockSpec((1, d), lambda i, e: (0, 0)),
                  pl.BlockSpec((1, d), lambda i, e: (0, 0))],
        out_specs=pl.BlockSpec((tm, d), lambda i, e: (i, 0)),
        out_shape=jax.ShapeDtypeStruct((n, d), F32),
        scratch_shapes=[pltpu.VMEM((tm, d), wg.dtype), pltpu.VMEM((tm, LANES), F32), pltpu.VMEM((tm, d), F32)],
        compiler_params=_cparams("parallel", "arbitrary"),
        name="moe_ln",
    )(x, rw, rb, wg, wu, wd, g, b)


def _route_t(logits, bias):
    s = jax.nn.sigmoid(logits)
    sel = s + bias
    top2, gscore = [], []
    for g in range(N_GROUPS):
        v = [sel[g * EXP_PER_GROUP + i:g * EXP_PER_GROUP + i + 1] for i in range(EXP_PER_GROUP)]
        flags = []
        for i in range(EXP_PER_GROUP):
            rank = jnp.zeros(v[i].shape, F32)
            for j in range(EXP_PER_GROUP):
                if j < i:
                    rank = rank + jnp.where(v[j] >= v[i], 1.0, 0.0)
                elif j > i:
                    rank = rank + jnp.where(v[j] > v[i], 1.0, 0.0)
            flags.append(rank < TOP_K_EXP)
        gs = jnp.where(flags[0], v[0], 0.0)
        for i in range(1, EXP_PER_GROUP):
            gs = gs + jnp.where(flags[i], v[i], 0.0)
        top2.append(flags)
        gscore.append(gs)
    gmax = gscore[0]
    for g in range(1, N_GROUPS):
        gmax = jnp.maximum(gmax, gscore[g])
    chosen, taken = [], None
    for g in range(N_GROUPS):
        c = gscore[g] == gmax
        if taken is not None:
            c = jnp.logical_and(c, jnp.logical_not(taken))
        taken = c if taken is None else jnp.logical_or(taken, c)
        chosen.append(c)
    gates = []
    for g in range(N_GROUPS):
        for i in range(EXP_PER_GROUP):
            e = g * EXP_PER_GROUP + i
            gates.append(jnp.where(jnp.logical_and(chosen[g], top2[g][i]), s[e:e + 1], 0.0))
    den = gates[0]
    for gt in gates[1:]:
        den = den + gt
    return [gt / den for gt in gates], chosen


MOE_BLOCK = 1024
MOE_TILE = 256
MOE_TAIL = 64

R_CHOSEN = EXP_PER_GROUP
R_RANK = EXP_PER_GROUP + 1


def _moe_group_kernel(x_ref, rwt_ref, rbt_ref, tri_ref, wg_ref, wu_ref, wd_ref, g_ref, b_ref, o_ref,
                      xb_scr, info_scr, acc_scr):
    g = pl.program_id(1)
    tb = x_ref.shape[0]
    mdt = xb_scr.dtype

    @pl.when(g == 0)
    def _():
        xb_scr[...] = x_ref[...].astype(mdt)
        logits = lax.dot_general(rwt_ref[...], x_ref[...], (((1,), (1,)), ((), ())),
                                 precision=lax.Precision.HIGHEST, preferred_element_type=F32)
        comb, chosen = _route_t(logits, rbt_ref[0:N_EXPERTS, 0:1])
        flags = [jnp.where(c, 1.0, 0.0) for c in chosen]
        pad = jnp.zeros((SUBLANES - N_GROUPS, tb), F32)
        prefix = _dot(jnp.concatenate(flags + [pad], axis=0).astype(BF16), tri_ref[...])
        fill = jnp.zeros((SUBLANES - EXP_PER_GROUP - 2, tb), F32)
        for gg in range(N_GROUPS):
            rows = comb[gg * EXP_PER_GROUP:(gg + 1) * EXP_PER_GROUP] + [flags[gg], prefix[gg:gg + 1] - 1.0, fill]
            info_scr[gg] = jnp.concatenate(rows, axis=0)
        acc_scr[...] = jnp.zeros_like(acc_scr)

    info = info_scr[g]
    chosen_row = info[R_CHOSEN:R_CHOSEN + 1] > 0.5
    rank_row = info[R_RANK:R_RANK + 1]
    count = jnp.max(jnp.where(chosen_row, rank_row + 1.0, 0.0)).astype(I32)

    def tile(first, rows):
        rid = (lax.broadcasted_iota(I32, (rows, tb), 0) + first).astype(F32)
        onehot = jnp.where(jnp.logical_and(rank_row == rid, chosen_row), 1.0, 0.0)
        pb = onehot.astype(mdt)
        xg = _dot(pb, xb_scr[...]).astype(mdt)
        out = jnp.zeros((rows, x_ref.shape[1]), F32)
        for i in range(EXP_PER_GROUP):
            gate = jnp.sum(onehot * info[i:i + 1], axis=1, keepdims=True)
            gt = _dot(xg, wg_ref[i])
            h = gt * jax.nn.sigmoid(gt) * _dot(xg, wu_ref[i])
            out = out + gate * _dot(h.astype(mdt), wd_ref[i])
        acc_scr[...] += _dot_tn(pb, out.astype(mdt))

    def full_tile(s, carry):
        tile(s * MOE_TILE, MOE_TILE)
        return carry

    nfull = count // MOE_TILE
    rem = count - nfull * MOE_TILE
    lax.fori_loop(0, nfull, full_tile, 0)

    @pl.when(rem > MOE_TAIL)
    def _():
        tile(nfull * MOE_TILE, MOE_TILE)

    @pl.when(jnp.logical_and(rem > 0, rem <= MOE_TAIL))
    def _():
        tile(nfull * MOE_TILE, MOE_TAIL)

    @pl.when(g == N_GROUPS - 1)
    def _():
        o_ref[...] = _layer_norm(ALPHA * x_ref[...] + acc_scr[...], g_ref[...], b_ref[...])


def _moe_ln_grouped(x, rwt, rbt, wg, wu, wd, g, b):
    n, d = x.shape
    tb = _row_block(n, MOE_BLOCK)
    f = wg.shape[2]
    tri = jnp.triu(jnp.ones((tb, tb), BF16))
    once = pl.Buffered(1)
    return pl.pallas_call(
        _moe_group_kernel,
        grid=(n // tb, N_GROUPS),
        in_specs=[pl.BlockSpec((tb, d), lambda i, q: (i, 0), pipeline_mode=once),
                  pl.BlockSpec((N_EXPERTS, d), lambda i, q: (0, 0), pipeline_mode=once),
                  pl.BlockSpec((LANES, LANES), lambda i, q: (0, 0), pipeline_mode=once),
                  pl.BlockSpec((tb, tb), lambda i, q: (0, 0), pipeline_mode=once),
                  pl.BlockSpec((EXP_PER_GROUP, d, f), lambda i, q: (q, 0, 0)),
                  pl.BlockSpec((EXP_PER_GROUP, d, f), lambda i, q: (q, 0, 0)),
                  pl.BlockSpec((EXP_PER_GROUP, f, d), lambda i, q: (q, 0, 0)),
                  pl.BlockSpec((1, d), lambda i, q: (0, 0)),
                  pl.BlockSpec((1, d), lambda i, q: (0, 0))],
        out_specs=pl.BlockSpec((tb, d), lambda i, q: (i, 0), pipeline_mode=once),
        out_shape=jax.ShapeDtypeStruct((n, d), F32),
        scratch_shapes=[pltpu.VMEM((tb, d), wg.dtype), pltpu.VMEM((N_GROUPS, SUBLANES, tb), F32),
                        pltpu.VMEM((tb, d), F32)],
        compiler_params=_cparams("parallel", "arbitrary"),
        name="moe_group_ln",
    )(x, rwt, rbt, tri, wg, wu, wd, g, b)


def _mlstm_kernel(a_ref, s_ref, g_ref, c0_ref, n0_ref, m0_ref, y_ref, c1_ref, n1_ref, m1_ref,
                  c_scr, n_scr, m_scr):
    c = pl.program_id(1)
    L = a_ref.shape[0]
    mdt = y_ref.dtype

    @pl.when(c == 0)
    def _():
        c_scr[...] = c0_ref[...]
        n_scr[...] = n0_ref[...]
        m_scr[...] = m0_ref[...]

    S = s_ref[...]
    lane = lax.broadcasted_iota(I32, S.shape, 1)
    is_f = jnp.logical_and(lane >= S_FG, lane < S_FG + H_A)
    lf = jnp.where(is_f, jax.nn.log_sigmoid(S), 0.0)
    row = lax.broadcasted_iota(I32, (L, L), 0)
    col = lax.broadcasted_iota(I32, (L, L), 1)
    causal = row >= col
    Fs = jnp.dot(causal.astype(F32), lf, precision=lax.Precision.HIGHEST, preferred_element_type=F32)
    Fa = pltpu.roll(Fs, LANES - (S_FG - S_IG), 1)
    AT = jnp.transpose(S - Fa)

    for h in range(H_A):
        ig = S[:, S_IG + h:S_IG + h + 1]
        F = Fa[:, S_IG + h:S_IG + h + 1]
        a_row = AT[S_IG + h:S_IG + h + 1, :]
        m_prev = m_scr[h:h + 1, 0:1]
        cm = jnp.max(jnp.where(causal, a_row, NEG), axis=1, keepdims=True)
        m_t = F + jnp.maximum(m_prev, cm)
        dmat = jnp.exp(jnp.where(causal, (F - m_t) + a_row, NEG))
        inter = jnp.exp(F + m_prev - m_t)
        q = a_ref[:, h * DK_A:(h + 1) * DK_A].astype(mdt)
        kf = a_ref[:, W_A + h * DK_A:W_A + (h + 1) * DK_A]
        k = kf.astype(mdt)
        v = a_ref[:, 2 * W_A + h * DV_A:2 * W_A + (h + 1) * DV_A].astype(mdt)
        C = c_scr[h]
        n = n_scr[h:h + 1, :]
        s = _dot_nt(q, k) * dmat
        num = _dot(s.astype(mdt), v) + inter * _dot(q, C.astype(mdt))
        qn = jnp.sum(q.astype(F32) * n.astype(mdt).astype(F32), axis=1, keepdims=True)
        den = jnp.sum(s, axis=1, keepdims=True) + inter * qn
        hout = num / jnp.maximum(jnp.abs(den), jnp.exp(-m_t))
        o = a_ref[:, 3 * W_A + h * DV_A:3 * W_A + (h + 1) * DV_A]
        y = _head_norm(hout) * g_ref[:, h * DV_A:(h + 1) * DV_A] * jax.nn.sigmoid(o)
        y_ref[:, h * DV_A:(h + 1) * DV_A] = y.astype(y_ref.dtype)
        m_new = m_t[L - 1:L, :]
        F_last = F[L - 1:L, :]
        w_s = jnp.exp(F_last - F + ig - m_new)
        decay = jnp.exp(F_last + m_prev - m_new)
        kw = kf * w_s
        c_scr[h] = decay * C + _dot_tn(kw.astype(mdt), v)
        n_scr[h:h + 1, :] = decay * n + jnp.sum(kw, axis=0, keepdims=True)
        m_scr[h:h + 1, :] = jnp.broadcast_to(m_new, (1, LANES))

    @pl.when(c == pl.num_programs(1) - 1)
    def _():
        c1_ref[...] = c_scr[...]
        n1_ref[...] = n_scr[...]
        m1_ref[...] = m_scr[...]


def _mlstm(a, s, gnorm, c0, n0, m0b, nb, nc, odt):
    n = a.shape[0]
    L = CHUNK
    return pl.pallas_call(
        _mlstm_kernel,
        grid=(nb, nc),
        in_specs=[pl.BlockSpec((L, 4 * W_A), lambda b, c: (b * nc + c, 0)),
                  pl.BlockSpec((L, LANES), lambda b, c: (b * nc + c, 0)),
                  pl.BlockSpec((1, W_A), lambda b, c: (0, 0)),
                  pl.BlockSpec((None, H_A, DK_A, DV_A), lambda b, c: (b, 0, 0, 0)),
                  pl.BlockSpec((None, H_A, DK_A), lambda b, c: (b, 0, 0)),
                  pl.BlockSpec((None, H_A, LANES), lambda b, c: (b, 0, 0))],
        out_specs=[pl.BlockSpec((L, W_A), lambda b, c: (b * nc + c, 0)),
                   pl.BlockSpec((None, H_A, DK_A, DV_A), lambda b, c: (b, 0, 0, 0)),
                   pl.BlockSpec((None, H_A, DK_A), lambda b, c: (b, 0, 0)),
                   pl.BlockSpec((None, H_A, LANES), lambda b, c: (b, 0, 0))],
        out_shape=[jax.ShapeDtypeStruct((n, W_A), odt),
                   jax.ShapeDtypeStruct((nb, H_A, DK_A, DV_A), F32),
                   jax.ShapeDtypeStruct((nb, H_A, DK_A), F32),
                   jax.ShapeDtypeStruct((nb, H_A, LANES), F32)],
        scratch_shapes=[pltpu.VMEM((H_A, DK_A, DV_A), F32), pltpu.VMEM((H_A, DK_A), F32),
                        pltpu.VMEM((H_A, LANES), F32)],
        compiler_params=_cparams("parallel", "arbitrary"),
        name="mlstm",
    )(a, s, gnorm, c0, n0, m0b)


def _retention_kernel(d_ref, cos_ref, sin_ref, dm_ref, dec_ref, g_ref, s0_ref, y_ref, s1_ref, s_scr):
    c = pl.program_id(1)

    @pl.when(c == 0)
    def _():
        s_scr[...] = s0_ref[...]

    cos2 = cos_ref[...]
    sin2 = sin_ref[...]
    mdt = y_ref.dtype
    for h in range(H_D):
        qf = d_ref[:, h * DK_D:(h + 1) * DK_D]
        kf = d_ref[:, W_D + h * DK_D:W_D + (h + 1) * DK_D]
        q = qf * cos2 + pltpu.roll(qf, DK_D // 2, 1) * sin2
        k = (kf * cos2 + pltpu.roll(kf, DK_D // 2, 1) * sin2) * DK_D ** -0.5
        v = d_ref[:, 2 * W_D + h * DV_D:2 * W_D + (h + 1) * DV_D].astype(mdt)
        qdec = dec_ref[h, :, 0:1]
        kdec = dec_ref[h, :, 1:2]
        sdec = dec_ref[h, 0:1, 2:3]
        S = s_scr[h]
        qb = q.astype(mdt)
        att = _dot_nt(qb, k.astype(mdt)) * dm_ref[h]
        o = _dot(att.astype(mdt), v) + qdec * _dot(qb, S.astype(mdt))
        s_scr[h] = sdec * S + _dot_tn((k * kdec).astype(mdt), v)
        gt = d_ref[:, 3 * W_D + h * DV_D:3 * W_D + (h + 1) * DV_D]
        y = _head_norm(o) * g_ref[:, h * DV_D:(h + 1) * DV_D] * (gt * jax.nn.sigmoid(gt))
        y_ref[:, h * DV_D:(h + 1) * DV_D] = y.astype(y_ref.dtype)

    @pl.when(c == pl.num_programs(1) - 1)
    def _():
        s1_ref[...] = s_scr[...]


def _retention(d, cos2, sin2, dmat, dec, gnorm, s0, nb, nc, odt):
    n = d.shape[0]
    L = CHUNK
    return pl.pallas_call(
        _retention_kernel,
        grid=(nb, nc),
        in_specs=[pl.BlockSpec((L, 4 * W_D), lambda b, c: (b * nc + c, 0)),
                  pl.BlockSpec((L, DK_D), lambda b, c: (c, 0)),
                  pl.BlockSpec((L, DK_D), lambda b, c: (c, 0)),
                  pl.BlockSpec((H_D, L, L), lambda b, c: (0, 0, 0)),
                  pl.BlockSpec((H_D, L, LANES), lambda b, c: (0, 0, 0)),
                  pl.BlockSpec((1, W_D), lambda b, c: (0, 0)),
                  pl.BlockSpec((None, H_D, DK_D, DV_D), lambda b, c: (b, 0, 0, 0))],
        out_specs=[pl.BlockSpec((L, W_D), lambda b, c: (b * nc + c, 0)),
                   pl.BlockSpec((None, H_D, DK_D, DV_D), lambda b, c: (b, 0, 0, 0))],
        out_shape=[jax.ShapeDtypeStruct((n, W_D), odt),
                   jax.ShapeDtypeStruct((nb, H_D, DK_D, DV_D), F32)],
        scratch_shapes=[pltpu.VMEM((H_D, DK_D, DV_D), F32)],
        compiler_params=_cparams("parallel", "arbitrary"),
        name="retention",
    )(d, cos2, sin2, dmat, dec, gnorm, s0)


def _shift_rows(x, prev, j):
    tb = x.shape[0]
    xs = pltpu.roll(x, j, 0)
    pr = pltpu.roll(prev, j, 0)
    row = lax.broadcasted_iota(I32, pr.shape, 0)
    first = jnp.where(row < j, pr, xs[0:SUBLANES])
    if tb == SUBLANES:
        return first
    return jnp.concatenate([first, xs[SUBLANES:]], axis=0)


def _rglru_kernel(c_ref, cw_ref, cb_ref, wa_ref, ba_ref, wx_ref, bx_ref, lam_ref, conv0_ref, h0_ref,
                  y_ref, hl_ref, prev_scr, h_scr):
    t = pl.program_id(1)
    tb = c_ref.shape[0]

    @pl.when(t == 0)
    def _():
        prev_scr[...] = conv0_ref[...]
        h_scr[...] = h0_ref[...]

    x = c_ref[:, 0:W_C]
    gate = c_ref[:, W_C:2 * W_C]
    prev = prev_scr[...]
    xc = x * cw_ref[CONV_W - 1:CONV_W, :] + cb_ref[...]
    for j in range(1, CONV_W):
        xc = xc + _shift_rows(x, prev, j) * cw_ref[CONV_W - 1 - j:CONV_W - j, :]
    prev_scr[...] = x[tb - SUBLANES:tb]

    xb = xc.astype(wa_ref.dtype)
    r = jax.nn.sigmoid(_dot(xb, wa_ref[...]) + ba_ref[...])
    i = jax.nn.sigmoid(_dot(xb, wx_ref[...]) + bx_ref[...])
    log_a = -LRU_C * r * jax.nn.softplus(-lam_ref[...])
    A = jnp.exp(log_a)
    th = jnp.tanh(log_a)
    U = jnp.sqrt(-2.0 * th / (1.0 - th)) * (i * xc)
    row = lax.broadcasted_iota(I32, (tb, W_C), 0)
    d = 1
    while d < tb:
        keep = row >= d
        U = jnp.where(keep, U + A * pltpu.roll(U, d, 0), U)
        A = jnp.where(keep, A * pltpu.roll(A, d, 0), A)
        d *= 2
    h = U + A * h_scr[0:1, :]
    h_scr[...] = jnp.broadcast_to(h[tb - 1:tb, :], (SUBLANES, W_C))
    y_ref[...] = (h * jax.nn.gelu(gate)).astype(y_ref.dtype)

    @pl.when(t == pl.num_programs(1) - 1)
    def _():
        hl_ref[...] = h_scr[...]


def _rglru(c, cw, cb, wa, ba, wx, bx, lam, conv0, h0, nb, nt, tb, odt):
    n = c.shape[0]
    vec = pl.BlockSpec((1, W_C), lambda b, t: (0, 0))
    mat = pl.BlockSpec((W_C, W_C), lambda b, t: (0, 0))
    st = pl.BlockSpec((None, SUBLANES, W_C), lambda b, t: (b, 0, 0))
    return pl.pallas_call(
        _rglru_kernel,
        grid=(nb, nt),
        in_specs=[pl.BlockSpec((tb, 2 * W_C), lambda b, t: (b * nt + t, 0)),
                  pl.BlockSpec((CONV_W, W_C), lambda b, t: (0, 0)), vec, mat, vec, mat, vec, vec, st, st],
        out_specs=[pl.BlockSpec((tb, W_C), lambda b, t: (b * nt + t, 0)), st],
        out_shape=[jax.ShapeDtypeStruct((n, W_C), odt), jax.ShapeDtypeStruct((nb, SUBLANES, W_C), F32)],
        scratch_shapes=[pltpu.VMEM((SUBLANES, W_C), F32), pltpu.VMEM((SUBLANES, W_C), F32)],
        compiler_params=_cparams("parallel", "arbitrary"),
        name="rglru",
    )(c, cw, cb, wa, ba, wx, bx, lam, conv0, h0)


def _sort_key(score):
    score = jnp.where(score == 0.0, 0.0, score)
    bits = lax.bitcast_convert_type(score, I32)
    return bits ^ ((bits >> 31) & 0x7FFFFFFF)


def _kth_largest_key(count_ge, k, rows):
    kf = float(k)
    zero = jnp.zeros((rows, 1), I32)
    t0 = jnp.where(count_ge(zero) >= kf, zero, jnp.full((rows, 1), INT_MIN, I32))

    def body(bi, t):
        cand = t + jnp.left_shift(jnp.int32(1), 30 - bi)
        return jnp.where(count_ge(cand) >= kf, cand, t)

    return lax.fori_loop(0, 31, body, t0)


def _key_to_score(key):
    return lax.bitcast_convert_type(key ^ ((key >> 31) & 0x7FFFFFFF), F32)


def _kth_largest_key_between(count_ge, lo, hi, k):
    kf = float(k)
    width = hi.astype(F32) - lo.astype(F32)
    exponent = (lax.bitcast_convert_type(width, I32) >> 23) - 127
    steps = jnp.max(exponent) + 2

    def body(_, carry):
        lo, hi = carry
        mid = (lo >> 1) + (hi >> 1) + (lo & hi & 1)
        ok = count_ge(mid) >= kf
        return jnp.where(ok, mid, lo), jnp.where(ok, hi, mid)

    lo, hi = lax.fori_loop(0, steps, body, (lo, hi))
    return lo


def _first_positions(count_eq_before, need, nbits, rows):
    def body(bi, x):
        cand = x + jnp.left_shift(jnp.int32(1), nbits - 1 - bi)
        return jnp.where(count_eq_before(cand) < need, cand, x)

    return lax.fori_loop(0, nbits, body, jnp.zeros((rows, 1), I32))


def _qi_heads(qi):
    lane = lax.broadcasted_iota(I32, (qi.shape[0], LANES), 1)
    low = lane < D_IDX
    out = []
    for h in range(H_IDX):
        pair = qi[:, (h // 2) * LANES:(h // 2 + 1) * LANES]
        if h % 2:
            pair = pltpu.roll(pair, D_IDX, 1)
        out.append(jnp.where(low, pair, 0.0).astype(BF16))
    return out


def _dsa_prompt_kernel(q_ref, s_ref, ki_ref, kb_ref, vb_ref, o_ref, key_scr, bias_scr, x_scr, thr_scr, lg_scr,
                       *, q_off, kt_size, topk):
    i = pl.program_id(1)
    QB = q_ref.shape[0]
    KT = kt_size
    nkt = key_scr.shape[0]
    lk = nkt * KT
    q0 = q_off + i * QB
    qih = _qi_heads(q_ref[:, W_B:2 * W_B])
    wi = s_ref[:, S_WI:S_WI + H_IDX]
    wcols = [wi[:, h:h + 1] for h in range(H_IDX)]
    rowpos = q0 + lax.broadcasted_iota(I32, (QB, KT), 0)
    colpos = lax.broadcasted_iota(I32, (QB, KT), 1)

    def score_tile(kt, carry):
        k0 = pl.multiple_of(kt * KT, KT)
        ki = ki_ref[pl.ds(k0, KT), :].astype(BF16)
        sc = jnp.zeros((QB, KT), F32)
        for h in range(H_IDX):
            sc = sc + wcols[h] * jnp.maximum(_dot_nt(qih[h], ki), 0.0)
        sc = jnp.where(colpos + k0 <= rowpos, sc, -jnp.inf)
        key_scr[kt] = _sort_key(sc)
        return carry

    lax.fori_loop(0, nkt, score_tile, 0, unroll=2)

    lanepos = lax.broadcasted_iota(I32, (QB, LANES), 1)
    x_scr[...] = jnp.full(x_scr.shape, lk, I32)

    SR = min(QB, CHUNK)
    lanepos_s = lax.broadcasted_iota(I32, (SR, LANES), 1)
    for r0 in range(0, QB, SR):
        part = slice(r0, r0 + SR)

        def count(pred, part=part):
            def body(kt, acc):
                tile = key_scr[kt, part, :]
                k0 = kt * KT
                for j in range(KT // LANES):
                    sl = slice(j * LANES, (j + 1) * LANES)
                    acc = acc + jnp.where(pred(tile[:, sl], lanepos_s + (k0 + j * LANES)), 1.0, 0.0)
                return acc
            acc = lax.fori_loop(0, nkt, body, jnp.zeros((SR, LANES), F32))
            return jnp.sum(acc, axis=1, keepdims=True)

        def top2(kt, carry, part=part):
            m1, m2 = carry
            tile = key_scr[kt, part, :]
            for j in range(KT // LANES):
                v = tile[:, j * LANES:(j + 1) * LANES]
                m2 = jnp.maximum(m2, jnp.minimum(m1, v))
                m1 = jnp.maximum(m1, v)
            return m1, m2

        floor_key = _sort_key(jnp.full((SR, LANES), -jnp.inf, F32))
        m1, m2 = lax.fori_loop(0, nkt, top2, (floor_key, floor_key))
        lo = _sort_key(jnp.min(_key_to_score(m2), axis=1, keepdims=True))
        hi = _sort_key(jnp.max(_key_to_score(m1), axis=1, keepdims=True)) + 1
        thr_p = _kth_largest_key_between(lambda t: count(lambda key, pos: key >= t), lo, hi, topk)
        thr_scr[part, :] = jnp.broadcast_to(thr_p, (SR, LANES))
        cnt_ge = count(lambda key, pos: key >= thr_p)

        @pl.when(jnp.max(cnt_ge) > float(topk))
        def _(count=count, thr_p=thr_p, part=part):
            need = float(topk) - count(lambda key, pos: key > thr_p)
            nbits = max(1, (lk - 1).bit_length())
            x = _first_positions(
                lambda c: count(lambda key, pos: jnp.logical_and(key == thr_p, pos < c)), need, nbits, SR)
            x_scr[part, :] = jnp.broadcast_to(x, (SR, LANES))

    thr = thr_scr[:, 0:1]
    xlim = x_scr[:, 0:1]

    def bias_tile(kt, carry):
        key = key_scr[kt]
        pos = colpos + kt * KT
        sel = jnp.logical_or(key > thr, jnp.logical_and(key == thr, pos <= xlim))
        bias_scr[kt] = jnp.where(jnp.logical_and(sel, pos <= rowpos), 0.0, NEG)
        return carry

    lax.fori_loop(0, nkt, bias_tile, 0)

    low = lanepos < DH_B
    ngrp = KT // LANES

    def group_fold(op, acc, x):
        for g in range(ngrp):
            acc = op(acc, x[:, g * LANES:(g + 1) * LANES])
        return acc

    for j in range(H_B // 2):
        qpair = q_ref[:, j * LANES:(j + 1) * LANES] * DH_B ** -0.5
        q0m = jnp.where(low, qpair, 0.0).astype(BF16)
        q1m = jnp.where(low, 0.0, qpair).astype(BF16)

        def pass1(kt, carry, q0m=q0m, q1m=q1m, j=j):
            mx0, mx1 = carry
            k0 = pl.multiple_of(kt * KT, KT)
            kk = kb_ref[pl.ds(k0, KT), j * LANES:(j + 1) * LANES]
            bias = bias_scr[kt]
            lg0 = _dot_nt(q0m, kk) + bias
            lg1 = _dot_nt(q1m, kk) + bias
            lg_scr[0, kt] = lg0
            lg_scr[1, kt] = lg1
            return group_fold(jnp.maximum, mx0, lg0), group_fold(jnp.maximum, mx1, lg1)

        neg = jnp.full((QB, LANES), NEG, F32)
        mx0, mx1 = lax.fori_loop(0, nkt, pass1, (neg, neg), unroll=2)
        m0 = jnp.max(mx0, axis=1, keepdims=True)
        m1 = jnp.max(mx1, axis=1, keepdims=True)

        def pass2(kt, carry, m0=m0, m1=m1, j=j):
            l0, l1, a0, a1 = carry
            k0 = pl.multiple_of(kt * KT, KT)
            vv = vb_ref[pl.ds(k0, KT), j * LANES:(j + 1) * LANES]
            p0 = jnp.exp(lg_scr[0, kt] - m0)
            p1 = jnp.exp(lg_scr[1, kt] - m1)
            a0 = a0 + _dot(p0.astype(BF16), vv)
            a1 = a1 + _dot(p1.astype(BF16), vv)
            return group_fold(jnp.add, l0, p0), group_fold(jnp.add, l1, p1), a0, a1

        zero = jnp.zeros((QB, LANES), F32)
        l0, l1, a0, a1 = lax.fori_loop(0, nkt, pass2, (zero, zero, zero, zero), unroll=2)
        out0 = a0 / jnp.sum(l0, axis=1, keepdims=True)
        out1 = a1 / jnp.sum(l1, axis=1, keepdims=True)
        o_ref[:, j * LANES:(j + 1) * LANES] = jnp.where(low, out0, out1).astype(o_ref.dtype)


def _dsa_prompt(bq, s, kb, vb, nb, t):
    n = bq.shape[0]
    QB = math.gcd(t, DSA_QB)
    topk = min(TOPK_MAX, t // 4)
    assert topk <= 2 * LANES
    nseg = 1
    for cand in (8, 4, 2):
        if t % cand == 0 and (t // cand) % 512 == 0:
            nseg = cand
            break
    seg = t // nseg
    KT = 512 if seg % 512 == 0 else seg
    s3 = s.reshape(nb, t, LANES)
    kb3 = kb.reshape(nb, t, W_B)
    vb3 = vb.reshape(nb, t, W_B)
    nqb = seg // QB
    outs = []
    for g in range(nseg):
        lk = (g + 1) * seg
        nkt = lk // KT
        row_blk = functools.partial(lambda b, i, g: (b * (t // QB) + g * nqb + i, 0), g=g)
        outs.append(pl.pallas_call(
            functools.partial(_dsa_prompt_kernel, q_off=g * seg, kt_size=KT, topk=topk),
            grid=(nb, nqb),
            in_specs=[pl.BlockSpec((QB, 2 * W_B), row_blk),
                      pl.BlockSpec((QB, LANES), row_blk),
                      pl.BlockSpec((None, lk, LANES), lambda b, i: (b, 0, 0)),
                      pl.BlockSpec((None, lk, W_B), lambda b, i: (b, 0, 0)),
                      pl.BlockSpec((None, lk, W_B), lambda b, i: (b, 0, 0))],
            out_specs=pl.BlockSpec((None, QB, W_B), lambda b, i: (b, i, 0)),
            out_shape=jax.ShapeDtypeStruct((nb, seg, W_B), BF16),
            scratch_shapes=[pltpu.VMEM((nkt, QB, KT), I32), pltpu.VMEM((nkt, QB, KT), F32),
                            pltpu.VMEM((QB, LANES), I32), pltpu.VMEM((QB, LANES), I32),
                            pltpu.VMEM((2, nkt, QB, KT), F32)],
            compiler_params=_cparams("parallel", "arbitrary"),
            name=f"dsa_prompt_{g}",
        )(bq, s, s3, kb3, vb3))
    return jnp.concatenate(outs, axis=1).reshape(n, W_B)


SCORE_PAGES = 16
ATTN_PAGES = 16


def _dsa_sample_score_kernel(pt_ref, q_ref, s_ref, snew_ref, *rest):
    page_refs, o_ref = rest[:-1], rest[-1]
    p = pl.program_id(1)
    last = pl.num_programs(1) - 1
    T = q_ref.shape[0]
    qi = q_ref[:, W_B:2 * W_B]
    wi = s_ref[:, S_WI:S_WI + H_IDX]
    qst = jnp.concatenate([qi[:, h * D_IDX:(h + 1) * D_IDX] for h in range(H_IDX)], axis=0)
    wcol = jnp.concatenate([wi[:, h:h + 1] for h in range(H_IDX)], axis=0)

    def scores(dots):
        r = wcol * jnp.maximum(dots, 0.0)
        sc = r[0:T]
        for h in range(1, H_IDX):
            sc = sc + r[h * T:(h + 1) * T]
        return sc

    @pl.when(p < last)
    def _():
        for j, page_ref in enumerate(page_refs):
            o_ref[j] = scores(_dot(qst, page_ref[...]))

    @pl.when(p == last)
    def _():
        sc = scores(_dot_nt(qst, snew_ref[:, S_KI:S_KI + D_IDX]))
        r = lax.broadcasted_iota(I32, sc.shape, 0)
        c = lax.broadcasted_iota(I32, sc.shape, 1)
        o_ref[0] = jnp.where(c <= r, sc, -jnp.inf)
        for j in range(1, len(page_refs)):
            o_ref[j] = jnp.full(sc.shape, -jnp.inf, F32)


def _dsa_sample_attn_kernel(pt_ref, q_ref, sc_ref, knew_ref, vnew_ref, *rest, topk, npg, G):
    kpages, vpages, o_ref = rest[:G], rest[G:2 * G], rest[2 * G]
    thr_scr, x_scr, m_scr, l_scr, acc_scr = rest[2 * G + 1:]
    p = pl.program_id(1)
    last = pl.num_programs(1) - 1
    T = q_ref.shape[0]
    R = H_B * T
    ltot = sc_ref.shape[0] * PAGE_SIZE

    @pl.when(p == 0)
    def _():
        key = _sort_key(sc_ref[...])
        pos = (lax.broadcasted_iota(I32, key.shape, 0) * PAGE_SIZE
               + lax.broadcasted_iota(I32, key.shape, 2))

        def count(pred):
            per_lane = jnp.sum(jnp.where(pred(key, pos), 1.0, 0.0), axis=0)
            return jnp.sum(per_lane, axis=1, keepdims=True)

        thr = _kth_largest_key(lambda t: count(lambda k_, p_: k_ >= t), topk, T)
        cnt_ge = count(lambda k_, p_: k_ >= thr)
        thr_scr[...] = jnp.broadcast_to(thr, thr_scr.shape)
        x_scr[...] = jnp.full(x_scr.shape, ltot, I32)

        @pl.when(jnp.max(cnt_ge) > float(topk))
        def _():
            cnt_gt = count(lambda k_, p_: k_ > thr)
            nbits = max(1, (ltot - 1).bit_length())
            x = _first_positions(lambda c: count(lambda k_, p_: jnp.logical_and(k_ == thr, p_ < c)),
                                 float(topk) - cnt_gt, nbits, T)
            x_scr[...] = jnp.broadcast_to(x, x_scr.shape)

        m_scr[...] = jnp.full(m_scr.shape, NEG, F32)
        l_scr[...] = jnp.zeros_like(l_scr)
        acc_scr[...] = jnp.zeros_like(acc_scr)

    q = q_ref[:, 0:W_B] * DH_B ** -0.5
    qrep = jnp.concatenate([q] * H_B, axis=0)
    rr = lax.broadcasted_iota(I32, (R, W_B), 0) // T
    cc = lax.broadcasted_iota(I32, (R, W_B), 1) // DH_B
    diag = rr == cc
    qbd = jnp.where(diag, qrep, 0.0)
    thr = thr_scr[:, 0:1]
    xlim = x_scr[:, 0:1]

    def selected(page):
        key = _sort_key(sc_ref[page])
        pos = page * PAGE_SIZE + lax.broadcasted_iota(I32, key.shape, 1)
        return jnp.logical_or(key > thr, jnp.logical_and(key == thr, pos <= xlim))

    def masked(logits, valid):
        bias = jnp.where(valid, 0.0, NEG)
        return logits + jnp.concatenate([bias] * H_B, axis=0)

    def update(lgs, pv):
        mx = lgs[0]
        for lg in lgs[1:]:
            mx = jnp.maximum(mx, lg)
        m = m_scr[:, 0:1]
        m_new = jnp.maximum(m, jnp.max(mx, axis=1, keepdims=True))
        alpha = jnp.exp(m - m_new)
        ps = [jnp.exp(lg - m_new) for lg in lgs]
        psum = ps[0]
        for pr in ps[1:]:
            psum = psum + pr
        acc = pv(0, ps[0])
        for j in range(1, len(ps)):
            acc = acc + pv(j, ps[j])
        l_scr[...] = jnp.broadcast_to(alpha * l_scr[:, 0:1] + jnp.sum(psum, axis=1, keepdims=True), l_scr.shape)
        acc_scr[...] = alpha * acc_scr[...] + acc
        m_scr[...] = jnp.broadcast_to(m_new, m_scr.shape)

    q2 = _split2(qbd)

    @pl.when(p < last)
    def _():
        lgs = [masked(_dot3(q2, _split2(kpages[j][...]), _dot), selected(p * G + j)) for j in range(G)]
        update(lgs, lambda j, pr: _dot3(_split2(pr), _split2(vpages[j][...]), _dot_nt))

    @pl.when(p == last)
    def _():
        sel = selected(npg)
        r = lax.broadcasted_iota(I32, sel.shape, 0)
        c = lax.broadcasted_iota(I32, sel.shape, 1)
        lg = masked(_dot3(q2, _split2(knew_ref[...]), _dot_nt), jnp.logical_and(sel, c <= r))
        update([lg], lambda j, pr: _dot3(_split2(pr), _split2(vnew_ref[...]), _dot))
        out = jnp.where(diag, acc_scr[...] / l_scr[:, 0:1], 0.0)
        res = out[0:T]
        for h in range(1, H_B):
            res = res + out[h * T:(h + 1) * T]
        o_ref[...] = res.astype(o_ref.dtype)


def _dsa_sample(bq, s, k, v, pool_k, pool_v, pool_ki, page_table, nb, t):
    n_pool = pool_k.shape[1]
    pool_k = jnp.transpose(pool_k, (0, 1, 3, 4, 2)).reshape(1, n_pool, W_B, PAGE_SIZE)
    pool_v = jnp.transpose(pool_v, (0, 1, 3, 4, 2)).reshape(1, n_pool, W_B, PAGE_SIZE)
    pool_ki = jnp.transpose(pool_ki, (0, 1, 3, 2))
    npg = page_table.shape[1]
    past = npg * PAGE_SIZE
    topk = min(TOPK_MAX, (past + t) // 4)
    gs, ga = math.gcd(npg, SCORE_PAGES), math.gcd(npg, ATTN_PAGES)
    pad = PAGE_SIZE - t
    s_new = jnp.pad(s.reshape(nb, t, LANES), ((0, 0), (0, pad), (0, 0)))
    k_new = jnp.pad(k.reshape(nb, t, W_B), ((0, 0), (0, pad), (0, 0)))
    v_new = jnp.pad(v.reshape(nb, t, W_B), ((0, 0), (0, pad), (0, 0)))
    lastp = npg - 1
    ptot = npg + gs

    def page_spec(tail, j, group):
        zeros = (0,) * len(tail)
        return pl.BlockSpec((None, None) + tail,
                            lambda b, p, pt: (0, pt[b, jnp.minimum(p * group + j, lastp)]) + zeros)

    scores = pl.pallas_call(
        _dsa_sample_score_kernel,
        grid_spec=pltpu.PrefetchScalarGridSpec(
            num_scalar_prefetch=1,
            grid=(nb, npg // gs + 1),
            in_specs=[pl.BlockSpec((t, 2 * W_B), lambda b, p, pt: (b, 0)),
                      pl.BlockSpec((t, LANES), lambda b, p, pt: (b, 0)),
                      pl.BlockSpec((None, PAGE_SIZE, LANES), lambda b, p, pt: (b, 0, 0))]
            + [page_spec((D_IDX, PAGE_SIZE), j, gs) for j in range(gs)],
            out_specs=pl.BlockSpec((None, gs, t, PAGE_SIZE), lambda b, p, pt: (b, p, 0, 0))),
        out_shape=jax.ShapeDtypeStruct((nb, ptot, t, PAGE_SIZE), F32),
        compiler_params=_cparams("parallel", "arbitrary"),
        name="dsa_sample_scores",
    )(page_table, bq, s, s_new, *([pool_ki] * gs))

    kv_tail = (W_B, PAGE_SIZE)
    return pl.pallas_call(
        functools.partial(_dsa_sample_attn_kernel, topk=topk, npg=npg, G=ga),
        grid_spec=pltpu.PrefetchScalarGridSpec(
            num_scalar_prefetch=1,
            grid=(nb, npg // ga + 1),
            in_specs=[pl.BlockSpec((t, 2 * W_B), lambda b, p, pt: (b, 0)),
                      pl.BlockSpec((None, ptot, t, PAGE_SIZE), lambda b, p, pt: (b, 0, 0, 0)),
                      pl.BlockSpec((None, PAGE_SIZE, W_B), lambda b, p, pt: (b, 0, 0)),
                      pl.BlockSpec((None, PAGE_SIZE, W_B), lambda b, p, pt: (b, 0, 0))]
            + [page_spec(kv_tail, j, ga) for j in range(ga)]
            + [page_spec(kv_tail, j, ga) for j in range(ga)],
            out_specs=pl.BlockSpec((t, W_B), lambda b, p, pt: (b, 0)),
            scratch_shapes=[pltpu.VMEM((t, LANES), I32), pltpu.VMEM((t, LANES), I32),
                            pltpu.VMEM((H_B * t, LANES), F32), pltpu.VMEM((H_B * t, LANES), F32),
                            pltpu.VMEM((H_B * t, W_B), F32)]),
        out_shape=jax.ShapeDtypeStruct((nb * t, W_B), F32),
        compiler_params=_cparams("parallel", "arbitrary"),
        name="dsa_sample_attn",
    )(page_table, bq, scores, k_new, v_new, *([pool_k] * ga), *([pool_v] * ga))


def _moe(x, prm, layer):
    wg, wu, wd = prm["moe_wg"][layer], prm["moe_wu"][layer], prm["moe_wd"][layer]
    g, b = prm["ln_g"][layer, 1], prm["ln_b"][layer, 1]
    if wg.dtype == BF16:
        return _moe_ln_grouped(x, prm["router_wt"], prm["router_bt"], wg, wu, wd, g, b)
    return _moe_ln(x, prm["router_w"], prm["router_b"], wg, wu, wd, g, b)


def _pad_chunks(a, nb, t, fill=None):
    c = a.shape[1]
    a3 = a.reshape(nb, t, c)
    if fill is None:
        a3 = jnp.pad(a3, ((0, 0), (0, CHUNK - t), (0, 0)))
    else:
        a3 = jnp.concatenate([a3, jnp.broadcast_to(fill, (nb, CHUNK - t, c))], axis=1)
    return a3.reshape(nb * CHUNK, c)


def _retention_tables(t_true, pos0, t_pad):
    L = CHUNK
    lt = min(L, t_true)
    lg = jnp.log1p(-jnp.exp2(-5.0 - jnp.arange(H_D, dtype=F32)))
    j = jnp.arange(L, dtype=F32)
    causal = jnp.tril(jnp.ones((L, L), dtype=bool))
    dmat = jnp.exp(jnp.where(causal, (j[:, None] - j[None, :]) * lg[:, None, None], -jnp.inf))
    qdec = jnp.exp((j + 1.0) * lg[:, None])
    kdec = jnp.where(j < lt, jnp.exp((lt - 1.0 - j) * lg[:, None]), 0.0)
    sdec = jnp.broadcast_to(jnp.exp(lt * lg)[:, None], (H_D, L))
    dec = jnp.zeros((H_D, L, LANES), F32)
    dec = dec.at[:, :, 0].set(qdec).at[:, :, 1].set(kdec).at[:, :, 2].set(sdec)
    half = DK_D // 2
    freq = ROPE_BASE ** (-jnp.arange(half, dtype=F32) / half)
    pos = (pos0 + jnp.arange(t_pad)).astype(F32)
    ang = pos[:, None] * freq[None, :]
    cos, sin = jnp.cos(ang), jnp.sin(ang)
    return dmat, dec, jnp.concatenate([cos, cos], -1), jnp.concatenate([-sin, sin], -1)


def _forward(x3, st, prm):
    nb, t, d = x3.shape
    n = nb * t
    x = x3.reshape(n, d)
    short = t < CHUNK
    t_pad = CHUNK if short else t
    nc = t_pad // CHUNK
    new = {}
    odt = prm["w_in_e"].dtype

    a, bq, k, v, s, kb, vb = _inproj_e(x, prm["w_in_e"], prm["bias_e"])
    if st is None:
        c0 = jnp.zeros((nb, H_A, DK_A, DV_A), F32)
        n0 = jnp.zeros((nb, H_A, DK_A), F32)
        m0 = jnp.zeros((nb, H_A), F32)
    else:
        c0, n0, m0 = st["mlstm_C"][0], st["mlstm_n"][0], st["mlstm_m"][0]
    m0b = jnp.broadcast_to(m0[:, :, None], (nb, H_A, LANES))
    if short:
        lane = jnp.arange(LANES)
        fill = jnp.where((lane >= S_IG) & (lane < S_IG + H_A), NEG,
                         jnp.where((lane >= S_FG) & (lane < S_FG + H_A), 1e4, 0.0)).astype(F32)
        a_m, s_m = _pad_chunks(a, nb, t), _pad_chunks(s, nb, t, fill)
    else:
        a_m, s_m = a, s
    ya, c1, n1, m1 = _mlstm(a_m, s_m, prm["a_norm_g"], c0, n0, m0b, nb, nc, odt)
    if short:
        ya = ya.reshape(nb, CHUNK, W_A)[:, :t].reshape(n, W_A)
    if st is None:
        yb = _dsa_prompt(bq, s, kb, vb, nb, t)
    else:
        yb = _dsa_sample(bq, s, k, v, st["pool_k"], st["pool_v"], st["pool_ki"], st["page_table"], nb, t)
    new["mlstm_C"], new["mlstm_n"], new["mlstm_m"] = c1[None], n1[None], m1[None, :, :, 0]
    new["k"] = k.reshape(1, nb, t, H_B, DH_B)
    new["v"] = v.reshape(1, nb, t, H_B, DH_B)
    new["kidx"] = s[:, S_KI:S_KI + D_IDX].reshape(1, nb, t, D_IDX)
    x = _outproj_ln(ya, yb, prm["w_out_e"], x, prm["ln_g"][0, 0], prm["ln_b"][0, 0])
    x = _moe(x, prm, 0)

    c, dd = _inproj_o(x, prm["w_in_o"])
    if st is None:
        conv0 = jnp.zeros((nb, SUBLANES, W_C), F32)
        h0 = jnp.zeros((nb, SUBLANES, W_C), F32)
        s0 = jnp.zeros((nb, H_D, DK_D, DV_D), F32)
        pos0 = 0
    else:
        conv0 = jnp.pad(st["conv"][0], ((0, 0), (SUBLANES - (CONV_W - 1), 0), (0, 0)))
        h0 = jnp.broadcast_to(st["lru_h"][0][:, None, :], (nb, SUBLANES, W_C))
        s0 = st["ret_S"][0]
        pos0 = st["page_table"].shape[1] * PAGE_SIZE
    tb = _row_block(t, 256)
    yc, hl = _rglru(c, prm["c_conv_w"], prm["c_conv_b"], prm["c_wa"], prm["c_ba"], prm["c_wx"], prm["c_bx"],
                    prm["c_lambda"], conv0, h0, nb, t // tb, tb, odt)
    dmat, dec, cos2, sin2 = _retention_tables(t, pos0, t_pad)
    d_m = _pad_chunks(dd, nb, t) if short else dd
    yd, s1 = _retention(d_m, cos2, sin2, dmat, dec, prm["d_norm_g"], s0, nb, nc, odt)
    if short:
        yd = yd.reshape(nb, CHUNK, W_D)[:, :t].reshape(n, W_D)
    new["conv"] = c[:, 0:W_C].reshape(nb, t, W_C)[None, :, t - (CONV_W - 1):]
    new["lru_h"] = hl[None, :, 0]
    new["ret_S"] = s1[None]
    x = _outproj_ln(yc, yd, prm["w_out_o"], x, prm["ln_g"][1, 0], prm["ln_b"][1, 0])
    x = _moe(x, prm, 1)
    return x.reshape(nb, t, d), new


def _block_diag(w):
    nblk, blk, _ = w.shape
    eye = jnp.eye(nblk, dtype=w.dtype)
    return (eye[:, None, :, None] * w[:, :, None, :]).reshape(nblk * blk, nblk * blk)


def _prepare_params(w_in_e, b_if_e, a_norm_g, w_out_e, w_in_o, c_conv_w, c_conv_b, c_wa, c_ba, c_wx, c_bx,
                    c_lambda, d_norm_g, w_out_o, router_w, router_bias, moe_w_gate, moe_w_up, moe_w_down,
                    ln_g, ln_b, wdt):
    d = w_in_e.shape[1]
    o = [0]
    for width in (W_A, W_A, W_A, W_A, 2 * H_A, W_B, W_B, W_B, H_IDX * D_IDX, D_IDX, H_IDX):
        o.append(o[-1] + width)
    we = w_in_e[0]
    col = lambda i: we[:, o[i]:o[i + 1]]
    slab = jnp.zeros((d, LANES), F32)
    slab = slab.at[:, S_KI:S_KI + D_IDX].set(col(9)).at[:, S_IG:S_IG + 2 * H_A].set(col(4))
    slab = slab.at[:, S_WI:S_WI + H_IDX].set(col(10))
    w_e = jnp.concatenate([col(0), col(1), col(2), col(3), col(5), col(8), col(6), col(7), slab], axis=1)
    bias_e = jnp.zeros((1, LANES), F32).at[0, S_IG:S_IG + 2 * H_A].set(b_if_e[0])
    rw = jnp.zeros((d, LANES), F32).at[:, :N_EXPERTS].set(router_w)
    rb = jnp.zeros((1, LANES), F32).at[0, :N_EXPERTS].set(router_bias)
    row = lambda a: a.reshape(1, -1).astype(F32)
    return {
        "w_in_e": w_e.astype(wdt), "bias_e": bias_e, "a_norm_g": row(a_norm_g[0]),
        "w_out_e": w_out_e[0].astype(wdt), "w_in_o": w_in_o[0].astype(wdt),
        "c_conv_w": c_conv_w[0].astype(F32), "c_conv_b": row(c_conv_b[0]),
        "c_wa": _block_diag(c_wa[0]).astype(wdt), "c_ba": row(c_ba[0]),
        "c_wx": _block_diag(c_wx[0]).astype(wdt), "c_bx": row(c_bx[0]),
        "c_lambda": row(c_lambda[0]), "d_norm_g": row(d_norm_g[0]), "w_out_o": w_out_o[0].astype(wdt),
        "router_w": rw, "router_b": rb,
        "router_wt": router_w.T.astype(F32), "router_bt": jnp.broadcast_to(rb.reshape(LANES, 1), (LANES, LANES)),
        "moe_wg": moe_w_gate.astype(wdt), "moe_wu": moe_w_up.astype(wdt), "moe_wd": moe_w_down.astype(wdt),
        "ln_g": ln_g.reshape(DEPTH, 2, 1, -1).astype(F32), "ln_b": ln_b.reshape(DEPTH, 2, 1, -1).astype(F32),
    }


def kernel(x_prompt, x_sample, state_mlstm_C, state_mlstm_n, state_mlstm_m, cache_k, cache_v, cache_kidx,
           page_table, state_conv, state_lru_h, state_ret_S, w_in_e, b_if_e, a_norm_g, w_out_e, w_in_o,
           c_conv_w, c_conv_b, c_wa, c_ba, c_wx, c_bx, c_lambda, d_norm_g, w_out_o, router_w, router_bias,
           moe_w_gate, moe_w_up, moe_w_down, ln_g, ln_b):
    weights = (w_in_e, b_if_e, a_norm_g, w_out_e, w_in_o, c_conv_w, c_conv_b, c_wa, c_ba, c_wx, c_bx, c_lambda,
               d_norm_g, w_out_o, router_w, router_bias, moe_w_gate, moe_w_up, moe_w_down, ln_g, ln_b)
    prm_prompt = _prepare_params(*weights, BF16)
    prm_sample = _prepare_params(*weights, F32)
    st = {"mlstm_C": state_mlstm_C, "mlstm_n": state_mlstm_n, "mlstm_m": state_mlstm_m,
          "pool_k": cache_k, "pool_v": cache_v, "pool_ki": cache_kidx, "page_table": page_table,
          "conv": state_conv, "lru_h": state_lru_h, "ret_S": state_ret_S}
    y_p, nsp = _forward(x_prompt, None, prm_prompt)
    y_s, nss = _forward(x_sample, st, prm_sample)
    names = ("mlstm_C", "mlstm_n", "mlstm_m", "k", "v", "kidx", "conv", "lru_h", "ret_S")
    return (y_p, y_s) + tuple(nsp[k] for k in names) + tuple(nss[k] for k in names)
```

```python
import functools
import math

import jax
import jax.numpy as jnp
from jax import lax
from jax.experimental import pallas as pl
from jax.experimental.pallas import tpu as pltpu

F32 = jnp.float32
BF16 = jnp.bfloat16
I32 = jnp.int32

DEPTH = 2
PAGE_SIZE = 128
H_A, DK_A, DV_A = 4, 128, 128
W_A = H_A * DV_A
H_B, DH_B = 8, 64
W_B = H_B * DH_B
H_IDX, D_IDX = 8, 64
TOPK_MAX = 256
W_C, N_BLK_C, CONV_W, LRU_C = 512, 8, 4, 8.0
BLK_C = W_C // N_BLK_C
H_D, DK_D, DV_D = 4, 128, 128
W_D = H_D * DV_D
ROPE_BASE = 10000.0
N_EXPERTS, N_GROUPS, TOP_K_EXP, D_FF_EXP = 16, 4, 2, 512
EXP_PER_GROUP = N_EXPERTS // N_GROUPS
ALPHA = (2 * DEPTH) ** 0.25
LN_EPS = 1e-5
HN_EPS = 1e-6

CHUNK = 128
DSA_QB = 256
LANES = 128
SUBLANES = 8
NEG = -1e30
INT_MIN = -2 ** 31
VMEM_LIMIT = 56 * 1024 * 1024

S_KI = 0
S_IG = 64
S_FG = 68
S_WI = 72


def _cparams(*sem):
    return pltpu.CompilerParams(dimension_semantics=sem, vmem_limit_bytes=VMEM_LIMIT)


def _precision(a, b):
    return lax.Precision.HIGHEST if a.dtype == F32 and b.dtype == F32 else None


def _dot(a, b):
    return jnp.dot(a, b, preferred_element_type=F32, precision=_precision(a, b))


def _dot_nt(a, b):
    return lax.dot_general(a, b, (((1,), (1,)), ((), ())), preferred_element_type=F32, precision=_precision(a, b))


def _dot_tn(a, b):
    return lax.dot_general(a, b, (((0,), (0,)), ((), ())), preferred_element_type=F32, precision=_precision(a, b))


def _split2(a):
    hi = a.astype(BF16)
    return hi, (a - hi.astype(F32)).astype(BF16)


def _dot3(a2, b2, dot):
    m = a2[0].shape[0]
    both = dot(jnp.concatenate([a2[0], a2[1]], axis=0), b2[0])
    return both[0:m] + (both[m:2 * m] + dot(a2[0], b2[1]))


def _row_block(n, target):
    t = min(n, target)
    while n % t:
        t //= 2
    return t


def _layer_norm(z, g, b):
    mu = jnp.mean(z, -1, keepdims=True)
    zc = z - mu
    var = jnp.mean(zc * zc, -1, keepdims=True)
    return zc * lax.rsqrt(var + LN_EPS) * g + b


def _head_norm(h):
    mu = jnp.mean(h, -1, keepdims=True)
    hc = h - mu
    var = jnp.mean(hc * hc, -1, keepdims=True)
    return hc * lax.rsqrt(var + HN_EPS)


def _inproj_e_kernel(x_ref, w_ref, bias_ref, a_ref, bq_ref, k_ref, v_ref, s_ref, kb_ref, vb_ref):
    x = x_ref[...].astype(w_ref.dtype)

    def mm(lo, hi):
        return _dot(x, w_ref[:, lo:hi])

    a_ref[:, 0:W_A] = mm(0, W_A)
    a_ref[:, W_A:2 * W_A] = mm(W_A, 2 * W_A) * DK_A ** -0.5
    a_ref[:, 2 * W_A:4 * W_A] = mm(2 * W_A, 4 * W_A)
    o = 4 * W_A
    bq_ref[...] = mm(o, o + 2 * W_B)
    o += 2 * W_B
    k = mm(o, o + W_B)
    k_ref[...] = k
    kb_ref[...] = k.astype(kb_ref.dtype)
    o += W_B
    v = mm(o, o + W_B)
    v_ref[...] = v
    vb_ref[...] = v.astype(vb_ref.dtype)
    o += W_B
    s_ref[...] = mm(o, o + LANES) + bias_ref[...]


def _inproj_e(x, w, bias):
    n, d = x.shape
    tm = _row_block(n, 256)
    wcols = w.shape[1]
    outs = [(4 * W_A, F32), (2 * W_B, F32), (W_B, F32), (W_B, F32), (LANES, F32), (W_B, w.dtype), (W_B, w.dtype)]
    return pl.pallas_call(
        _inproj_e_kernel,
        grid=(n // tm,),
        in_specs=[pl.BlockSpec((tm, d), lambda i: (i, 0)),
                  pl.BlockSpec((d, wcols), lambda i: (0, 0)),
                  pl.BlockSpec((1, LANES), lambda i: (0, 0))],
        out_specs=[pl.BlockSpec((tm, c), lambda i: (i, 0)) for c, _ in outs],
        out_shape=[jax.ShapeDtypeStruct((n, c), dt) for c, dt in outs],
        compiler_params=_cparams("parallel"),
        name="inproj_even",
    )(x, w, bias)


def _inproj_o_kernel(x_ref, w_ref, c_ref, d_ref):
    x = x_ref[...].astype(w_ref.dtype)
    c_ref[...] = _dot(x, w_ref[:, 0:2 * W_C])
    d_ref[...] = _dot(x, w_ref[:, 2 * W_C:2 * W_C + 4 * W_D])


def _inproj_o(x, w):
    n, d = x.shape
    tm = _row_block(n, 256)
    return pl.pallas_call(
        _inproj_o_kernel,
        grid=(n // tm,),
        in_specs=[pl.BlockSpec((tm, d), lambda i: (i, 0)),
                  pl.BlockSpec(w.shape, lambda i: (0, 0))],
        out_specs=[pl.BlockSpec((tm, 2 * W_C), lambda i: (i, 0)),
                   pl.BlockSpec((tm, 4 * W_D), lambda i: (i, 0))],
        out_shape=[jax.ShapeDtypeStruct((n, 2 * W_C), F32), jax.ShapeDtypeStruct((n, 4 * W_D), F32)],
        compiler_params=_cparams("parallel"),
        name="inproj_odd",
    )(x, w)


def _outproj_ln_kernel(y1_ref, y2_ref, w_ref, x_ref, g_ref, b_ref, o_ref):
    half = y1_ref.shape[1]
    y = _dot(y1_ref[...], w_ref[0:half, :]) + _dot(y2_ref[...], w_ref[half:2 * half, :])
    o_ref[...] = _layer_norm(ALPHA * x_ref[...] + y, g_ref[...], b_ref[...])


def _outproj_ln(y1, y2, w, x, g, b):
    n, d = x.shape
    tm = _row_block(n, 512)
    half = y1.shape[1]
    return pl.pallas_call(
        _outproj_ln_kernel,
        grid=(n // tm,),
        in_specs=[pl.BlockSpec((tm, half), lambda i: (i, 0)),
                  pl.BlockSpec((tm, half), lambda i: (i, 0)),
                  pl.BlockSpec(w.shape, lambda i: (0, 0)),
                  pl.BlockSpec((tm, d), lambda i: (i, 0)),
                  pl.BlockSpec((1, d), lambda i: (0, 0)),
                  pl.BlockSpec((1, d), lambda i: (0, 0))],
        out_specs=pl.BlockSpec((tm, d), lambda i: (i, 0)),
        out_shape=jax.ShapeDtypeStruct((n, d), F32),
        compiler_params=_cparams("parallel"),
        name="outproj_ln",
    )(y1, y2, w, x, g, b)


def _route(logits, bias):
    lane = lax.broadcasted_iota(I32, logits.shape, 1)
    valid = lane < N_EXPERTS
    pos = lane % EXP_PER_GROUP
    grp = (lane // EXP_PER_GROUP).astype(F32)
    s = jax.nn.sigmoid(logits)
    sel = jnp.where(valid, s + bias, NEG)
    rank = jnp.zeros(logits.shape, F32)
    for d in range(1, EXP_PER_GROUP):
        lo = pltpu.roll(sel, d, 1)
        hi = pltpu.roll(sel, LANES - d, 1)
        rank = rank + jnp.where(jnp.logical_and(pos >= d, lo >= sel), 1.0, 0.0)
        rank = rank + jnp.where(jnp.logical_and(pos + d < EXP_PER_GROUP, hi > sel), 1.0, 0.0)
    top2 = jnp.logical_and(rank < TOP_K_EXP, valid)
    contrib = jnp.where(top2, sel, 0.0)
    gs = contrib
    for d in range(1, EXP_PER_GROUP):
        lo = pltpu.roll(contrib, d, 1)
        hi = pltpu.roll(contrib, LANES - d, 1)
        gs = gs + jnp.where(pos >= d, lo, 0.0) + jnp.where(pos + d < EXP_PER_GROUP, hi, 0.0)
    gs = jnp.where(valid, gs, NEG)
    gmax = jnp.max(gs, axis=1, keepdims=True)
    best = jnp.min(jnp.where(gs == gmax, grp, 1e9), axis=1, keepdims=True)
    chosen = jnp.logical_and(top2, grp == best)
    s_sel = jnp.where(chosen, s, 0.0)
    return s_sel / jnp.sum(s_sel, axis=1, keepdims=True)


def _moe_kernel(x_ref, rw_ref, rb_ref, wg_ref, wu_ref, wd_ref, g_ref, b_ref, o_ref,
                xb_scr, comb_scr, acc_scr, *, sub):
    e = pl.program_id(1)
    tm = x_ref.shape[0]

    @pl.when(e == 0)
    def _():
        xb_scr[...] = x_ref[...].astype(xb_scr.dtype)
        logits = jnp.dot(x_ref[...], rw_ref[...], precision=lax.Precision.HIGHEST, preferred_element_type=F32)
        comb_scr[...] = _route(logits, rb_ref[...])
        acc_scr[...] = jnp.zeros_like(acc_scr)

    def rows(r, carry):
        r0 = pl.multiple_of(r * sub, sub)
        xb = xb_scr[pl.ds(r0, sub), :]
        gate = _dot(xb, wg_ref[...])
        h = gate * jax.nn.sigmoid(gate) * _dot(xb, wu_ref[...])
        y = _dot(h.astype(wd_ref.dtype), wd_ref[...])
        comb = comb_scr[pl.ds(r0, sub), :]
        lane = lax.broadcasted_iota(I32, comb.shape, 1)
        c_e = jnp.sum(jnp.where(lane == e, comb, 0.0), axis=1, keepdims=True)
        acc_scr[pl.ds(r0, sub), :] += c_e * y
        return carry

    lax.fori_loop(0, tm // sub, rows, 0)

    @pl.when(e == N_EXPERTS - 1)
    def _():
        o_ref[...] = _layer_norm(ALPHA * x_ref[...] + acc_scr[...], g_ref[...], b_ref[...])


def _moe_ln(x, rw, rb, wg, wu, wd, g, b):
    n, d = x.shape
    tm = _row_block(n, 1024)
    sub = _row_block(tm, 256)
    f = wg.shape[2]
    return pl.pallas_call(
        functools.partial(_moe_kernel, sub=sub),
        grid=(n // tm, N_EXPERTS),
        in_specs=[pl.BlockSpec((tm, d), lambda i, e: (i, 0)),
                  pl.BlockSpec((d, LANES), lambda i, e: (0, 0)),
                  pl.BlockSpec((1, LANES), lambda i, e: (0, 0)),
                  pl.BlockSpec((None, d, f), lambda i, e: (e, 0, 0)),
                  pl.BlockSpec((None, d, f), lambda i, e: (e, 0, 0)),
                  pl.BlockSpec((None, f, d), lambda i, e: (e, 0, 0)),
                  pl.BlockSpec((1, d), lambda i, e: (0, 0)),
                  pl.BlockSpec((1, d), lambda i, e: (0, 0))],
        out_specs=pl.BlockSpec((tm, d), lambda i, e: (i, 0)),
        out_shape=jax.ShapeDtypeStruct((n, d), F32),
        scratch_shapes=[pltpu.VMEM((tm, d), wg.dtype), pltpu.VMEM((tm, LANES), F32), pltpu.VMEM((tm, d), F32)],
        compiler_params=_cparams("parallel", "arbitrary"),
        name="moe_ln",
    )(x, rw, rb, wg, wu, wd, g, b)


def _route_t(logits, bias):
    s = jax.nn.sigmoid(logits)
    sel = s + bias
    top2, gscore = [], []
    for g in range(N_GROUPS):
        v = [sel[g * EXP_PER_GROUP + i:g * EXP_PER_GROUP + i + 1] for i in range(EXP_PER_GROUP)]
        flags = []
        for i in range(EXP_PER_GROUP):
            rank = jnp.zeros(v[i].shape, F32)
            for j in range(EXP_PER_GROUP):
                if j < i:
                    rank = rank + jnp.where(v[j] >= v[i], 1.0, 0.0)
                elif j > i:
                    rank = rank + jnp.where(v[j] > v[i], 1.0, 0.0)
            flags.append(rank < TOP_K_EXP)
        gs = jnp.where(flags[0], v[0], 0.0)
        for i in range(1, EXP_PER_GROUP):
            gs = gs + jnp.where(flags[i], v[i], 0.0)
        top2.append(flags)
        gscore.append(gs)
    gmax = gscore[0]
    for g in range(1, N_GROUPS):
        gmax = jnp.maximum(gmax, gscore[g])
    chosen, taken = [], None
    for g in range(N_GROUPS):
        c = gscore[g] == gmax
        if taken is not None:
            c = jnp.logical_and(c, jnp.logical_not(taken))
        taken = c if taken is None else jnp.logical_or(taken, c)
        chosen.append(c)
    gates = []
    for g in range(N_GROUPS):
        for i in range(EXP_PER_GROUP):
            e = g * EXP_PER_GROUP + i
            gates.append(jnp.where(jnp.logical_and(chosen[g], top2[g][i]), s[e:e + 1], 0.0))
    den = gates[0]
    for gt in gates[1:]:
        den = den + gt
    return [gt / den for gt in gates], chosen


MOE_BLOCK = 1024
MOE_TILE = 288
MOE_TAIL = 64

R_CHOSEN = EXP_PER_GROUP
R_RANK = EXP_PER_GROUP + 1


def _moe_group_kernel(x_ref, rwt_ref, rbt_ref, tri_ref, wg_ref, wu_ref, wd_ref, g_ref, b_ref, o_ref,
                      xb_scr, info_scr, acc_scr):
    g = pl.program_id(1)
    tb = x_ref.shape[0]
    mdt = xb_scr.dtype

    @pl.when(g == 0)
    def _():
        xb_scr[...] = x_ref[...].astype(mdt)
        logits = lax.dot_general(rwt_ref[...], x_ref[...], (((1,), (1,)), ((), ())),
                                 precision=lax.Precision.HIGHEST, preferred_element_type=F32)
        comb, chosen = _route_t(logits, rbt_ref[0:N_EXPERTS, 0:1])
        flags = [jnp.where(c, 1.0, 0.0) for c in chosen]
        pad = jnp.zeros((SUBLANES - N_GROUPS, tb), F32)
        prefix = _dot(jnp.concatenate(flags + [pad], axis=0).astype(BF16), tri_ref[...])
        fill = jnp.zeros((SUBLANES - EXP_PER_GROUP - 2, tb), F32)
        for gg in range(N_GROUPS):
            rows = comb[gg * EXP_PER_GROUP:(gg + 1) * EXP_PER_GROUP] + [flags[gg], prefix[gg:gg + 1] - 1.0, fill]
            info_scr[gg] = jnp.concatenate(rows, axis=0)
        acc_scr[...] = jnp.zeros_like(acc_scr)

    info = info_scr[g]
    chosen_row = info[R_CHOSEN:R_CHOSEN + 1] > 0.5
    rank_row = info[R_RANK:R_RANK + 1]
    count = jnp.max(jnp.where(chosen_row, rank_row + 1.0, 0.0)).astype(I32)

    def tile(first, rows):
        rid = (lax.broadcasted_iota(I32, (rows, tb), 0) + first).astype(F32)
        onehot = jnp.where(jnp.logical_and(rank_row == rid, chosen_row), 1.0, 0.0)
        pb = onehot.astype(mdt)
        xg = _dot(pb, xb_scr[...]).astype(mdt)
        out = jnp.zeros((rows, x_ref.shape[1]), F32)
        for i in range(EXP_PER_GROUP):
            gate = jnp.sum(onehot * info[i:i + 1], axis=1, keepdims=True)
            gt = _dot(xg, wg_ref[i])
            h = gt * jax.nn.sigmoid(gt) * _dot(xg, wu_ref[i])
            out = out + gate * _dot(h.astype(mdt), wd_ref[i])
        acc_scr[...] += _dot_tn(pb, out.astype(mdt))

    def full_tile(s, carry):
        tile(s * MOE_TILE, MOE_TILE)
        return carry

    nfull = count // MOE_TILE
    rem = count - nfull * MOE_TILE
    lax.fori_loop(0, nfull, full_tile, 0)

    @pl.when(rem > MOE_TAIL)
    def _():
        tile(nfull * MOE_TILE, MOE_TILE)

    @pl.when(jnp.logical_and(rem > 0, rem <= MOE_TAIL))
    def _():
        tile(nfull * MOE_TILE, MOE_TAIL)

    @pl.when(g == N_GROUPS - 1)
    def _():
        o_ref[...] = _layer_norm(ALPHA * x_ref[...] + acc_scr[...], g_ref[...], b_ref[...])


def _moe_ln_grouped(x, rwt, rbt, wg, wu, wd, g, b):
    n, d = x.shape
    tb = _row_block(n, MOE_BLOCK)
    f = wg.shape[2]
    tri = jnp.triu(jnp.ones((tb, tb), BF16))
    once = pl.Buffered(1)
    return pl.pallas_call(
        _moe_group_kernel,
        grid=(n // tb, N_GROUPS),
        in_specs=[pl.BlockSpec((tb, d), lambda i, q: (i, 0), pipeline_mode=once),
                  pl.BlockSpec((N_EXPERTS, d), lambda i, q: (0, 0), pipeline_mode=once),
                  pl.BlockSpec((LANES, LANES), lambda i, q: (0, 0), pipeline_mode=once),
                  pl.BlockSpec((tb, tb), lambda i, q: (0, 0), pipeline_mode=once),
                  pl.BlockSpec((EXP_PER_GROUP, d, f), lambda i, q: (q, 0, 0)),
                  pl.BlockSpec((EXP_PER_GROUP, d, f), lambda i, q: (q, 0, 0)),
                  pl.BlockSpec((EXP_PER_GROUP, f, d), lambda i, q: (q, 0, 0)),
                  pl.BlockSpec((1, d), lambda i, q: (0, 0)),
                  pl.BlockSpec((1, d), lambda i, q: (0, 0))],
        out_specs=pl.BlockSpec((tb, d), lambda i, q: (i, 0), pipeline_mode=once),
        out_shape=jax.ShapeDtypeStruct((n, d), F32),
        scratch_shapes=[pltpu.VMEM((tb, d), wg.dtype), pltpu.VMEM((N_GROUPS, SUBLANES, tb), F32),
                        pltpu.VMEM((tb, d), F32)],
        compiler_params=_cparams("parallel", "arbitrary"),
        name="moe_group_ln",
    )(x, rwt, rbt, tri, wg, wu, wd, g, b)


def _mlstm_kernel(a_ref, s_ref, g_ref, c0_ref, n0_ref, m0_ref, y_ref, c1_ref, n1_ref, m1_ref,
                  c_scr, n_scr, m_scr):
    c = pl.program_id(1)
    L = a_ref.shape[0]
    mdt = y_ref.dtype

    @pl.when(c == 0)
    def _():
        c_scr[...] = c0_ref[...]
        n_scr[...] = n0_ref[...]
        m_scr[...] = m0_ref[...]

    S = s_ref[...]
    lane = lax.broadcasted_iota(I32, S.shape, 1)
    is_f = jnp.logical_and(lane >= S_FG, lane < S_FG + H_A)
    lf = jnp.where(is_f, jax.nn.log_sigmoid(S), 0.0)
    row = lax.broadcasted_iota(I32, (L, L), 0)
    col = lax.broadcasted_iota(I32, (L, L), 1)
    causal = row >= col
    Fs = jnp.dot(causal.astype(F32), lf, precision=lax.Precision.HIGHEST, preferred_element_type=F32)
    Fa = pltpu.roll(Fs, LANES - (S_FG - S_IG), 1)
    AT = jnp.transpose(S - Fa)

    for h in range(H_A):
        ig = S[:, S_IG + h:S_IG + h + 1]
        F = Fa[:, S_IG + h:S_IG + h + 1]
        a_row = AT[S_IG + h:S_IG + h + 1, :]
        m_prev = m_scr[h:h + 1, 0:1]
        cm = jnp.max(jnp.where(causal, a_row, NEG), axis=1, keepdims=True)
        m_t = F + jnp.maximum(m_prev, cm)
        dmat = jnp.exp(jnp.where(causal, (F - m_t) + a_row, NEG))
        inter = jnp.exp(F + m_prev - m_t)
        q = a_ref[:, h * DK_A:(h + 1) * DK_A].astype(mdt)
        kf = a_ref[:, W_A + h * DK_A:W_A + (h + 1) * DK_A]
        k = kf.astype(mdt)
        v = a_ref[:, 2 * W_A + h * DV_A:2 * W_A + (h + 1) * DV_A].astype(mdt)
        C = c_scr[h]
        n = n_scr[h:h + 1, :]
        s = _dot_nt(q, k) * dmat
        num = _dot(s.astype(mdt), v) + inter * _dot(q, C.astype(mdt))
        qn = jnp.sum(q.astype(F32) * n.astype(mdt).astype(F32), axis=1, keepdims=True)
        den = jnp.sum(s, axis=1, keepdims=True) + inter * qn
        hout = num / jnp.maximum(jnp.abs(den), jnp.exp(-m_t))
        o = a_ref[:, 3 * W_A + h * DV_A:3 * W_A + (h + 1) * DV_A]
        y = _head_norm(hout) * g_ref[:, h * DV_A:(h + 1) * DV_A] * jax.nn.sigmoid(o)
        y_ref[:, h * DV_A:(h + 1) * DV_A] = y.astype(y_ref.dtype)
        m_new = m_t[L - 1:L, :]
        F_last = F[L - 1:L, :]
        w_s = jnp.exp(F_last - F + ig - m_new)
        decay = jnp.exp(F_last + m_prev - m_new)
        kw = kf * w_s
        c_scr[h] = decay * C + _dot_tn(kw.astype(mdt), v)
        n_scr[h:h + 1, :] = decay * n + jnp.sum(kw, axis=0, keepdims=True)
        m_scr[h:h + 1, :] = jnp.broadcast_to(m_new, (1, LANES))

    @pl.when(c == pl.num_programs(1) - 1)
    def _():
        c1_ref[...] = c_scr[...]
        n1_ref[...] = n_scr[...]
        m1_ref[...] = m_scr[...]


def _mlstm(a, s, gnorm, c0, n0, m0b, nb, nc, odt):
    n = a.shape[0]
    L = CHUNK
    return pl.pallas_call(
        _mlstm_kernel,
        grid=(nb, nc),
        in_specs=[pl.BlockSpec((L, 4 * W_A), lambda b, c: (b * nc + c, 0)),
                  pl.BlockSpec((L, LANES), lambda b, c: (b * nc + c, 0)),
                  pl.BlockSpec((1, W_A), lambda b, c: (0, 0)),
                  pl.BlockSpec((None, H_A, DK_A, DV_A), lambda b, c: (b, 0, 0, 0)),
                  pl.BlockSpec((None, H_A, DK_A), lambda b, c: (b, 0, 0)),
                  pl.BlockSpec((None, H_A, LANES), lambda b, c: (b, 0, 0))],
        out_specs=[pl.BlockSpec((L, W_A), lambda b, c: (b * nc + c, 0)),
                   pl.BlockSpec((None, H_A, DK_A, DV_A), lambda b, c: (b, 0, 0, 0)),
                   pl.BlockSpec((None, H_A, DK_A), lambda b, c: (b, 0, 0)),
                   pl.BlockSpec((None, H_A, LANES), lambda b, c: (b, 0, 0))],
        out_shape=[jax.ShapeDtypeStruct((n, W_A), odt),
                   jax.ShapeDtypeStruct((nb, H_A, DK_A, DV_A), F32),
                   jax.ShapeDtypeStruct((nb, H_A, DK_A), F32),
                   jax.ShapeDtypeStruct((nb, H_A, LANES), F32)],
        scratch_shapes=[pltpu.VMEM((H_A, DK_A, DV_A), F32), pltpu.VMEM((H_A, DK_A), F32),
                        pltpu.VMEM((H_A, LANES), F32)],
        compiler_params=_cparams("parallel", "arbitrary"),
        name="mlstm",
    )(a, s, gnorm, c0, n0, m0b)


def _retention_kernel(d_ref, cos_ref, sin_ref, dm_ref, dec_ref, g_ref, s0_ref, y_ref, s1_ref, s_scr):
    c = pl.program_id(1)

    @pl.when(c == 0)
    def _():
        s_scr[...] = s0_ref[...]

    cos2 = cos_ref[...]
    sin2 = sin_ref[...]
    mdt = y_ref.dtype
    for h in range(H_D):
        qf = d_ref[:, h * DK_D:(h + 1) * DK_D]
        kf = d_ref[:, W_D + h * DK_D:W_D + (h + 1) * DK_D]
        q = qf * cos2 + pltpu.roll(qf, DK_D // 2, 1) * sin2
        k = (kf * cos2 + pltpu.roll(kf, DK_D // 2, 1) * sin2) * DK_D ** -0.5
        v = d_ref[:, 2 * W_D + h * DV_D:2 * W_D + (h + 1) * DV_D].astype(mdt)
        qdec = dec_ref[h, :, 0:1]
        kdec = dec_ref[h, :, 1:2]
        sdec = dec_ref[h, 0:1, 2:3]
        S = s_scr[h]
        qb = q.astype(mdt)
        att = _dot_nt(qb, k.astype(mdt)) * dm_ref[h]
        o = _dot(att.astype(mdt), v) + qdec * _dot(qb, S.astype(mdt))
        s_scr[h] = sdec * S + _dot_tn((k * kdec).astype(mdt), v)
        gt = d_ref[:, 3 * W_D + h * DV_D:3 * W_D + (h + 1) * DV_D]
        y = _head_norm(o) * g_ref[:, h * DV_D:(h + 1) * DV_D] * (gt * jax.nn.sigmoid(gt))
        y_ref[:, h * DV_D:(h + 1) * DV_D] = y.astype(y_ref.dtype)

    @pl.when(c == pl.num_programs(1) - 1)
    def _():
        s1_ref[...] = s_scr[...]


def _retention(d, cos2, sin2, dmat, dec, gnorm, s0, nb, nc, odt):
    n = d.shape[0]
    L = CHUNK
    return pl.pallas_call(
        _retention_kernel,
        grid=(nb, nc),
        in_specs=[pl.BlockSpec((L, 4 * W_D), lambda b, c: (b * nc + c, 0)),
                  pl.BlockSpec((L, DK_D), lambda b, c: (c, 0)),
                  pl.BlockSpec((L, DK_D), lambda b, c: (c, 0)),
                  pl.BlockSpec((H_D, L, L), lambda b, c: (0, 0, 0)),
                  pl.BlockSpec((H_D, L, LANES), lambda b, c: (0, 0, 0)),
                  pl.BlockSpec((1, W_D), lambda b, c: (0, 0)),
                  pl.BlockSpec((None, H_D, DK_D, DV_D), lambda b, c: (b, 0, 0, 0))],
        out_specs=[pl.BlockSpec((L, W_D), lambda b, c: (b * nc + c, 0)),
                   pl.BlockSpec((None, H_D, DK_D, DV_D), lambda b, c: (b, 0, 0, 0))],
        out_shape=[jax.ShapeDtypeStruct((n, W_D), odt),
                   jax.ShapeDtypeStruct((nb, H_D, DK_D, DV_D), F32)],
        scratch_shapes=[pltpu.VMEM((H_D, DK_D, DV_D), F32)],
        compiler_params=_cparams("parallel", "arbitrary"),
        name="retention",
    )(d, cos2, sin2, dmat, dec, gnorm, s0)


def _shift_rows(x, prev, j):
    tb = x.shape[0]
    xs = pltpu.roll(x, j, 0)
    pr = pltpu.roll(prev, j, 0)
    row = lax.broadcasted_iota(I32, pr.shape, 0)
    first = jnp.where(row < j, pr, xs[0:SUBLANES])
    if tb == SUBLANES:
        return first
    return jnp.concatenate([first, xs[SUBLANES:]], axis=0)


def _rglru_kernel(c_ref, cw_ref, cb_ref, wa_ref, ba_ref, wx_ref, bx_ref, lam_ref, conv0_ref, h0_ref,
                  y_ref, hl_ref, prev_scr, h_scr):
    t = pl.program_id(1)
    tb = c_ref.shape[0]

    @pl.when(t == 0)
    def _():
        prev_scr[...] = conv0_ref[...]
        h_scr[...] = h0_ref[...]

    x = c_ref[:, 0:W_C]
    gate = c_ref[:, W_C:2 * W_C]
    prev = prev_scr[...]
    xc = x * cw_ref[CONV_W - 1:CONV_W, :] + cb_ref[...]
    for j in range(1, CONV_W):
        xc = xc + _shift_rows(x, prev, j) * cw_ref[CONV_W - 1 - j:CONV_W - j, :]
    prev_scr[...] = x[tb - SUBLANES:tb]

    xb = xc.astype(wa_ref.dtype)
    r = jax.nn.sigmoid(_dot(xb, wa_ref[...]) + ba_ref[...])
    i = jax.nn.sigmoid(_dot(xb, wx_ref[...]) + bx_ref[...])
    log_a = -LRU_C * r * jax.nn.softplus(-lam_ref[...])
    A = jnp.exp(log_a)
    th = jnp.tanh(log_a)
    U = jnp.sqrt(-2.0 * th / (1.0 - th)) * (i * xc)
    row = lax.broadcasted_iota(I32, (tb, W_C), 0)
    d = 1
    while d < tb:
        keep = row >= d
        U = jnp.where(keep, U + A * pltpu.roll(U, d, 0), U)
        A = jnp.where(keep, A * pltpu.roll(A, d, 0), A)
        d *= 2
    h = U + A * h_scr[0:1, :]
    h_scr[...] = jnp.broadcast_to(h[tb - 1:tb, :], (SUBLANES, W_C))
    y_ref[...] = (h * jax.nn.gelu(gate)).astype(y_ref.dtype)

    @pl.when(t == pl.num_programs(1) - 1)
    def _():
        hl_ref[...] = h_scr[...]


def _rglru(c, cw, cb, wa, ba, wx, bx, lam, conv0, h0, nb, nt, tb, odt):
    n = c.shape[0]
    vec = pl.BlockSpec((1, W_C), lambda b, t: (0, 0))
    mat = pl.BlockSpec((W_C, W_C), lambda b, t: (0, 0))
    st = pl.BlockSpec((None, SUBLANES, W_C), lambda b, t: (b, 0, 0))
    return pl.pallas_call(
        _rglru_kernel,
        grid=(nb, nt),
        in_specs=[pl.BlockSpec((tb, 2 * W_C), lambda b, t: (b * nt + t, 0)),
                  pl.BlockSpec((CONV_W, W_C), lambda b, t: (0, 0)), vec, mat, vec, mat, vec, vec, st, st],
        out_specs=[pl.BlockSpec((tb, W_C), lambda b, t: (b * nt + t, 0)), st],
        out_shape=[jax.ShapeDtypeStruct((n, W_C), odt), jax.ShapeDtypeStruct((nb, SUBLANES, W_C), F32)],
        scratch_shapes=[pltpu.VMEM((SUBLANES, W_C), F32), pltpu.VMEM((SUBLANES, W_C), F32)],
        compiler_params=_cparams("parallel", "arbitrary"),
        name="rglru",
    )(c, cw, cb, wa, ba, wx, bx, lam, conv0, h0)


KEY_NEG_INF = -2139095041


def _code_to_score(code):
    code = jnp.maximum(code, KEY_NEG_INF)
    return lax.bitcast_convert_type(code ^ ((code >> 31) & 0x7FFFFFFF), F32)


def _kth_largest_key(count_ge, k, rows):
    kf = float(k)
    zero = jnp.zeros((rows, 1), I32)
    t0 = jnp.where(count_ge(_code_to_score(zero)) >= kf, zero, jnp.full((rows, 1), INT_MIN, I32))

    def body(bi, t):
        cand = t + jnp.left_shift(jnp.int32(1), 30 - bi)
        return jnp.where(count_ge(_code_to_score(cand)) >= kf, cand, t)

    return _code_to_score(lax.fori_loop(0, 31, body, t0))


def _first_positions(count_eq_before, need, nbits, rows):
    def body(bi, x):
        cand = x + jnp.left_shift(jnp.int32(1), nbits - 1 - bi)
        return jnp.where(count_eq_before(cand) < need, cand, x)

    return lax.fori_loop(0, nbits, body, jnp.zeros((rows, 1), I32))


def _qi_heads(qi):
    lane = lax.broadcasted_iota(I32, (qi.shape[0], LANES), 1)
    low = lane < D_IDX
    out = []
    for h in range(H_IDX):
        pair = qi[:, (h // 2) * LANES:(h // 2 + 1) * LANES]
        if h % 2:
            pair = pltpu.roll(pair, D_IDX, 1)
        out.append(jnp.where(low, pair, 0.0).astype(BF16))
    return out


def _dsa_prompt_kernel(q_ref, s_ref, ki_ref, kb_ref, vb_ref, o_ref, key_scr, bias_scr, x_scr, thr_scr, lg_scr,
                       *, q_off, kt_size, topk):
    i = pl.program_id(1)
    QB = q_ref.shape[0]
    KT = kt_size
    nkt = key_scr.shape[0]
    lk = nkt * KT
    q0 = q_off + i * QB
    qih = _qi_heads(q_ref[:, W_B:2 * W_B])
    wi = s_ref[:, S_WI:S_WI + H_IDX]
    wcols = [wi[:, h:h + 1] for h in range(H_IDX)]
    rowpos = q0 + lax.broadcasted_iota(I32, (QB, KT), 0)
    colpos = lax.broadcasted_iota(I32, (QB, KT), 1)

    def score_tile(kt, carry):
        k0 = pl.multiple_of(kt * KT, KT)
        ki = ki_ref[pl.ds(k0, KT), :].astype(BF16)
        sc = jnp.zeros((QB, KT), F32)
        for h in range(H_IDX):
            sc = sc + wcols[h] * jnp.maximum(_dot_nt(qih[h], ki), 0.0)
        sc = jnp.where(colpos + k0 <= rowpos, sc, -jnp.inf)
        key_scr[kt] = sc
        return carry

    lax.fori_loop(0, nkt, score_tile, 0, unroll=2)

    lanepos = lax.broadcasted_iota(I32, (QB, LANES), 1)
    x_scr[...] = jnp.full(x_scr.shape, lk, I32)

    SR = min(QB, CHUNK)
    lanepos_s = lax.broadcasted_iota(I32, (SR, LANES), 1)
    for r0 in range(0, QB, SR):
        part = slice(r0, r0 + SR)

        def count(pred, part=part):
            def body(kt, acc):
                tile = key_scr[kt, part, :]
                k0 = kt * KT
                for j in range(KT // LANES):
                    sl = slice(j * LANES, (j + 1) * LANES)
                    acc = acc + jnp.where(pred(tile[:, sl], lanepos_s + (k0 + j * LANES)), 1.0, 0.0)
                return acc
            acc = lax.fori_loop(0, nkt, body, jnp.zeros((SR, LANES), F32))
            return jnp.sum(acc, axis=1, keepdims=True)

        thr_p = _kth_largest_key(lambda t: count(lambda key, pos: key >= t), topk, SR)
        thr_scr[part, :] = jnp.broadcast_to(thr_p, (SR, LANES))
        cnt_ge = count(lambda key, pos: key >= thr_p)

        @pl.when(jnp.max(cnt_ge) > float(topk))
        def _(count=count, thr_p=thr_p, part=part):
            need = float(topk) - count(lambda key, pos: key > thr_p)
            nbits = max(1, (lk - 1).bit_length())
            x = _first_positions(
                lambda c: count(lambda key, pos: jnp.logical_and(key == thr_p, pos < c)), need, nbits, SR)
            x_scr[part, :] = jnp.broadcast_to(x, (SR, LANES))

    thr = thr_scr[:, 0:1]
    xlim = x_scr[:, 0:1]

    def bias_tile(kt, carry):
        key = key_scr[kt]
        pos = colpos + kt * KT
        sel = jnp.logical_or(key > thr, jnp.logical_and(key == thr, pos <= xlim))
        bias_scr[kt] = jnp.where(jnp.logical_and(sel, pos <= rowpos), 0.0, NEG)
        return carry

    lax.fori_loop(0, nkt, bias_tile, 0)

    low = lanepos < DH_B
    ngrp = KT // LANES

    def group_fold(op, acc, x):
        for g in range(ngrp):
            acc = op(acc, x[:, g * LANES:(g + 1) * LANES])
        return acc

    for j in range(H_B // 2):
        qpair = q_ref[:, j * LANES:(j + 1) * LANES] * DH_B ** -0.5
        q0m = jnp.where(low, qpair, 0.0).astype(BF16)
        q1m = jnp.where(low, 0.0, qpair).astype(BF16)

        def pass1(kt, carry, q0m=q0m, q1m=q1m, j=j):
            mx0, mx1 = carry
            k0 = pl.multiple_of(kt * KT, KT)
            kk = kb_ref[pl.ds(k0, KT), j * LANES:(j + 1) * LANES]
            bias = bias_scr[kt]
            lg0 = _dot_nt(q0m, kk) + bias
            lg1 = _dot_nt(q1m, kk) + bias
            lg_scr[0, kt] = lg0
            lg_scr[1, kt] = lg1
            return group_fold(jnp.maximum, mx0, lg0), group_fold(jnp.maximum, mx1, lg1)

        neg = jnp.full((QB, LANES), NEG, F32)
        mx0, mx1 = lax.fori_loop(0, nkt, pass1, (neg, neg), unroll=2)
        m0 = jnp.max(mx0, axis=1, keepdims=True)
        m1 = jnp.max(mx1, axis=1, keepdims=True)

        def pass2(kt, carry, m0=m0, m1=m1, j=j):
            l0, l1, a0, a1 = carry
            k0 = pl.multiple_of(kt * KT, KT)
            vv = vb_ref[pl.ds(k0, KT), j * LANES:(j + 1) * LANES]
            p0 = jnp.exp(lg_scr[0, kt] - m0)
            p1 = jnp.exp(lg_scr[1, kt] - m1)
            a0 = a0 + _dot(p0.astype(BF16), vv)
            a1 = a1 + _dot(p1.astype(BF16), vv)
            return group_fold(jnp.add, l0, p0), group_fold(jnp.add, l1, p1), a0, a1

        zero = jnp.zeros((QB, LANES), F32)
        l0, l1, a0, a1 = lax.fori_loop(0, nkt, pass2, (zero, zero, zero, zero), unroll=2)
        out0 = a0 / jnp.sum(l0, axis=1, keepdims=True)
        out1 = a1 / jnp.sum(l1, axis=1, keepdims=True)
        o_ref[:, j * LANES:(j + 1) * LANES] = jnp.where(low, out0, out1).astype(o_ref.dtype)


def _dsa_prompt(bq, s, kb, vb, nb, t):
    n = bq.shape[0]
    QB = math.gcd(t, DSA_QB)
    topk = min(TOPK_MAX, t // 4)
    nseg = 1
    for cand in (8, 4, 2):
        if t % cand == 0 and (t // cand) % 512 == 0:
            nseg = cand
            break
    seg = t // nseg
    KT = 512 if seg % 512 == 0 else seg
    s3 = s.reshape(nb, t, LANES)
    kb3 = kb.reshape(nb, t, W_B)
    vb3 = vb.reshape(nb, t, W_B)
    nqb = seg // QB
    outs = []
    for g in range(nseg):
        lk = (g + 1) * seg
        nkt = lk // KT
        row_blk = functools.partial(lambda b, i, g: (b * (t // QB) + g * nqb + i, 0), g=g)
        outs.append(pl.pallas_call(
            functools.partial(_dsa_prompt_kernel, q_off=g * seg, kt_size=KT, topk=topk),
            grid=(nb, nqb),
            in_specs=[pl.BlockSpec((QB, 2 * W_B), row_blk),
                      pl.BlockSpec((QB, LANES), row_blk),
                      pl.BlockSpec((None, lk, LANES), lambda b, i: (b, 0, 0)),
                      pl.BlockSpec((None, lk, W_B), lambda b, i: (b, 0, 0)),
                      pl.BlockSpec((None, lk, W_B), lambda b, i: (b, 0, 0))],
            out_specs=pl.BlockSpec((None, QB, W_B), lambda b, i: (b, i, 0)),
            out_shape=jax.ShapeDtypeStruct((nb, seg, W_B), BF16),
            scratch_shapes=[pltpu.VMEM((nkt, QB, KT), F32), pltpu.VMEM((nkt, QB, KT), F32),
                            pltpu.VMEM((QB, LANES), I32), pltpu.VMEM((QB, LANES), F32),
                            pltpu.VMEM((2, nkt, QB, KT), F32)],
            compiler_params=_cparams("parallel", "arbitrary"),
            name=f"dsa_prompt_{g}",
        )(bq, s, s3, kb3, vb3))
    return jnp.concatenate(outs, axis=1).reshape(n, W_B)


SCORE_PAGES = 16
ATTN_PAGES = 16


def _dsa_sample_score_kernel(pt_ref, q_ref, s_ref, snew_ref, *rest):
    page_refs, o_ref = rest[:-1], rest[-1]
    p = pl.program_id(1)
    last = pl.num_programs(1) - 1
    T = q_ref.shape[0]
    qi = q_ref[:, W_B:2 * W_B]
    wi = s_ref[:, S_WI:S_WI + H_IDX]
    qst = jnp.concatenate([qi[:, h * D_IDX:(h + 1) * D_IDX] for h in range(H_IDX)], axis=0)
    wcol = jnp.concatenate([wi[:, h:h + 1] for h in range(H_IDX)], axis=0)

    def scores(dots):
        r = wcol * jnp.maximum(dots, 0.0)
        sc = r[0:T]
        for h in range(1, H_IDX):
            sc = sc + r[h * T:(h + 1) * T]
        return sc

    @pl.when(p < last)
    def _():
        for j, page_ref in enumerate(page_refs):
            o_ref[j] = scores(_dot(qst, page_ref[...]))

    @pl.when(p == last)
    def _():
        sc = scores(_dot_nt(qst, snew_ref[:, S_KI:S_KI + D_IDX]))
        r = lax.broadcasted_iota(I32, sc.shape, 0)
        c = lax.broadcasted_iota(I32, sc.shape, 1)
        o_ref[0] = jnp.where(c <= r, sc, -jnp.inf)
        for j in range(1, len(page_refs)):
            o_ref[j] = jnp.full(sc.shape, -jnp.inf, F32)


def _dsa_sample_attn_kernel(pt_ref, q_ref, sc_ref, knew_ref, vnew_ref, *rest, topk, npg, G):
    kpages, vpages, o_ref = rest[:G], rest[G:2 * G], rest[2 * G]
    thr_scr, x_scr, m_scr, l_scr, acc_scr = rest[2 * G + 1:]
    p = pl.program_id(1)
    last = pl.num_programs(1) - 1
    T = q_ref.shape[0]
    R = H_B * T
    ltot = sc_ref.shape[0] * PAGE_SIZE

    @pl.when(p == 0)
    def _():
        key = sc_ref[...]
        pos = (lax.broadcasted_iota(I32, key.shape, 0) * PAGE_SIZE
               + lax.broadcasted_iota(I32, key.shape, 2))

        def count(pred):
            per_lane = jnp.sum(jnp.where(pred(key, pos), 1.0, 0.0), axis=0)
            return jnp.sum(per_lane, axis=1, keepdims=True)

        thr = _kth_largest_key(lambda t: count(lambda k_, p_: k_ >= t), topk, T)
        cnt_ge = count(lambda k_, p_: k_ >= thr)
        thr_scr[...] = jnp.broadcast_to(thr, thr_scr.shape)
        x_scr[...] = jnp.full(x_scr.shape, ltot, I32)

        @pl.when(jnp.max(cnt_ge) > float(topk))
        def _():
            cnt_gt = count(lambda k_, p_: k_ > thr)
            nbits = max(1, (ltot - 1).bit_length())
            x = _first_positions(lambda c: count(lambda k_, p_: jnp.logical_and(k_ == thr, p_ < c)),
                                 float(topk) - cnt_gt, nbits, T)
            x_scr[...] = jnp.broadcast_to(x, x_scr.shape)

        m_scr[...] = jnp.full(m_scr.shape, NEG, F32)
        l_scr[...] = jnp.zeros_like(l_scr)
        acc_scr[...] = jnp.zeros_like(acc_scr)

    q = q_ref[:, 0:W_B] * DH_B ** -0.5
    qrep = jnp.concatenate([q] * H_B, axis=0)
    rr = lax.broadcasted_iota(I32, (R, W_B), 0) // T
    cc = lax.broadcasted_iota(I32, (R, W_B), 1) // DH_B
    diag = rr == cc
    qbd = jnp.where(diag, qrep, 0.0)
    thr = thr_scr[:, 0:1]
    xlim = x_scr[:, 0:1]

    def selected(page):
        key = sc_ref[page]
        pos = page * PAGE_SIZE + lax.broadcasted_iota(I32, key.shape, 1)
        return jnp.logical_or(key > thr, jnp.logical_and(key == thr, pos <= xlim))

    def masked(logits, valid):
        bias = jnp.where(valid, 0.0, NEG)
        return logits + jnp.concatenate([bias] * H_B, axis=0)

    def update(lgs, pv):
        mx = lgs[0]
        for lg in lgs[1:]:
            mx = jnp.maximum(mx, lg)
        m = m_scr[:, 0:1]
        m_new = jnp.maximum(m, jnp.max(mx, axis=1, keepdims=True))
        alpha = jnp.exp(m - m_new)
        ps = [jnp.exp(lg - m_new) for lg in lgs]
        psum = ps[0]
        for pr in ps[1:]:
            psum = psum + pr
        acc = pv(0, ps[0])
        for j in range(1, len(ps)):
            acc = acc + pv(j, ps[j])
        l_scr[...] = jnp.broadcast_to(alpha * l_scr[:, 0:1] + jnp.sum(psum, axis=1, keepdims=True), l_scr.shape)
        acc_scr[...] = alpha * acc_scr[...] + acc
        m_scr[...] = jnp.broadcast_to(m_new, m_scr.shape)

    q2 = _split2(qbd)

    @pl.when(p < last)
    def _():
        lgs = [masked(_dot3(q2, _split2(kpages[j][...]), _dot), selected(p * G + j)) for j in range(G)]
        update(lgs, lambda j, pr: _dot3(_split2(pr), _split2(vpages[j][...]), _dot_nt))

    @pl.when(p == last)
    def _():
        sel = selected(npg)
        r = lax.broadcasted_iota(I32, sel.shape, 0)
        c = lax.broadcasted_iota(I32, sel.shape, 1)
        lg = masked(_dot3(q2, _split2(knew_ref[...]), _dot_nt), jnp.logical_and(sel, c <= r))
        update([lg], lambda j, pr: _dot3(_split2(pr), _split2(vnew_ref[...]), _dot))
        out = jnp.where(diag, acc_scr[...] / l_scr[:, 0:1], 0.0)
        res = out[0:T]
        for h in range(1, H_B):
            res = res + out[h * T:(h + 1) * T]
        o_ref[...] = res.astype(o_ref.dtype)


def _dsa_sample(bq, s, k, v, pool_k, pool_v, pool_ki, page_table, nb, t):
    n_pool = pool_k.shape[1]
    pool_k = jnp.transpose(pool_k, (0, 1, 3, 4, 2)).reshape(1, n_pool, W_B, PAGE_SIZE)
    pool_v = jnp.transpose(pool_v, (0, 1, 3, 4, 2)).reshape(1, n_pool, W_B, PAGE_SIZE)
    pool_ki = jnp.transpose(pool_ki, (0, 1, 3, 2))
    npg = page_table.shape[1]
    past = npg * PAGE_SIZE
    topk = min(TOPK_MAX, (past + t) // 4)
    gs, ga = math.gcd(npg, SCORE_PAGES), math.gcd(npg, ATTN_PAGES)
    pad = PAGE_SIZE - t
    s_new = jnp.pad(s.reshape(nb, t, LANES), ((0, 0), (0, pad), (0, 0)))
    k_new = jnp.pad(k.reshape(nb, t, W_B), ((0, 0), (0, pad), (0, 0)))
    v_new = jnp.pad(v.reshape(nb, t, W_B), ((0, 0), (0, pad), (0, 0)))
    ptot = npg + gs

    def page_spec(tail, j, group):
        zeros = (0,) * len(tail)
        last_group = npg // group - 1
        return pl.BlockSpec((None, None) + tail,
                            lambda b, p, pt: (0, pt[b, jnp.minimum(p, last_group) * group + j]) + zeros)

    scores = pl.pallas_call(
        _dsa_sample_score_kernel,
        grid_spec=pltpu.PrefetchScalarGridSpec(
            num_scalar_prefetch=1,
            grid=(nb, npg // gs + 1),
            in_specs=[pl.BlockSpec((t, 2 * W_B), lambda b, p, pt: (b, 0)),
                      pl.BlockSpec((t, LANES), lambda b, p, pt: (b, 0)),
                      pl.BlockSpec((None, PAGE_SIZE, LANES), lambda b, p, pt: (b, 0, 0))]
            + [page_spec((D_IDX, PAGE_SIZE), j, gs) for j in range(gs)],
            out_specs=pl.BlockSpec((None, gs, t, PAGE_SIZE), lambda b, p, pt: (b, p, 0, 0))),
        out_shape=jax.ShapeDtypeStruct((nb, ptot, t, PAGE_SIZE), F32),
        compiler_params=_cparams("parallel", "arbitrary"),
        name="dsa_sample_scores",
    )(page_table, bq, s, s_new, *([pool_ki] * gs))

    kv_tail = (W_B, PAGE_SIZE)
    return pl.pallas_call(
        functools.partial(_dsa_sample_attn_kernel, topk=topk, npg=npg, G=ga),
        grid_spec=pltpu.PrefetchScalarGridSpec(
            num_scalar_prefetch=1,
            grid=(nb, npg // ga + 1),
            in_specs=[pl.BlockSpec((t, 2 * W_B), lambda b, p, pt: (b, 0)),
                      pl.BlockSpec((None, ptot, t, PAGE_SIZE), lambda b, p, pt: (b, 0, 0, 0)),
                      pl.BlockSpec((None, PAGE_SIZE, W_B), lambda b, p, pt: (b, 0, 0)),
                      pl.BlockSpec((None, PAGE_SIZE, W_B), lambda b, p, pt: (b, 0, 0))]
            + [page_spec(kv_tail, j, ga) for j in range(ga)]
            + [page_spec(kv_tail, j, ga) for j in range(ga)],
            out_specs=pl.BlockSpec((t, W_B), lambda b, p, pt: (b, 0)),
            scratch_shapes=[pltpu.VMEM((t, LANES), F32), pltpu.VMEM((t, LANES), I32),
                            pltpu.VMEM((H_B * t, LANES), F32), pltpu.VMEM((H_B * t, LANES), F32),
                            pltpu.VMEM((H_B * t, W_B), F32)]),
        out_shape=jax.ShapeDtypeStruct((nb * t, W_B), F32),
        compiler_params=_cparams("parallel", "arbitrary"),
        name="dsa_sample_attn",
    )(page_table, bq, scores, k_new, v_new, *([pool_k] * ga), *([pool_v] * ga))


def _moe(x, prm, layer):
    wg, wu, wd = prm["moe_wg"][layer], prm["moe_wu"][layer], prm["moe_wd"][layer]
    g, b = prm["ln_g"][layer, 1], prm["ln_b"][layer, 1]
    if wg.dtype == BF16:
        return _moe_ln_grouped(x, prm["router_wt"], prm["router_bt"], wg, wu, wd, g, b)
    return _moe_ln(x, prm["router_w"], prm["router_b"], wg, wu, wd, g, b)


def _pad_chunks(a, nb, t, fill=None):
    c = a.shape[1]
    a3 = a.reshape(nb, t, c)
    if fill is None:
        a3 = jnp.pad(a3, ((0, 0), (0, CHUNK - t), (0, 0)))
    else:
        a3 = jnp.concatenate([a3, jnp.broadcast_to(fill, (nb, CHUNK - t, c))], axis=1)
    return a3.reshape(nb * CHUNK, c)


def _retention_tables(t_true, pos0, t_pad):
    L = CHUNK
    lt = min(L, t_true)
    lg = jnp.log1p(-jnp.exp2(-5.0 - jnp.arange(H_D, dtype=F32)))
    j = jnp.arange(L, dtype=F32)
    causal = jnp.tril(jnp.ones((L, L), dtype=bool))
    dmat = jnp.exp(jnp.where(causal, (j[:, None] - j[None, :]) * lg[:, None, None], -jnp.inf))
    qdec = jnp.exp((j + 1.0) * lg[:, None])
    kdec = jnp.where(j < lt, jnp.exp((lt - 1.0 - j) * lg[:, None]), 0.0)
    sdec = jnp.broadcast_to(jnp.exp(lt * lg)[:, None], (H_D, L))
    dec = jnp.zeros((H_D, L, LANES), F32)
    dec = dec.at[:, :, 0].set(qdec).at[:, :, 1].set(kdec).at[:, :, 2].set(sdec)
    half = DK_D // 2
    freq = ROPE_BASE ** (-jnp.arange(half, dtype=F32) / half)
    pos = (pos0 + jnp.arange(t_pad)).astype(F32)
    ang = pos[:, None] * freq[None, :]
    cos, sin = jnp.cos(ang), jnp.sin(ang)
    return dmat, dec, jnp.concatenate([cos, cos], -1), jnp.concatenate([-sin, sin], -1)


def _forward(x3, st, prm):
    nb, t, d = x3.shape
    n = nb * t
    x = x3.reshape(n, d)
    short = t < CHUNK
    t_pad = CHUNK if short else t
    nc = t_pad // CHUNK
    new = {}
    odt = prm["w_in_e"].dtype

    a, bq, k, v, s, kb, vb = _inproj_e(x, prm["w_in_e"], prm["bias_e"])
    if st is None:
        c0 = jnp.zeros((nb, H_A, DK_A, DV_A), F32)
        n0 = jnp.zeros((nb, H_A, DK_A), F32)
        m0 = jnp.zeros((nb, H_A), F32)
    else:
        c0, n0, m0 = st["mlstm_C"][0], st["mlstm_n"][0], st["mlstm_m"][0]
    m0b = jnp.broadcast_to(m0[:, :, None], (nb, H_A, LANES))
    if short:
        lane = jnp.arange(LANES)
        fill = jnp.where((lane >= S_IG) & (lane < S_IG + H_A), NEG,
                         jnp.where((lane >= S_FG) & (lane < S_FG + H_A), 1e4, 0.0)).astype(F32)
        a_m, s_m = _pad_chunks(a, nb, t), _pad_chunks(s, nb, t, fill)
    else:
        a_m, s_m = a, s
    ya, c1, n1, m1 = _mlstm(a_m, s_m, prm["a_norm_g"], c0, n0, m0b, nb, nc, odt)
    if short:
        ya = ya.reshape(nb, CHUNK, W_A)[:, :t].reshape(n, W_A)
    if st is None:
        yb = _dsa_prompt(bq, s, kb, vb, nb, t)
    else:
        yb = _dsa_sample(bq, s, k, v, st["pool_k"], st["pool_v"], st["pool_ki"], st["page_table"], nb, t)
    new["mlstm_C"], new["mlstm_n"], new["mlstm_m"] = c1[None], n1[None], m1[None, :, :, 0]
    new["k"] = k.reshape(1, nb, t, H_B, DH_B)
    new["v"] = v.reshape(1, nb, t, H_B, DH_B)
    new["kidx"] = s[:, S_KI:S_KI + D_IDX].reshape(1, nb, t, D_IDX)
    x = _outproj_ln(ya, yb, prm["w_out_e"], x, prm["ln_g"][0, 0], prm["ln_b"][0, 0])
    x = _moe(x, prm, 0)

    c, dd = _inproj_o(x, prm["w_in_o"])
    if st is None:
        conv0 = jnp.zeros((nb, SUBLANES, W_C), F32)
        h0 = jnp.zeros((nb, SUBLANES, W_C), F32)
        s0 = jnp.zeros((nb, H_D, DK_D, DV_D), F32)
        pos0 = 0
    else:
        conv0 = jnp.pad(st["conv"][0], ((0, 0), (SUBLANES - (CONV_W - 1), 0), (0, 0)))
        h0 = jnp.broadcast_to(st["lru_h"][0][:, None, :], (nb, SUBLANES, W_C))
        s0 = st["ret_S"][0]
        pos0 = st["page_table"].shape[1] * PAGE_SIZE
    tb = _row_block(t, 256)
    yc, hl = _rglru(c, prm["c_conv_w"], prm["c_conv_b"], prm["c_wa"], prm["c_ba"], prm["c_wx"], prm["c_bx"],
                    prm["c_lambda"], conv0, h0, nb, t // tb, tb, odt)
    dmat, dec, cos2, sin2 = _retention_tables(t, pos0, t_pad)
    d_m = _pad_chunks(dd, nb, t) if short else dd
    yd, s1 = _retention(d_m, cos2, sin2, dmat, dec, prm["d_norm_g"], s0, nb, nc, odt)
    if short:
        yd = yd.reshape(nb, CHUNK, W_D)[:, :t].reshape(n, W_D)
    new["conv"] = c[:, 0:W_C].reshape(nb, t, W_C)[None, :, t - (CONV_W - 1):]
    new["lru_h"] = hl[None, :, 0]
    new["ret_S"] = s1[None]
    x = _outproj_ln(yc, yd, prm["w_out_o"], x, prm["ln_g"][1, 0], prm["ln_b"][1, 0])
    x = _moe(x, prm, 1)
    return x.reshape(nb, t, d), new


def _block_diag(w):
    nblk, blk, _ = w.shape
    eye = jnp.eye(nblk, dtype=w.dtype)
    return (eye[:, None, :, None] * w[:, :, None, :]).reshape(nblk * blk, nblk * blk)


def _prepare_params(w_in_e, b_if_e, a_norm_g, w_out_e, w_in_o, c_conv_w, c_conv_b, c_wa, c_ba, c_wx, c_bx,
                    c_lambda, d_norm_g, w_out_o, router_w, router_bias, moe_w_gate, moe_w_up, moe_w_down,
                    ln_g, ln_b, wdt):
    d = w_in_e.shape[1]
    o = [0]
    for width in (W_A, W_A, W_A, W_A, 2 * H_A, W_B, W_B, W_B, H_IDX * D_IDX, D_IDX, H_IDX):
        o.append(o[-1] + width)
    we = w_in_e[0]
    col = lambda i: we[:, o[i]:o[i + 1]]
    slab = jnp.zeros((d, LANES), F32)
    slab = slab.at[:, S_KI:S_KI + D_IDX].set(col(9)).at[:, S_IG:S_IG + 2 * H_A].set(col(4))
    slab = slab.at[:, S_WI:S_WI + H_IDX].set(col(10))
    w_e = jnp.concatenate([col(0), col(1), col(2), col(3), col(5), col(8), col(6), col(7), slab], axis=1)
    bias_e = jnp.zeros((1, LANES), F32).at[0, S_IG:S_IG + 2 * H_A].set(b_if_e[0])
    rw = jnp.zeros((d, LANES), F32).at[:, :N_EXPERTS].set(router_w)
    rb = jnp.zeros((1, LANES), F32).at[0, :N_EXPERTS].set(router_bias)
    row = lambda a: a.reshape(1, -1).astype(F32)
    return {
        "w_in_e": w_e.astype(wdt), "bias_e": bias_e, "a_norm_g": row(a_norm_g[0]),
        "w_out_e": w_out_e[0].astype(wdt), "w_in_o": w_in_o[0].astype(wdt),
        "c_conv_w": c_conv_w[0].astype(F32), "c_conv_b": row(c_conv_b[0]),
        "c_wa": _block_diag(c_wa[0]).astype(wdt), "c_ba": row(c_ba[0]),
        "c_wx": _block_diag(c_wx[0]).astype(wdt), "c_bx": row(c_bx[0]),
        "c_lambda": row(c_lambda[0]), "d_norm_g": row(d_norm_g[0]), "w_out_o": w_out_o[0].astype(wdt),
        "router_w": rw, "router_b": rb,
        "router_wt": router_w.T.astype(F32), "router_bt": jnp.broadcast_to(rb.reshape(LANES, 1), (LANES, LANES)),
        "moe_wg": moe_w_gate.astype(wdt), "moe_wu": moe_w_up.astype(wdt), "moe_wd": moe_w_down.astype(wdt),
        "ln_g": ln_g.reshape(DEPTH, 2, 1, -1).astype(F32), "ln_b": ln_b.reshape(DEPTH, 2, 1, -1).astype(F32),
    }


def kernel(x_prompt, x_sample, state_mlstm_C, state_mlstm_n, state_mlstm_m, cache_k, cache_v, cache_kidx,
           page_table, state_conv, state_lru_h, state_ret_S, w_in_e, b_if_e, a_norm_g, w_out_e, w_in_o,
           c_conv_w, c_conv_b, c_wa, c_ba, c_wx, c_bx, c_lambda, d_norm_g, w_out_o, router_w, router_bias,
           moe_w_gate, moe_w_up, moe_w_down, ln_g, ln_b):
    weights = (w_in_e, b_if_e, a_norm_g, w_out_e, w_in_o, c_conv_w, c_conv_b, c_wa, c_ba, c_wx, c_bx, c_lambda,
               d_norm_g, w_out_o, router_w, router_bias, moe_w_gate, moe_w_up, moe_w_down, ln_g, ln_b)
    prm_prompt = _prepare_params(*weights, BF16)
    prm_sample = _prepare_params(*weights, F32)
    st = {"mlstm_C": state_mlstm_C, "mlstm_n": state_mlstm_n, "mlstm_m": state_mlstm_m,
          "pool_k": cache_k, "pool_v": cache_v, "pool_ki": cache_kidx, "page_table": page_table,
          "conv": state_conv, "lru_h": state_lru_h, "ret_S": state_ret_S}
    y_p, nsp = _forward(x_prompt, None, prm_prompt)
    y_s, nss = _forward(x_sample, st, prm_sample)
    names = ("mlstm_C", "mlstm_n", "mlstm_m", "k", "v", "kidx", "conv", "lru_h", "ret_S")
    return (y_p, y_s) + tuple(nsp[k] for k in names) + tuple(nss[k] for k in names)
```
